```python
import math
import jax, jax.numpy as jnp
from jax import lax
import numpy as np

D_MODEL = 1024
BATCH = 16
SEQ = 2048
DEPTH = 4

N_MIXERS = 3
PLE_DIM = 256
ROPE_THETA = 10000.0
NORM_EPS = 1e-6
POS_OFFSET_MAX = 4096

RET_HEADS = 4
RET_DK = D_MODEL // RET_HEADS
RET_DV = 2 * RET_DK
RET_CHUNK = 128

NSA_HEADS = 8
NSA_GROUPS = 2
NSA_DH = D_MODEL // NSA_HEADS
NSA_CMP_LEN = 32
NSA_CMP_STRIDE = 16
NSA_SEL_LEN = 64
NSA_TOP_N = 16
NSA_WINDOW = 512
NSA_CMP_HID = 256
NSA_SEL_Q_BLOCK = 16
NSA_WIN_Q_BLOCK = 128

S5_GROUP = 16
S5_GROUPS = D_MODEL // S5_GROUP
S5_STATE = 64

D_FF = 2816
N_EXPERTS = 8
TOP_K = 2
D_FF_EXPERT = 3584

kernel_name = "hybrid_retention_nsa_s5_moe_trunk"


def rms_norm(x, g):
    xf = x.astype(jnp.float32)
    y = xf * lax.rsqrt(jnp.mean(xf * xf, axis=-1, keepdims=True) + NORM_EPS)
    return (y * g.astype(jnp.float32)).astype(x.dtype)


def rope(x, pos):
    d = x.shape[-1]
    inv = ROPE_THETA ** (-jnp.arange(0, d, 2, dtype=jnp.float32) / d)
    ang = pos.astype(jnp.float32)[..., None] * inv
    cos = jnp.cos(ang)[..., None, :]
    sin = jnp.sin(ang)[..., None, :]
    xf = x.astype(jnp.float32)
    x1, x2 = xf[..., : d // 2], xf[..., d // 2:]
    return jnp.concatenate([x1 * cos - x2 * sin, x2 * cos + x1 * sin], axis=-1).astype(x.dtype)


def swiglu(h, w_gate_up, w_down):
    g, u = jnp.split(h @ w_gate_up, 2, axis=-1)
    return (jax.nn.silu(g) * u) @ w_down


def retention_mixer(xn, pos, w_in, gn, w_out):
    B, S, _ = xn.shape
    H, dk, dv, C = RET_HEADS, RET_DK, RET_DV, RET_CHUNK
    N = S // C
    q, k, v, g = jnp.split(xn @ w_in, [H * dk, 2 * H * dk, 2 * H * dk + H * dv], axis=-1)
    q = rope(q.reshape(B, S, H, dk), pos)
    k = rope(k.reshape(B, S, H, dk), pos) * (dk ** -0.5)
    v = v.reshape(B, S, H, dv)
    log_gamma = jnp.log1p(-jnp.exp2(-5.0 - jnp.arange(H, dtype=jnp.float32)))
    idx = jnp.arange(C, dtype=jnp.float32)
    diff = idx[:, None] - idx[None, :]
    d_mask = jnp.where(diff >= 0, jnp.exp(jnp.maximum(diff, 0.0) * log_gamma[:, None, None]), 0.0)
    q_decay = jnp.exp((idx + 1.0) * log_gamma[:, None])
    k_decay = jnp.exp((C - 1.0 - idx) * log_gamma[:, None])
    chunk_decay = jnp.exp(C * log_gamma)

    def chunks(t):
        return t.reshape(B, N, C, H, t.shape[-1]).transpose(1, 0, 3, 2, 4)

    def step(R, inp):
        qn, kn, vn = inp
        scores = jnp.einsum('bhid,bhjd->bhij', qn, kn) * d_mask
        intra = jnp.einsum('bhij,bhjv->bhiv', scores, vn)
        cross = jnp.einsum('bhid,bhdv->bhiv', qn, R) * q_decay[None, :, :, None]
        R_new = R * chunk_decay[None, :, None, None] + jnp.einsum(
            'bhjd,bhjv->bhdv', kn * k_decay[None, :, :, None], vn)
        return R_new, intra + cross

    R0 = jnp.zeros((B, H, dk, dv), jnp.float32)
    _, o = lax.scan(step, R0, (chunks(q), chunks(k), chunks(v)))
    o = o.transpose(1, 0, 3, 2, 4).reshape(B, S, H, dv)
    mu = jnp.mean(o, axis=-1, keepdims=True)
    var = jnp.mean(jnp.square(o - mu), axis=-1, keepdims=True)
    o = ((o - mu) * lax.rsqrt(var + NORM_EPS)).reshape(B, S, H * dv) * gn.astype(jnp.float32)
    return (o.astype(xn.dtype) * jax.nn.silu(g)) @ w_out


def nsa_mixer(xn, pos, w_in, q_norm, k_norm, cmp_pos, cmp_w1, cmp_w2, w_out):
    B, S, _ = xn.shape
    H, G, dh = NSA_HEADS, NSA_GROUPS, NSA_DH
    hg = H // G
    L, STR, SEL, WIN = NSA_CMP_LEN, NSA_CMP_STRIDE, NSA_SEL_LEN, NSA_WINDOW
    neg = jnp.finfo(jnp.float32).min
    scale = dh ** -0.5
    kvw = G * dh
    cuts = [int(c) for c in np.cumsum([H * dh, kvw, kvw, kvw, kvw, kvw, kvw])]
    q, kc, vc, ks, vs, kw, vw, gate = jnp.split(xn @ w_in, cuts, axis=-1)

    q = rope(rms_norm(q.reshape(B, S, H, dh), q_norm), pos).reshape(B, S, G, hg, dh)
    ks = rope(rms_norm(ks.reshape(B, S, G, dh), k_norm[1]), pos)
    kw = rope(rms_norm(kw.reshape(B, S, G, dh), k_norm[2]), pos)
    vs = vs.reshape(B, S, G, dh)
    vw = vw.reshape(B, S, G, dh)
    t_idx = jnp.arange(S)

    n_c = (S - L) // STR + 1
    blk = np.arange(n_c)[:, None] * STR + np.arange(L)[None, :]
    cmp_start = blk[:, 0]

    def compress(t, pos_emb, w1, w2):
        tb = t.reshape(B, S, G, dh)[:, blk] + pos_emb[None, None, :, None, :]
        tb = tb.transpose(0, 1, 3, 2, 4).reshape(B, n_c, G, L * dh)
        return jax.nn.gelu(tb @ w1) @ w2

    kc = compress(kc, cmp_pos[0], cmp_w1[0], cmp_w2[0])
    vc = compress(vc, cmp_pos[1], cmp_w1[1], cmp_w2[1])
    kc = rope(rms_norm(kc, k_norm[0]), pos[:, cmp_start])
    s_c = jnp.einsum('bsghd,bcgd->bghsc', q, kc).astype(jnp.float32) * scale
    valid_c = jnp.asarray(blk[:, -1])[None, :] <= t_idx[:, None]
    p_c = jax.nn.softmax(jnp.where(valid_c, s_c, neg), axis=-1)
    p_c = jnp.where(valid_c, p_c, 0.0)
    o_c = jnp.einsum('bghsc,bcgd->bsghd', p_c.astype(vc.dtype), vc)

    n_sel = S // SEL
    js = np.arange(n_sel)[None, :] * SEL
    cs = cmp_start[:, None]
    overlap = np.clip(np.minimum(cs + L, js + SEL) - np.maximum(cs, js), 0, None) / L
    M = jnp.asarray(overlap, dtype=jnp.float32)
    imp = jnp.einsum('bghsc,cj->bgsj', p_c, M)
    cur = (t_idx // SEL)[:, None]
    jb = jnp.arange(n_sel)[None, :]
    forced = (jb == 0) | (jb == cur) | (jb == cur - 1)
    imp = jnp.where(forced, jnp.inf, jnp.where(jb > cur, -jnp.inf, imp))
    _, sel_idx = lax.top_k(imp, min(NSA_TOP_N, n_sel))

    ks_b = ks.reshape(B, n_sel, SEL, G, dh).transpose(0, 3, 1, 2, 4)
    vs_b = vs.reshape(B, n_sel, SEL, G, dh).transpose(0, 3, 1, 2, 4)
    gather_blocks = jax.vmap(jax.vmap(lambda blocks, ix: blocks[ix]))

    QS = NSA_SEL_Q_BLOCK

    def sel_block(qb):
        q0 = qb * QS
        qblk = lax.dynamic_slice_in_dim(q, q0, QS, axis=1)
        ix = lax.dynamic_slice_in_dim(sel_idx, q0, QS, axis=2)
        t = q0 + jnp.arange(QS)
        k_sel = gather_blocks(ks_b, ix)
        v_sel = gather_blocks(vs_b, ix)
        s = jnp.einsum('btghd,bgtnkd->bghtnk', qblk, k_sel).astype(jnp.float32) * scale
        kpos = ix[..., None] * SEL + jnp.arange(SEL)
        mask = (kpos <= t[None, None, :, None, None])[:, :, None]
        s = jnp.where(mask, s, neg)
        sh = s.shape
        pr = jax.nn.softmax(s.reshape(sh[:4] + (-1,)), axis=-1).reshape(sh)
        return jnp.einsum('bghtnk,bgtnkd->btghd', pr.astype(v_sel.dtype), v_sel)

    o_s = lax.map(sel_block, jnp.arange(S // QS))
    o_s = o_s.transpose(1, 0, 2, 3, 4, 5).reshape(B, S, H, dh)

    QW = NSA_WIN_Q_BLOCK
    kw_p = jnp.pad(kw, ((0, 0), (WIN, 0), (0, 0), (0, 0)))
    vw_p = jnp.pad(vw, ((0, 0), (WIN, 0), (0, 0), (0, 0)))

    def win_block(qb):
        q0 = qb * QW
        qblk = lax.dynamic_slice_in_dim(q, q0, QW, axis=1)
        kwin = lax.dynamic_slice_in_dim(kw_p, q0, QW + WIN, axis=1)
        vwin = lax.dynamic_slice_in_dim(vw_p, q0, QW + WIN, axis=1)
        t = q0 + jnp.arange(QW)
        kpos = q0 - WIN + jnp.arange(QW + WIN)
        dist = t[:, None] - kpos[None, :]
        mask = (dist >= 0) & (dist < WIN) & (kpos >= 0)[None, :]
        s = jnp.einsum('btghd,bkgd->bghtk', qblk, kwin).astype(jnp.float32) * scale
        pr = jax.nn.softmax(jnp.where(mask, s, neg), axis=-1)
        return jnp.einsum('bghtk,bkgd->btghd', pr.astype(vwin.dtype), vwin)

    o_w = lax.map(win_block, jnp.arange(S // QW))
    o_w = o_w.transpose(1, 0, 2, 3, 4, 5).reshape(B, S, H, dh)

    gts = jax.nn.sigmoid(gate.reshape(B, S, H, 3))
    o = (gts[..., 0:1] * o_c.reshape(B, S, H, dh) + gts[..., 1:2] * o_s + gts[..., 2:3] * o_w)
    return o.reshape(B, S, H * dh) @ w_out


def s5_mixer(xn, w_in, a_re, a_im, b_re, b_im, c_re, c_im, d_skip, log_dt, w_glu):
    B, S, D = xn.shape
    G, P, Cg = S5_GROUPS, S5_STATE, S5_GROUP
    u = (xn @ w_in).astype(jnp.float32)
    ug = u.reshape(B, S, G, Cg)
    dt = jnp.exp(log_dt.astype(jnp.float32))[:, None]
    ar, ai = a_re.astype(jnp.float32), a_im.astype(jnp.float32)
    mag = jnp.exp(dt * ar)
    abar_re, abar_im = mag * jnp.cos(dt * ai), mag * jnp.sin(dt * ai)
    den = ar * ar + ai * ai
    nr, ni = abar_re - 1.0, abar_im
    f_re = (nr * ar + ni * ai) / den
    f_im = (ni * ar - nr * ai) / den
    br, bi = b_re.astype(jnp.float32), b_im.astype(jnp.float32)
    bbar_re = f_re[..., None] * br - f_im[..., None] * bi
    bbar_im = f_re[..., None] * bi + f_im[..., None] * br
    bu_re = jnp.einsum('bsgc,gpc->sbgp', ug, bbar_re)
    bu_im = jnp.einsum('bsgc,gpc->sbgp', ug, bbar_im)
    la_re = jnp.broadcast_to(abar_re[None, None], (S, 1, G, P))
    la_im = jnp.broadcast_to(abar_im[None, None], (S, 1, G, P))

    def combine(e1, e2):
        a1r, a1i, b1r, b1i = e1
        a2r, a2i, b2r, b2i = e2
        return (a2r * a1r - a2i * a1i, a2r * a1i + a2i * a1r,
                a2r * b1r - a2i * b1i + b2r, a2r * b1i + a2i * b1r + b2i)

    _, _, xr, xi = lax.associative_scan(combine, (la_re, la_im, bu_re, bu_im), axis=0)
    y = (jnp.einsum('sbgp,gcp->bsgc', xr, c_re.astype(jnp.float32))
         - jnp.einsum('sbgp,gcp->bsgc', xi, c_im.astype(jnp.float32)))
    y = y.reshape(B, S, D) + d_skip.astype(jnp.float32) * u
    y = jax.nn.gelu(y).astype(xn.dtype)
    a, b = jnp.split(y @ w_glu, 2, axis=-1)
    return a * jax.nn.sigmoid(b)


def moe_ffn(hn, router, w_gate_up, w_down):
    logits = (hn @ router).astype(jnp.float32)
    top_v, top_i = lax.top_k(logits, TOP_K)
    w = jax.nn.softmax(top_v, axis=-1)
    gates = jnp.sum(jax.nn.one_hot(top_i, N_EXPERTS, dtype=jnp.float32) * w[..., None], axis=-2)
    gates = gates.astype(hn.dtype)
    out = jnp.zeros_like(hn)
    for e in range(N_EXPERTS):
        out = out + gates[..., e:e + 1] * swiglu(hn, w_gate_up[e], w_down[e])
    return out


def setup_inputs(seed: int = 0) -> dict:
    key = jax.random.key(seed)
    keys = iter(jax.random.split(key, 64))
    f32 = jnp.float32
    D = D_MODEL

    def nrm(shape, scale):
        return jax.random.normal(next(keys), shape, f32) * scale

    def gain(shape):
        return 1.0 + 0.05 * jax.random.normal(next(keys), shape, f32)

    n_ret = len(range(0, DEPTH, N_MIXERS))
    n_nsa = len(range(1, DEPTH, N_MIXERS))
    n_s5 = len(range(2, DEPTH, N_MIXERS))
    n_dense = len(range(0, DEPTH, 2))
    n_moe = len(range(1, DEPTH, 2))
    ret_in = 2 * RET_HEADS * RET_DK + 2 * RET_HEADS * RET_DV
    nsa_in = NSA_HEADS * NSA_DH + 6 * NSA_GROUPS * NSA_DH + 3 * NSA_HEADS

    x = nrm((BATCH, SEQ, D), 1.0)
    p = nrm((DEPTH, BATCH, SEQ, PLE_DIM), 1.0)
    positions = (jax.random.randint(next(keys), (BATCH, 1), 0, POS_OFFSET_MAX, jnp.int32)
                 + jnp.arange(SEQ, dtype=jnp.int32)[None, :])
    a_re = -0.5 + 0.01 * jax.random.normal(next(keys), (n_s5, S5_GROUPS, S5_STATE), f32)
    a_im = (jnp.pi * jnp.arange(S5_STATE, dtype=f32)[None, None, :]
            + 0.01 * jax.random.normal(next(keys), (n_s5, S5_GROUPS, S5_STATE), f32))
    return {
        "x": x, "p": p, "positions": positions,
        "norm_mix": gain((DEPTH, D)), "norm_ffn": gain((DEPTH, D)), "norm_ple": gain((DEPTH, D)),
        "ret_w_in": nrm((n_ret, D, ret_in), D ** -0.5),
        "ret_gn": gain((n_ret, RET_HEADS * RET_DV)),
        "ret_w_out": nrm((n_ret, RET_HEADS * RET_DV, D), (RET_HEADS * RET_DV) ** -0.5),
        "nsa_w_in": nrm((n_nsa, D, nsa_in), D ** -0.5),
        "nsa_q_norm": gain((n_nsa, NSA_DH)),
        "nsa_k_norm": gain((n_nsa, 3, NSA_DH)),
        "nsa_cmp_pos": nrm((n_nsa, 2, NSA_CMP_LEN, NSA_DH), 0.1),
        "nsa_cmp_w1": nrm((n_nsa, 2, NSA_CMP_LEN * NSA_DH, NSA_CMP_HID), (NSA_CMP_LEN * NSA_DH) ** -0.5),
        "nsa_cmp_w2": nrm((n_nsa, 2, NSA_CMP_HID, NSA_DH), NSA_CMP_HID ** -0.5),
        "nsa_w_out": nrm((n_nsa, NSA_HEADS * NSA_DH, D), (NSA_HEADS * NSA_DH) ** -0.5),
        "s5_w_in": nrm((n_s5, D, D), D ** -0.5),
        "s5_a_re": a_re, "s5_a_im": a_im,
        "s5_b_re": nrm((n_s5, S5_GROUPS, S5_STATE, S5_GROUP), (2 * S5_GROUP) ** -0.5),
        "s5_b_im": nrm((n_s5, S5_GROUPS, S5_STATE, S5_GROUP), (2 * S5_GROUP) ** -0.5),
        "s5_c_re": nrm((n_s5, S5_GROUPS, S5_GROUP, S5_STATE), (2 * S5_STATE) ** -0.5),
        "s5_c_im": nrm((n_s5, S5_GROUPS, S5_GROUP, S5_STATE), (2 * S5_STATE) ** -0.5),
        "s5_d": nrm((n_s5, D), 0.5),
        "s5_log_dt": jax.random.uniform(next(keys), (n_s5, S5_GROUPS), f32, math.log(1e-3), math.log(1e-1)),
        "s5_w_glu": nrm((n_s5, D, 2 * D), D ** -0.5),
        "ffn_w_gate_up": nrm((n_dense, D, 2 * D_FF), D ** -0.5),
        "ffn_w_down": nrm((n_dense, D_FF, D), D_FF ** -0.5),
        "moe_router": nrm((n_moe, D, N_EXPERTS), D ** -0.5),
        "moe_w_gate_up": nrm((n_moe, N_EXPERTS, D, 2 * D_FF_EXPERT), D ** -0.5),
        "moe_w_down": nrm((n_moe, N_EXPERTS, D_FF_EXPERT, D), D_FF_EXPERT ** -0.5),
        "ple_w_proj": nrm((DEPTH, PLE_DIM, D), PLE_DIM ** -0.5),
        "ple_w_gate": nrm((DEPTH, D, D), D ** -0.5),
    }


def reference(x, p, positions, norm_mix, norm_ffn, norm_ple,
              ret_w_in, ret_gn, ret_w_out,
              nsa_w_in, nsa_q_norm, nsa_k_norm, nsa_cmp_pos, nsa_cmp_w1, nsa_cmp_w2, nsa_w_out,
              s5_w_in, s5_a_re, s5_a_im, s5_b_re, s5_b_im, s5_c_re, s5_c_im, s5_d, s5_log_dt, s5_w_glu,
              ffn_w_gate_up, ffn_w_down,
              moe_router, moe_w_gate_up, moe_w_down,
              ple_w_proj, ple_w_gate):
    h = x
    for i in range(DEPTH):
        m, j = i % N_MIXERS, i // N_MIXERS
        hn = rms_norm(h, norm_mix[i])
        if m == 0:
            y = retention_mixer(hn, positions, ret_w_in[j], ret_gn[j], ret_w_out[j])
        elif m == 1:
            y = nsa_mixer(hn, positions, nsa_w_in[j], nsa_q_norm[j], nsa_k_norm[j], nsa_cmp_pos[j],
                          nsa_cmp_w1[j], nsa_cmp_w2[j], nsa_w_out[j])
        else:
            y = s5_mixer(hn, s5_w_in[j], s5_a_re[j], s5_a_im[j], s5_b_re[j], s5_b_im[j],
                         s5_c_re[j], s5_c_im[j], s5_d[j], s5_log_dt[j], s5_w_glu[j])
        h = h + y.astype(h.dtype)
        hn = rms_norm(h, norm_ffn[i])
        if i % 2 == 0:
            f = swiglu(hn, ffn_w_gate_up[i // 2], ffn_w_down[i // 2])
        else:
            f = moe_ffn(hn, moe_router[i // 2], moe_w_gate_up[i // 2], moe_w_down[i // 2])
        h = h + f.astype(h.dtype)
        gate = jax.nn.sigmoid(rms_norm(h, norm_ple[i]) @ ple_w_gate[i])
        h = h + (gate * (p[i] @ ple_w_proj[i])).astype(h.dtype)
    return h
```

```python
import functools
import math

import numpy as np
import jax
import jax.numpy as jnp
from jax import lax
from jax.experimental import pallas as pl
from jax.experimental.pallas import tpu as pltpu

F32 = jnp.float32
BF16 = jnp.bfloat16

NORM_EPS = 1e-6
ROPE_THETA = 10000.0

RET_HEADS = 4
RET_CHUNK = 128

NSA_HEADS = 8
NSA_GROUPS = 2
NSA_DH = 128
NSA_CMP_LEN = 32
NSA_CMP_STRIDE = 16
NSA_SEL_LEN = 64
NSA_TOP_N = 16
NSA_WINDOW = 512
NSA_Q_TILE = 128
NSA_KV_TILE = 512

S5_GROUP = 16
S5_STATE = 64
S5_SET = 8
S5_TIME_CHUNK = 64

N_EXPERTS = 8
LANES = 128
MASK_NEG = -1e30

VMEM_LIMIT = 56 * 1024 * 1024


def _params(*sem):
    return pltpu.CompilerParams(dimension_semantics=sem, vmem_limit_bytes=VMEM_LIMIT)


def _dot(a, b):
    return jnp.dot(a, b, preferred_element_type=F32)


def _dot_nt(a, b):
    return lax.dot_general(a, b, (((1,), (1,)), ((), ())), preferred_element_type=F32)


def _rms(x, g):
    return x * lax.rsqrt(jnp.mean(x * x, axis=-1, keepdims=True) + NORM_EPS) * g


def _gelu_tanh(x):
    return 0.5 * x * (1.0 + jnp.tanh(math.sqrt(2.0 / math.pi) * (x + 0.044715 * (x * x * x))))


def _rope_table_kernel(pos_ref, inv_ref, sign_ref, cos_ref, sin_ref):
    ang = pos_ref[...].astype(F32) * inv_ref[...]
    cos_ref[...] = jnp.cos(ang)
    sin_ref[...] = jnp.sin(ang) * sign_ref[...]


def rope_tables(pos_col, inv, sign, tm):
    T = pos_col.shape[0]
    return pl.pallas_call(
        _rope_table_kernel,
        grid=(T // tm,),
        in_specs=[pl.BlockSpec((tm, 1), lambda i: (i, 0)),
                  pl.BlockSpec((1, LANES), lambda i: (0, 0)),
                  pl.BlockSpec((1, LANES), lambda i: (0, 0))],
        out_specs=[pl.BlockSpec((tm, LANES), lambda i: (i, 0))] * 2,
        out_shape=[jax.ShapeDtypeStruct((T, LANES), F32)] * 2,
        compiler_params=_params("parallel"),
        name="rope_tables",
    )(pos_col, inv, sign)


def _norm_matmul_kernel(x_ref, g_ref, w_ref, o_ref, xn_ref):
    @pl.when(pl.program_id(1) == 0)
    def _():
        xn_ref[...] = _rms(x_ref[...], g_ref[...]).astype(BF16)

    o_ref[...] = _dot(xn_ref[...], w_ref[...]).astype(o_ref.dtype)


def norm_matmul(x, g, w, *, tm, tn, out_dtype, time_major_batch=None):
    T, D = x.shape
    N = w.shape[1]
    nj = N // tn
    if time_major_batch is None:
        out_shape = jax.ShapeDtypeStruct((T, N), out_dtype)
        out_spec = pl.BlockSpec((tm, tn), lambda i, j: (i, j))
    else:
        B, S = time_major_batch
        ns = S // tm
        out_shape = jax.ShapeDtypeStruct((S, B * N), out_dtype)
        out_spec = pl.BlockSpec((tm, tn), lambda i, j: (i % ns, (i // ns) * nj + j))
    return pl.pallas_call(
        _norm_matmul_kernel,
        grid=(T // tm, nj),
        in_specs=[pl.BlockSpec((tm, D), lambda i, j: (i, 0)),
                  pl.BlockSpec((1, D), lambda i, j: (0, 0)),
                  pl.BlockSpec((D, tn), lambda i, j: (0, j))],
        out_specs=out_spec,
        out_shape=out_shape,
        scratch_shapes=[pltpu.VMEM((tm, D), BF16)],
        compiler_params=_params("parallel", "arbitrary"),
        name="norm_matmul",
    )(x, g.reshape(1, D), w)


def _matmul_res_kernel(x_ref, w_ref, r_ref, o_ref):
    o_ref[...] = r_ref[...] + _dot(x_ref[...], w_ref[...])


def matmul_res(x, w, res, *, tm):
    T, K = x.shape
    N = w.shape[1]
    return pl.pallas_call(
        _matmul_res_kernel,
        grid=(T // tm,),
        in_specs=[pl.BlockSpec((tm, K), lambda i: (i, 0)),
                  pl.BlockSpec((K, N), lambda i: (0, 0)),
                  pl.BlockSpec((tm, N), lambda i: (i, 0))],
        out_specs=pl.BlockSpec((tm, N), lambda i: (i, 0)),
        out_shape=jax.ShapeDtypeStruct((T, N), F32),
        compiler_params=_params("parallel"),
        name="matmul_res",
    )(x, w, res)


def _retention_kernel(q_ref, k_ref, v_ref, g_ref, cos_ref, sin_ref, dmask_ref, qdec_ref, kdec_ref,
                      cdec_ref, gn_ref, o_ref, r_scr):
    @pl.when(pl.program_id(2) == 0)
    def _():
        r_scr[...] = jnp.zeros_like(r_scr)

    cos = cos_ref[...]
    sin = sin_ref[...]
    half = cos.shape[1]

    def rope(x):
        x1, x2 = x[:, :half], x[:, half:]
        return jnp.concatenate([x1 * cos - x2 * sin, x2 * cos + x1 * sin], axis=-1)

    dk = q_ref.shape[1]
    q = rope(q_ref[...].astype(F32))
    k = rope(k_ref[...].astype(F32)) * (dk ** -0.5)
    v = v_ref[...]
    qb = q.astype(BF16)
    scores = _dot_nt(qb, k.astype(BF16)) * dmask_ref[0]
    intra = _dot(scores.astype(BF16), v)
    r_old = r_scr[...]
    cross = _dot(qb, r_old.astype(BF16)) * qdec_ref[0]
    kd_t = (k * kdec_ref[0]).T.astype(BF16)
    r_scr[...] = r_old * cdec_ref[0] + _dot(kd_t, v)

    o = intra + cross
    mu = jnp.mean(o, axis=-1, keepdims=True)
    oc = o - mu
    var = jnp.mean(oc * oc, axis=-1, keepdims=True)
    on = oc * lax.rsqrt(var + NORM_EPS) * gn_ref[...]
    g = g_ref[...].astype(F32)
    o_ref[...] = (on * (g * jax.nn.sigmoid(g))).astype(o_ref.dtype)


def retention_core(proj, cos, sin, gn, B, S):
    H, C = RET_HEADS, RET_CHUNK
    T = proj.shape[0]
    dk = proj.shape[1] // (6 * H)
    dv = 2 * dk
    N = S // C
    log_gamma = jnp.log1p(-jnp.exp2(-5.0 - jnp.arange(H, dtype=F32)))
    idx = jnp.arange(C, dtype=F32)
    diff = idx[:, None] - idx[None, :]
    dmask = jnp.where(diff >= 0, jnp.exp(jnp.maximum(diff, 0.0) * log_gamma[:, None, None]), 0.0)
    qdec = jnp.exp((idx + 1.0) * log_gamma[:, None])[..., None]
    kdec = jnp.exp((C - 1.0 - idx) * log_gamma[:, None])[..., None]
    cdec = jnp.exp(C * log_gamma).reshape(H, 1, 1)
    row = lambda b, h, n: b * N + n
    return pl.pallas_call(
        _retention_kernel,
        grid=(B, H, N),
        in_specs=[pl.BlockSpec((C, dk), lambda b, h, n: (row(b, h, n), h)),
                  pl.BlockSpec((C, dk), lambda b, h, n: (row(b, h, n), H + h)),
                  pl.BlockSpec((C, dv), lambda b, h, n: (row(b, h, n), H + h)),
                  pl.BlockSpec((C, dv), lambda b, h, n: (row(b, h, n), 2 * H + h)),
                  pl.BlockSpec((C, dk // 2), lambda b, h, n: (row(b, h, n), 0)),
                  pl.BlockSpec((C, dk // 2), lambda b, h, n: (row(b, h, n), 0)),
                  pl.BlockSpec((1, C, C), lambda b, h, n: (h, 0, 0)),
                  pl.BlockSpec((1, C, 1), lambda b, h, n: (h, 0, 0)),
                  pl.BlockSpec((1, C, 1), lambda b, h, n: (h, 0, 0)),
                  pl.BlockSpec((1, 1, 1), lambda b, h, n: (h, 0, 0)),
                  pl.BlockSpec((1, dv), lambda b, h, n: (0, h))],
        out_specs=pl.BlockSpec((C, dv), lambda b, h, n: (row(b, h, n), h)),
        out_shape=jax.ShapeDtypeStruct((T, H * dv), BF16),
        scratch_shapes=[pltpu.VMEM((dk, dv), F32)],
        compiler_params=_params("parallel", "parallel", "arbitrary"),
        name="retention",
    )(proj, proj, proj, proj, cos, sin, dmask, qdec, kdec, cdec, gn.reshape(1, H * dv))


def _nsa_compress_kernel(kc_ref, vc_ref, pe_ref, w1_ref, w2_ref, kn_ref, cos_ref, sin_ref,
                         kco_ref, vco_ref, xs_scr):
    nh = NSA_CMP_LEN // NSA_CMP_STRIDE
    n_rows = xs_scr.shape[0] // NSA_CMP_STRIDE
    for br, (src, dst) in enumerate(((kc_ref, kco_ref), (vc_ref, vco_ref))):
        xs_scr[...] = src[...].astype(F32)
        acc = [jnp.zeros((n_rows, w1_ref.shape[-1]), F32) for _ in range(nh)]
        for l in range(NSA_CMP_STRIDE):
            piece = xs_scr[pl.ds(l, n_rows, stride=NSA_CMP_STRIDE), :]
            for a in range(nh):
                ll = a * NSA_CMP_STRIDE + l
                acc[a] = acc[a] + _dot((piece + pe_ref[br, ll:ll + 1, :]).astype(BF16), w1_ref[br, ll])
        hid = acc[0] + pltpu.roll(acc[1], n_rows - 1, axis=0)
        z = _dot(_gelu_tanh(hid).astype(BF16), w2_ref[br])
        if br == 0:
            z = _rms(z, kn_ref[0:1, :])
            cos = cos_ref[pl.ds(0, n_rows, stride=NSA_CMP_STRIDE), :]
            sin = sin_ref[pl.ds(0, n_rows, stride=NSA_CMP_STRIDE), :]
            z = z * cos + pltpu.roll(z, NSA_DH // 2, axis=1) * sin
        dst[0, 0] = z.astype(dst.dtype)


def nsa_compress(proj, pe, w1, w2, kn, cosn, sinn, B, S):
    G, dh = NSA_GROUPS, NSA_DH
    nr = S // NSA_CMP_STRIDE
    kc0 = NSA_HEADS
    vc0 = NSA_HEADS + G
    return pl.pallas_call(
        _nsa_compress_kernel,
        grid=(B, G),
        in_specs=[pl.BlockSpec((S, dh), lambda b, g: (b, kc0 + g)),
                  pl.BlockSpec((S, dh), lambda b, g: (b, vc0 + g)),
                  pl.BlockSpec(pe.shape, lambda b, g: (0, 0, 0)),
                  pl.BlockSpec(w1.shape, lambda b, g: (0, 0, 0, 0)),
                  pl.BlockSpec(w2.shape, lambda b, g: (0, 0, 0)),
                  pl.BlockSpec(kn.shape, lambda b, g: (0, 0)),
                  pl.BlockSpec((S, dh), lambda b, g: (b, 0)),
                  pl.BlockSpec((S, dh), lambda b, g: (b, 0))],
        out_specs=[pl.BlockSpec((1, 1, nr, dh), lambda b, g: (b, g, 0, 0))] * 2,
        out_shape=[jax.ShapeDtypeStruct((B, G, nr, dh), BF16)] * 2,
        scratch_shapes=[pltpu.VMEM((S, dh), F32)],
        compiler_params=_params("parallel", "parallel"),
        name="nsa_compress",
    )(proj, proj, pe, w1, w2, kn, cosn, sinn)


def _softmax_parts(s, ok):
    s = jnp.where(ok, s, MASK_NEG)
    m = jnp.max(s, axis=-1, keepdims=True)
    p = jnp.where(ok, jnp.exp(s - m), 0.0)
    return p, jnp.sum(p, axis=-1, keepdims=True)


def _nsa_attn_kernel(q_ref, gate_ref, cosq_ref, sinq_ref, qn_ref, kc_ref, vc_ref,
                     ks_ref, vs_ref, kw_ref, vw_ref, kn_ref, cosk_ref, sink_ref, mmat_ref,
                     o_ref, ks_scr, kw_scr):
    grp = pl.program_id(1)
    qi = pl.program_id(2)
    tq = q_ref.shape[0]
    dh = NSA_DH
    hg = q_ref.shape[1] // dh
    S = ks_ref.shape[0]
    tk = min(NSA_KV_TILE, S)
    q0 = qi * tq

    @pl.when(qi == 0)
    def _():
        cosk = cosk_ref[...]
        sink = sink_ref[...]
        for src, dst, r in ((ks_ref, ks_scr, 1), (kw_ref, kw_scr, 2)):
            x = _rms(src[...].astype(F32), kn_ref[r:r + 1, :])
            dst[...] = (x * cosk + pltpu.roll(x, dh // 2, axis=1) * sink).astype(BF16)

    cosq = cosq_ref[...]
    sinq = sinq_ref[...]
    qs = []
    for h in range(hg):
        x = _rms(q_ref[:, h * dh:(h + 1) * dh].astype(F32), qn_ref[...])
        x = (x * cosq + pltpu.roll(x, dh // 2, axis=1) * sinq) * (dh ** -0.5)
        qs.append(x.astype(BF16))
    qall = jnp.concatenate(qs, axis=0)

    def token_of_row(n):
        return q0 + jnp.bitwise_and(lax.broadcasted_iota(jnp.int32, (hg * tq, n), 0), tq - 1)

    ncp = kc_ref.shape[2]
    c_i = lax.broadcasted_iota(jnp.int32, (hg * tq, ncp), 1)
    valid_c = c_i * NSA_CMP_STRIDE + (NSA_CMP_LEN - 1) <= token_of_row(ncp)
    p_c, l_c = _softmax_parts(_dot_nt(qall, kc_ref[0, 0]), valid_c)
    p_c = p_c / jnp.where(l_c > 0.0, l_c, 1.0)
    o_c = _dot(p_c.astype(BF16), vc_ref[0, 0])
    p_sum = p_c[0:tq]
    for h in range(1, hg):
        p_sum = p_sum + p_c[h * tq:(h + 1) * tq]
    imp = jnp.dot(p_sum, mmat_ref[...], preferred_element_type=F32, precision=lax.Precision.HIGHEST)

    n_sel = S // NSA_SEL_LEN
    sel_rows = ((n_sel + 7) // 8) * 8
    v_imp = imp.T[0:sel_rows, :]
    jb = lax.broadcasted_iota(jnp.int32, (sel_rows, tq), 0)
    cur = (q0 + lax.broadcasted_iota(jnp.int32, (sel_rows, tq), 1)) // NSA_SEL_LEN
    forced = (jb == 0) | (jb == cur) | (jb == cur - 1)
    v_imp = jnp.where(forced, jnp.inf, jnp.where(jb > cur, -jnp.inf, v_imp))
    if sel_rows > n_sel:
        v_imp = jnp.where(jb >= n_sel, -jnp.inf, v_imp)
    rank = jnp.zeros((sel_rows, tq), jnp.int32)
    for i in range(n_sel):
        r = v_imp[i:i + 1, :]
        beats = (r > v_imp) | ((r == v_imp) & (jb > i))
        rank = rank + beats.astype(jnp.int32)
    sel_t = ((rank < min(NSA_TOP_N, n_sel)) & (jb < n_sel)).astype(F32)
    if sel_rows < LANES:
        sel_t = jnp.concatenate([sel_t, jnp.zeros((LANES - sel_rows, tq), F32)], axis=0)
    sel = sel_t.T.astype(BF16)
    sel = jnp.concatenate([sel] * hg, axis=0)

    row_t = token_of_row(tk)
    lane_k = lax.broadcasted_iota(jnp.int32, (hg * tq, tk), 1)
    e_row = lax.broadcasted_iota(jnp.int32, (LANES, tk), 0)
    e_blk = lax.broadcasted_iota(jnp.int32, (LANES, tk), 1) // NSA_SEL_LEN

    def sel_step(j, carry):
        m, l, acc = carry
        base = pl.multiple_of(j * tk, tk)
        s = _dot_nt(qall, ks_scr[pl.ds(base, tk), :])
        expand = (e_blk + j * (tk // NSA_SEL_LEN) == e_row).astype(BF16)
        chosen = _dot(sel, expand)
        ok = (chosen > 0.5) & (lane_k + base <= row_t)
        s = jnp.where(ok, s, MASK_NEG)
        m_new = jnp.maximum(m, jnp.max(s, axis=-1, keepdims=True))
        alpha = jnp.exp(m - m_new)
        p = jnp.where(ok, jnp.exp(s - m_new), 0.0)
        l = alpha * l + jnp.sum(p, axis=-1, keepdims=True)
        acc = alpha * acc + _dot(p.astype(BF16), vs_ref[pl.ds(base, tk), :])
        return m_new, l, acc

    n_kv = (q0 + tq + tk - 1) // tk
    init = (jnp.full((hg * tq, 1), MASK_NEG, F32), jnp.zeros((hg * tq, 1), F32), jnp.zeros((hg * tq, dh), F32))
    _, l_s, acc_s = lax.fori_loop(0, n_kv, sel_step, init)
    o_s = acc_s / l_s

    span = min(tq + NSA_WINDOW, S)
    start = pl.multiple_of(jnp.maximum(q0 + tq - span, 0), tq)
    dist = token_of_row(span) - (start + lax.broadcasted_iota(jnp.int32, (hg * tq, span), 1))
    ok_w = (dist >= 0) & (dist < NSA_WINDOW)
    p_w, l_w = _softmax_parts(_dot_nt(qall, kw_scr[pl.ds(start, span), :]), ok_w)
    o_w = _dot(p_w.astype(BF16), vw_ref[pl.ds(start, span), :]) / l_w

    gsig = jax.nn.sigmoid(gate_ref[...].astype(F32))
    n_br = 3
    for h in range(hg):
        def gate_col(br):
            lo = h * n_br + br
            hi = lo + hg * n_br
            return jnp.where(grp == 0, gsig[:, lo:lo + 1], gsig[:, hi:hi + 1])
        rows = slice(h * tq, (h + 1) * tq)
        o = gate_col(0) * o_c[rows] + gate_col(1) * o_s[rows] + gate_col(2) * o_w[rows]
        o_ref[:, h * dh:(h + 1) * dh] = o.astype(o_ref.dtype)


def nsa_attention(proj, kcc, vcc, qn, kn, cosn, sinn, B, S):
    H, G, dh = NSA_HEADS, NSA_GROUPS, NSA_DH
    hg = H // G
    T = proj.shape[0]
    tq = min(NSA_Q_TILE, S)
    nq = S // tq
    nr = kcc.shape[2]
    n_c = (S - NSA_CMP_LEN) // NSA_CMP_STRIDE + 1
    n_sel = S // NSA_SEL_LEN
    cs = (np.arange(n_c) * NSA_CMP_STRIDE)[:, None]
    js = (np.arange(n_sel) * NSA_SEL_LEN)[None, :]
    overlap = np.clip(np.minimum(cs + NSA_CMP_LEN, js + NSA_SEL_LEN) - np.maximum(cs, js), 0, None) / NSA_CMP_LEN
    mmat = np.zeros((nr, LANES), np.float32)
    mmat[:n_c, :n_sel] = overlap
    col = lambda base: (lambda b, g, i: (b, base + g))
    return pl.pallas_call(
        _nsa_attn_kernel,
        grid=(B, G, nq),
        in_specs=[pl.BlockSpec((tq, hg * dh), lambda b, g, i: (b * nq + i, g)),
                  pl.BlockSpec((tq, LANES), lambda b, g, i: (b * nq + i, H + 6 * G)),
                  pl.BlockSpec((tq, dh), lambda b, g, i: (b * nq + i, 0)),
                  pl.BlockSpec((tq, dh), lambda b, g, i: (b * nq + i, 0)),
                  pl.BlockSpec((1, dh), lambda b, g, i: (0, 0)),
                  pl.BlockSpec((1, 1, nr, dh), lambda b, g, i: (b, g, 0, 0)),
                  pl.BlockSpec((1, 1, nr, dh), lambda b, g, i: (b, g, 0, 0)),
                  pl.BlockSpec((S, dh), col(H + 2 * G)),
                  pl.BlockSpec((S, dh), col(H + 3 * G)),
                  pl.BlockSpec((S, dh), col(H + 4 * G)),
                  pl.BlockSpec((S, dh), col(H + 5 * G)),
                  pl.BlockSpec(kn.shape, lambda b, g, i: (0, 0)),
                  pl.BlockSpec((S, dh), lambda b, g, i: (b, 0)),
                  pl.BlockSpec((S, dh), lambda b, g, i: (b, 0)),
                  pl.BlockSpec(mmat.shape, lambda b, g, i: (0, 0))],
        out_specs=pl.BlockSpec((tq, hg * dh), lambda b, g, i: (b * nq + i, g)),
        out_shape=jax.ShapeDtypeStruct((T, H * dh), BF16),
        scratch_shapes=[pltpu.VMEM((S, dh), BF16), pltpu.VMEM((S, dh), BF16)],
        compiler_params=_params("parallel", "parallel", "arbitrary"),
        name="nsa_attention",
    )(proj, proj, cosn, sinn, qn.reshape(1, dh), kcc, vcc, proj, proj, proj, proj, kn, cosn, sinn,
      jnp.asarray(mmat))


def _s5_scan_kernel(u_ref, bbd_ref, are_ref, aim_ref, cbd_ref, d_ref, y_ref, bu_scr, xs_scr, st_scr):
    nb = st_scr.shape[0] // 2
    w = are_ref.shape[-1]
    tc = u_ref.shape[0] // nb

    @pl.when(pl.program_id(1) == 0)
    def _():
        st_scr[...] = jnp.zeros_like(st_scr)

    u = u_ref[...]
    bu_scr[...] = _dot(u.astype(BF16), bbd_ref[0])
    a_re = jnp.broadcast_to(are_ref[0], (nb, w))
    a_im = jnp.broadcast_to(aim_ref[0], (nb, w))

    def step(t, carry):
        xr, xi = carry
        r0 = pl.multiple_of(t * nb, nb)
        nxr = a_re * xr - a_im * xi + bu_scr[pl.ds(r0, nb), 0:w]
        nxi = a_re * xi + a_im * xr + bu_scr[pl.ds(r0, nb), w:2 * w]
        xs_scr[pl.ds(r0, nb), 0:w] = nxr
        xs_scr[pl.ds(r0, nb), w:2 * w] = nxi
        return nxr, nxi

    xr, xi = lax.fori_loop(0, tc, step, (st_scr[0:nb, :], st_scr[nb:2 * nb, :]), unroll=4)
    st_scr[0:nb, :] = xr
    st_scr[nb:2 * nb, :] = xi
    y = _dot(xs_scr[...].astype(BF16), cbd_ref[0]) + d_ref[...] * u
    y_ref[...] = _gelu_tanh(y).astype(y_ref.dtype)


def s5_scan(u_tm, a_re, a_im, b_re, b_im, c_re, c_im, d_skip, log_dt, B, S):
    D = u_tm.shape[1]
    Cg, P, NS = S5_GROUP, S5_STATE, S5_SET
    G = D // Cg
    K = G // NS
    w = NS * P
    tc = min(S5_TIME_CHUNK, S)
    dt = jnp.exp(log_dt.astype(F32))[:, None]
    mag = jnp.exp(dt * a_re)
    abar_re, abar_im = mag * jnp.cos(dt * a_im), mag * jnp.sin(dt * a_im)
    den = a_re * a_re + a_im * a_im
    nr_, ni_ = abar_re - 1.0, abar_im
    f_re = (nr_ * a_re + ni_ * a_im) / den
    f_im = (ni_ * a_re - nr_ * a_im) / den
    bbar_re = f_re[..., None] * b_re - f_im[..., None] * b_im
    bbar_im = f_re[..., None] * b_im + f_im[..., None] * b_re
    eye = jnp.eye(NS, dtype=F32)

    def in_blockdiag(bb):
        t = bb.reshape(K, NS, P, Cg).transpose(0, 1, 3, 2)
        return jnp.einsum('kgcp,gh->kgchp', t, eye).reshape(K, NS * Cg, NS * P)

    def out_blockdiag(cc):
        t = cc.reshape(K, NS, Cg, P).transpose(0, 1, 3, 2)
        return jnp.einsum('kgpc,gh->kgphc', t, eye).reshape(K, NS * P, NS * Cg)

    bbd = jnp.concatenate([in_blockdiag(bbar_re), in_blockdiag(bbar_im)], axis=-1).astype(BF16)
    cbd = jnp.concatenate([out_blockdiag(c_re), -out_blockdiag(c_im)], axis=1).astype(BF16)
    are = abar_re.reshape(K, 1, w)
    aim = abar_im.reshape(K, 1, w)
    rows = tc * B
    return pl.pallas_call(
        _s5_scan_kernel,
        grid=(K, S // tc),
        in_specs=[pl.BlockSpec((rows, LANES), lambda k, c: (c, k)),
                  pl.BlockSpec((1, LANES, 2 * w), lambda k, c: (k, 0, 0)),
                  pl.BlockSpec((1, 1, w), lambda k, c: (k, 0, 0)),
                  pl.BlockSpec((1, 1, w), lambda k, c: (k, 0, 0)),
                  pl.BlockSpec((1, 2 * w, LANES), lambda k, c: (k, 0, 0)),
                  pl.BlockSpec((1, LANES), lambda k, c: (0, k))],
        out_specs=pl.BlockSpec((rows, LANES), lambda k, c: (c, k)),
        out_shape=jax.ShapeDtypeStruct((S * B, D), BF16),
        scratch_shapes=[pltpu.VMEM((rows, 2 * w), F32), pltpu.VMEM((rows, 2 * w), F32),
                        pltpu.VMEM((2 * B, w), F32)],
        compiler_params=_params("parallel", "arbitrary"),
        name="s5_scan",
    )(u_tm, bbd, are, aim, cbd, d_skip.reshape(1, D))


def _glu_res_kernel(y_ref, wa_ref, wb_ref, r_ref, o_ref):
    y = y_ref[...]
    a = _dot(y, wa_ref[...])
    b = _dot(y, wb_ref[...])
    o_ref[...] = r_ref[...] + a * jax.nn.sigmoid(b)


def glu_res(y_tm, w, res, B, S, *, tm):
    D = w.shape[0]
    N = w.shape[1] // 2
    ns = S // tm
    return pl.pallas_call(
        _glu_res_kernel,
        grid=(B * ns,),
        in_specs=[pl.BlockSpec((tm, D), lambda i: (i % ns, i // ns)),
                  pl.BlockSpec((D, N), lambda i: (0, 0)),
                  pl.BlockSpec((D, N), lambda i: (0, 1)),
                  pl.BlockSpec((tm, N), lambda i: (i, 0))],
        out_specs=pl.BlockSpec((tm, N), lambda i: (i, 0)),
        out_shape=jax.ShapeDtypeStruct((B * S, N), F32),
        compiler_params=_params("parallel"),
        name="glu_res",
    )(y_tm, w, w, res)


def _router_kernel(x_ref, g_ref, w_ref, o_ref):
    xn = _rms(x_ref[...], g_ref[...])
    logits = jnp.dot(xn, w_ref[...], preferred_element_type=F32, precision=lax.Precision.HIGHEST)
    lane = lax.broadcasted_iota(jnp.int32, logits.shape, 1).astype(F32)
    logits = jnp.where(lane < N_EXPERTS, logits, -jnp.inf)

    def top1(v):
        m = jnp.max(v, axis=-1, keepdims=True)
        idx = jnp.min(jnp.where(v == m, lane, float(LANES)), axis=-1, keepdims=True)
        return m, lane == idx

    m1, hot1 = top1(logits)
    m2, hot2 = top1(jnp.where(hot1, -jnp.inf, logits))
    e2 = jnp.exp(m2 - m1)
    w1 = 1.0 / (1.0 + e2)
    o_ref[...] = jnp.where(hot1, w1, 0.0) + jnp.where(hot2, e2 * w1, 0.0)


def router_gates(x, g, w_router, *, tm):
    T, D = x.shape
    w = jnp.zeros((D, LANES), F32).at[:, :N_EXPERTS].set(w_router)
    return pl.pallas_call(
        _router_kernel,
        grid=(T // tm,),
        in_specs=[pl.BlockSpec((tm, D), lambda i: (i, 0)),
                  pl.BlockSpec((1, D), lambda i: (0, 0)),
                  pl.BlockSpec((D, LANES), lambda i: (0, 0))],
        out_specs=pl.BlockSpec((tm, LANES), lambda i: (i, 0)),
        out_shape=jax.ShapeDtypeStruct((T, LANES), F32),
        compiler_params=_params("parallel"),
        name="router",
    )(x, g.reshape(1, D), w)


def _ffn_kernel(x_ref, g_ref, gate_ref, wg_ref, wu_ref, wd_ref, o_ref, xn_scr, acc_scr, gsc_scr, *, gated):
    e = pl.program_id(1)
    f = pl.program_id(2)

    @pl.when((e == 0) & (f == 0))
    def _():
        xn_scr[...] = _rms(x_ref[...], g_ref[...]).astype(BF16)
        acc_scr[...] = jnp.zeros_like(acc_scr)

    if gated:
        @pl.when(f == 0)
        def _():
            gates = gate_ref[...]
            lane = lax.broadcasted_iota(jnp.int32, gates.shape, 1)
            gsc_scr[...] = jnp.sum(jnp.where(lane == e, gates, 0.0), axis=-1, keepdims=True)

    xn = xn_scr[...]
    gg = _dot(xn, wg_ref[0])
    uu = _dot(xn, wu_ref[0])
    act = gg * jax.nn.sigmoid(gg) * uu
    if gated:
        act = act * gsc_scr[...]
    acc_scr[...] += _dot(act.astype(BF16), wd_ref[0])

    @pl.when((e == pl.num_programs(1) - 1) & (f == pl.num_programs(2) - 1))
    def _():
        o_ref[...] = x_ref[...] + acc_scr[...]


def swiglu_ffn(x, g, w_gate_up, w_down, gates, *, tm, tf):
    T, D = x.shape
    E, F, _ = w_down.shape
    nf = F // tf
    gated = gates is not None
    if not gated:
        gates = jnp.ones((8, LANES), F32)
        gate_spec = pl.BlockSpec((8, LANES), lambda i, e, f: (0, 0))
    else:
        gate_spec = pl.BlockSpec((tm, LANES), lambda i, e, f: (i, 0))
    return pl.pallas_call(
        functools.partial(_ffn_kernel, gated=gated),
        grid=(T // tm, E, nf),
        in_specs=[pl.BlockSpec((tm, D), lambda i, e, f: (i, 0)),
                  pl.BlockSpec((1, D), lambda i, e, f: (0, 0)),
                  gate_spec,
                  pl.BlockSpec((1, D, tf), lambda i, e, f: (e, 0, f)),
                  pl.BlockSpec((1, D, tf), lambda i, e, f: (e, 0, nf + f)),
                  pl.BlockSpec((1, tf, D), lambda i, e, f: (e, f, 0))],
        out_specs=pl.BlockSpec((tm, D), lambda i, e, f: (i, 0)),
        out_shape=jax.ShapeDtypeStruct((T, D), F32),
        scratch_shapes=[pltpu.VMEM((tm, D), BF16), pltpu.VMEM((tm, D), F32), pltpu.VMEM((tm, 1), F32)],
        compiler_params=_params("parallel", "arbitrary", "arbitrary"),
        name="swiglu_ffn",
    )(x, g.reshape(1, D), gates, w_gate_up, w_gate_up, w_down)


def _ple_kernel(x_ref, g_ref, p_ref, wgate_ref, wproj_ref, o_ref):
    x = x_ref[...]
    gate = jax.nn.sigmoid(_dot(_rms(x, g_ref[...]).astype(BF16), wgate_ref[...]))
    o_ref[...] = x + gate * _dot(p_ref[...].astype(BF16), wproj_ref[...])


def ple_update(x, g, p, w_gate, w_proj, *, tm):
    T, D = x.shape
    PD = p.shape[1]
    return pl.pallas_call(
        _ple_kernel,
        grid=(T // tm,),
        in_specs=[pl.BlockSpec((tm, D), lambda i: (i, 0)),
                  pl.BlockSpec((1, D), lambda i: (0, 0)),
                  pl.BlockSpec((tm, PD), lambda i: (i, 0)),
                  pl.BlockSpec((D, D), lambda i: (0, 0)),
                  pl.BlockSpec((PD, D), lambda i: (0, 0))],
        out_specs=pl.BlockSpec((tm, D), lambda i: (i, 0)),
        out_shape=jax.ShapeDtypeStruct((T, D), F32),
        compiler_params=_params("parallel"),
        name="ple",
    )(x, g.reshape(1, D), p, w_gate, w_proj)


def _row_tile(S, want):
    return min(want, S)


def retention_layer(h, tabs, B, S, g_norm, w_in, gn, w_out):
    tm = _row_tile(S, 1024)
    proj = norm_matmul(h, g_norm, w_in.astype(BF16), tm=tm, tn=min(2048, w_in.shape[1]), out_dtype=BF16)
    o = retention_core(proj, tabs["ret_cos"], tabs["ret_sin"], gn, B, S)
    return matmul_res(o, w_out.astype(BF16), h, tm=_row_tile(S, 512))


def nsa_layer(h, tabs, B, S, g_norm, w_in, q_norm, k_norm, cmp_pos, cmp_w1, cmp_w2, w_out):
    D, n_in = w_in.shape
    n_pad = -(-n_in // (7 * LANES)) * (7 * LANES)
    w_in_p = jnp.zeros((D, n_pad), BF16).at[:, :n_in].set(w_in.astype(BF16))
    proj = norm_matmul(h, g_norm, w_in_p, tm=_row_tile(S, 1024), tn=7 * LANES, out_dtype=BF16)
    kn = jnp.zeros((8, NSA_DH), F32).at[:3].set(k_norm)
    w1 = cmp_w1.astype(BF16).reshape(2, NSA_CMP_LEN, NSA_DH, cmp_w1.shape[-1])
    kcc, vcc = nsa_compress(proj, cmp_pos, w1, cmp_w2.astype(BF16), kn, tabs["nsa_cos"], tabs["nsa_sin"], B, S)
    o = nsa_attention(proj, kcc, vcc, q_norm, kn, tabs["nsa_cos"], tabs["nsa_sin"], B, S)
    return matmul_res(o, w_out.astype(BF16), h, tm=_row_tile(S, 512))


def s5_layer(h, B, S, g_norm, w_in, a_re, a_im, b_re, b_im, c_re, c_im, d_skip, log_dt, w_glu):
    D = h.shape[1]
    tm = _row_tile(S, 1024)
    u = norm_matmul(h, g_norm, w_in.astype(BF16), tm=tm, tn=D, out_dtype=F32, time_major_batch=(B, S))
    y = s5_scan(u.reshape(S * B, D), a_re, a_im, b_re, b_im, c_re, c_im, d_skip, log_dt, B, S)
    return glu_res(y.reshape(S, B * D), w_glu.astype(BF16), h, B, S, tm=_row_tile(S, 512))


def kernel(x, p, positions, norm_mix, norm_ffn, norm_ple, ret_w_in, ret_gn, ret_w_out, nsa_w_in, nsa_q_norm, nsa_k_norm, nsa_cmp_pos, nsa_cmp_w1, nsa_cmp_w2, nsa_w_out, s5_w_in, s5_a_re, s5_a_im, s5_b_re, s5_b_im, s5_c_re, s5_c_im, s5_d, s5_log_dt, s5_w_glu, ffn_w_gate_up, ffn_w_down, moe_router, moe_w_gate_up, moe_w_down, ple_w_proj, ple_w_gate):
    B, S, D = x.shape
    depth = p.shape[0]
    T = B * S
    h = x.reshape(T, D)
    pos_col = positions.reshape(T, 1)

    ones = jnp.ones((1, LANES), F32)
    dk = D // RET_HEADS
    inv_ret = (ROPE_THETA ** (-jnp.arange(0, dk, 2, dtype=F32) / dk)).reshape(1, LANES)
    inv_half = ROPE_THETA ** (-jnp.arange(0, NSA_DH, 2, dtype=F32) / NSA_DH)
    inv_nsa = jnp.concatenate([inv_half, inv_half]).reshape(1, LANES)
    sign_nsa = jnp.concatenate([-jnp.ones((NSA_DH // 2,), F32), jnp.ones((NSA_DH // 2,), F32)]).reshape(1, LANES)
    tabs = {}
    tabs["ret_cos"], tabs["ret_sin"] = rope_tables(pos_col, inv_ret, ones, _row_tile(T, 1024))
    if depth > 1:
        tabs["nsa_cos"], tabs["nsa_sin"] = rope_tables(pos_col, inv_nsa, sign_nsa, _row_tile(T, 1024))

    for i in range(depth):
        m, j = i % 3, i // 3
        if m == 0:
            h = retention_layer(h, tabs, B, S, norm_mix[i], ret_w_in[j], ret_gn[j], ret_w_out[j])
        elif m == 1:
            h = nsa_layer(h, tabs, B, S, norm_mix[i], nsa_w_in[j], nsa_q_norm[j], nsa_k_norm[j],
                          nsa_cmp_pos[j], nsa_cmp_w1[j], nsa_cmp_w2[j], nsa_w_out[j])
        else:
            h = s5_layer(h, B, S, norm_mix[i], s5_w_in[j], s5_a_re[j], s5_a_im[j], s5_b_re[j], s5_b_im[j],
                         s5_c_re[j], s5_c_im[j], s5_d[j], s5_log_dt[j], s5_w_glu[j])
        tm_ffn = _row_tile(T, 1024)
        if i % 2 == 0:
            w_gu = ffn_w_gate_up[i // 2].astype(BF16)[None]
            w_dn = ffn_w_down[i // 2].astype(BF16)[None]
            h = swiglu_ffn(h, norm_ffn[i], w_gu, w_dn, None, tm=tm_ffn, tf=256)
        else:
            gates = router_gates(h, norm_ffn[i], moe_router[i // 2], tm=tm_ffn)
            h = swiglu_ffn(h, norm_ffn[i], moe_w_gate_up[i // 2].astype(BF16), moe_w_down[i // 2].astype(BF16),
                           gates, tm=tm_ffn, tf=512)
        h = ple_update(h, norm_ple[i], p[i].reshape(T, -1), ple_w_gate[i].astype(BF16),
                       ple_w_proj[i].astype(BF16), tm=_row_tile(T, 512))
    return h.reshape(B, S, D)
```

```python
import functools
import math

import numpy as np
import jax
import jax.numpy as jnp
from jax import lax
from jax.experimental import pallas as pl
from jax.experimental.pallas import tpu as pltpu

F32 = jnp.float32
BF16 = jnp.bfloat16

NORM_EPS = 1e-6
ROPE_THETA = 10000.0

RET_HEADS = 4
RET_CHUNK = 128

NSA_HEADS = 8
NSA_GROUPS = 2
NSA_DH = 128
NSA_CMP_LEN = 32
NSA_CMP_STRIDE = 16
NSA_SEL_LEN = 64
NSA_TOP_N = 16
NSA_WINDOW = 512
NSA_Q_TILE = 128
NSA_KV_TILE = 512

S5_GROUP = 16
S5_STATE = 64
S5_SET = 8
S5_TIME_CHUNK = 64

N_EXPERTS = 8
LANES = 128
MASK_NEG = -1e30

VMEM_LIMIT = 56 * 1024 * 1024


def _params(*sem):
    return pltpu.CompilerParams(dimension_semantics=sem, vmem_limit_bytes=VMEM_LIMIT)


def _dot(a, b):
    return jnp.dot(a, b, preferred_element_type=F32)


def _dot_nt(a, b):
    return lax.dot_general(a, b, (((1,), (1,)), ((), ())), preferred_element_type=F32)


def _rms(x, g):
    return x * lax.rsqrt(jnp.mean(x * x, axis=-1, keepdims=True) + NORM_EPS) * g


def _gelu_tanh(x):
    return 0.5 * x * (1.0 + jnp.tanh(math.sqrt(2.0 / math.pi) * (x + 0.044715 * (x * x * x))))


def _rope_table_kernel(pos_ref, inv_ref, sign_ref, cos_ref, sin_ref):
    ang = pos_ref[...].astype(F32) * inv_ref[...]
    cos_ref[...] = jnp.cos(ang)
    sin_ref[...] = jnp.sin(ang) * sign_ref[...]


def rope_tables(pos_col, inv, sign, tm):
    T = pos_col.shape[0]
    return pl.pallas_call(
        _rope_table_kernel,
        grid=(T // tm,),
        in_specs=[pl.BlockSpec((tm, 1), lambda i: (i, 0)),
                  pl.BlockSpec((1, LANES), lambda i: (0, 0)),
                  pl.BlockSpec((1, LANES), lambda i: (0, 0))],
        out_specs=[pl.BlockSpec((tm, LANES), lambda i: (i, 0))] * 2,
        out_shape=[jax.ShapeDtypeStruct((T, LANES), F32)] * 2,
        compiler_params=_params("parallel"),
        name="rope_tables",
    )(pos_col, inv, sign)


def _norm_matmul_kernel(x_ref, g_ref, w_ref, o_ref, xn_ref):
    @pl.when(pl.program_id(1) == 0)
    def _():
        xn_ref[...] = _rms(x_ref[...], g_ref[...]).astype(BF16)

    o_ref[...] = _dot(xn_ref[...], w_ref[...]).astype(o_ref.dtype)


def norm_matmul(x, g, w, *, tm, tn, out_dtype, time_major_batch=None):
    T, D = x.shape
    N = w.shape[1]
    nj = N // tn
    if time_major_batch is None:
        out_shape = jax.ShapeDtypeStruct((T, N), out_dtype)
        out_spec = pl.BlockSpec((tm, tn), lambda i, j: (i, j))
    else:
        B, S = time_major_batch
        ns = S // tm
        out_shape = jax.ShapeDtypeStruct((S, B * N), out_dtype)
        out_spec = pl.BlockSpec((tm, tn), lambda i, j: (i % ns, (i // ns) * nj + j))
    return pl.pallas_call(
        _norm_matmul_kernel,
        grid=(T // tm, nj),
        in_specs=[pl.BlockSpec((tm, D), lambda i, j: (i, 0)),
                  pl.BlockSpec((1, D), lambda i, j: (0, 0)),
                  pl.BlockSpec((D, tn), lambda i, j: (0, j))],
        out_specs=out_spec,
        out_shape=out_shape,
        scratch_shapes=[pltpu.VMEM((tm, D), BF16)],
        compiler_params=_params("parallel", "arbitrary"),
        name="norm_matmul",
    )(x, g.reshape(1, D), w)


def _matmul_res_kernel(x_ref, w_ref, r_ref, o_ref):
    o_ref[...] = r_ref[...] + _dot(x_ref[...], w_ref[...])


def matmul_res(x, w, res, *, tm):
    T, K = x.shape
    N = w.shape[1]
    return pl.pallas_call(
        _matmul_res_kernel,
        grid=(T // tm,),
        in_specs=[pl.BlockSpec((tm, K), lambda i: (i, 0)),
                  pl.BlockSpec((K, N), lambda i: (0, 0)),
                  pl.BlockSpec((tm, N), lambda i: (i, 0))],
        out_specs=pl.BlockSpec((tm, N), lambda i: (i, 0)),
        out_shape=jax.ShapeDtypeStruct((T, N), F32),
        compiler_params=_params("parallel"),
        name="matmul_res",
    )(x, w, res)


def _retention_kernel(q_ref, k_ref, v_ref, g_ref, cos_ref, sin_ref, dmask_ref, qdec_ref, kdec_ref,
                      cdec_ref, gn_ref, o_ref, r_scr):
    @pl.when(pl.program_id(2) == 0)
    def _():
        r_scr[...] = jnp.zeros_like(r_scr)

    cos = cos_ref[...]
    sin = sin_ref[...]
    half = cos.shape[1]

    def rope(x):
        x1, x2 = x[:, :half], x[:, half:]
        return jnp.concatenate([x1 * cos - x2 * sin, x2 * cos + x1 * sin], axis=-1)

    dk = q_ref.shape[1]
    q = rope(q_ref[...].astype(F32))
    k = rope(k_ref[...].astype(F32)) * (dk ** -0.5)
    v = v_ref[...]
    qb = q.astype(BF16)
    scores = _dot_nt(qb, k.astype(BF16)) * dmask_ref[0]
    intra = _dot(scores.astype(BF16), v)
    r_old = r_scr[...]
    cross = _dot(qb, r_old.astype(BF16)) * qdec_ref[0]
    kd_t = (k * kdec_ref[0]).T.astype(BF16)
    r_scr[...] = r_old * cdec_ref[0] + _dot(kd_t, v)

    o = intra + cross
    mu = jnp.mean(o, axis=-1, keepdims=True)
    oc = o - mu
    var = jnp.mean(oc * oc, axis=-1, keepdims=True)
    on = oc * lax.rsqrt(var + NORM_EPS) * gn_ref[...]
    g = g_ref[...].astype(F32)
    o_ref[...] = (on * (g * jax.nn.sigmoid(g))).astype(o_ref.dtype)


def retention_core(proj, cos, sin, gn, B, S):
    H, C = RET_HEADS, RET_CHUNK
    T = proj.shape[0]
    dk = proj.shape[1] // (6 * H)
    dv = 2 * dk
    N = S // C
    log_gamma = jnp.log1p(-jnp.exp2(-5.0 - jnp.arange(H, dtype=F32)))
    idx = jnp.arange(C, dtype=F32)
    diff = idx[:, None] - idx[None, :]
    dmask = jnp.where(diff >= 0, jnp.exp(jnp.maximum(diff, 0.0) * log_gamma[:, None, None]), 0.0)
    qdec = jnp.exp((idx + 1.0) * log_gamma[:, None])[..., None]
    kdec = jnp.exp((C - 1.0 - idx) * log_gamma[:, None])[..., None]
    cdec = jnp.exp(C * log_gamma).reshape(H, 1, 1)
    row = lambda b, h, n: b * N + n
    return pl.pallas_call(
        _retention_kernel,
        grid=(B, H, N),
        in_specs=[pl.BlockSpec((C, dk), lambda b, h, n: (row(b, h, n), h)),
                  pl.BlockSpec((C, dk), lambda b, h, n: (row(b, h, n), H + h)),
                  pl.BlockSpec((C, dv), lambda b, h, n: (row(b, h, n), H + h)),
                  pl.BlockSpec((C, dv), lambda b, h, n: (row(b, h, n), 2 * H + h)),
                  pl.BlockSpec((C, dk // 2), lambda b, h, n: (row(b, h, n), 0)),
                  pl.BlockSpec((C, dk // 2), lambda b, h, n: (row(b, h, n), 0)),
                  pl.BlockSpec((1, C, C), lambda b, h, n: (h, 0, 0)),
                  pl.BlockSpec((1, C, 1), lambda b, h, n: (h, 0, 0)),
                  pl.BlockSpec((1, C, 1), lambda b, h, n: (h, 0, 0)),
                  pl.BlockSpec((1, 1, 1), lambda b, h, n: (h, 0, 0)),
                  pl.BlockSpec((1, dv), lambda b, h, n: (0, h))],
        out_specs=pl.BlockSpec((C, dv), lambda b, h, n: (row(b, h, n), h)),
        out_shape=jax.ShapeDtypeStruct((T, H * dv), BF16),
        scratch_shapes=[pltpu.VMEM((dk, dv), F32)],
        compiler_params=_params("parallel", "parallel", "arbitrary"),
        name="retention",
    )(proj, proj, proj, proj, cos, sin, dmask, qdec, kdec, cdec, gn.reshape(1, H * dv))


def _nsa_compress_kernel(kc_ref, vc_ref, pe_ref, w1_ref, w2_ref, kn_ref, cos_ref, sin_ref,
                         kco_ref, vco_ref, xs_scr):
    nh = NSA_CMP_LEN // NSA_CMP_STRIDE
    n_rows = xs_scr.shape[0] // NSA_CMP_STRIDE
    for br, (src, dst) in enumerate(((kc_ref, kco_ref), (vc_ref, vco_ref))):
        xs_scr[...] = src[...].astype(F32)
        acc = [jnp.zeros((n_rows, w1_ref.shape[-1]), F32) for _ in range(nh)]
        for l in range(NSA_CMP_STRIDE):
            piece = xs_scr[pl.ds(l, n_rows, stride=NSA_CMP_STRIDE), :]
            for a in range(nh):
                ll = a * NSA_CMP_STRIDE + l
                acc[a] = acc[a] + _dot((piece + pe_ref[br, ll:ll + 1, :]).astype(BF16), w1_ref[br, ll])
        hid = acc[0] + pltpu.roll(acc[1], n_rows - 1, axis=0)
        z = _dot(_gelu_tanh(hid).astype(BF16), w2_ref[br])
        if br == 0:
            z = _rms(z, kn_ref[0:1, :])
            cos = cos_ref[pl.ds(0, n_rows, stride=NSA_CMP_STRIDE), :]
            sin = sin_ref[pl.ds(0, n_rows, stride=NSA_CMP_STRIDE), :]
            z = z * cos + pltpu.roll(z, NSA_DH // 2, axis=1) * sin
        dst[0, 0] = z.astype(dst.dtype)


def nsa_compress(proj, pe, w1, w2, kn, cosn, sinn, B, S):
    G, dh = NSA_GROUPS, NSA_DH
    nr = S // NSA_CMP_STRIDE
    kc0 = NSA_HEADS
    vc0 = NSA_HEADS + G
    return pl.pallas_call(
        _nsa_compress_kernel,
        grid=(B, G),
        in_specs=[pl.BlockSpec((S, dh), lambda b, g: (b, kc0 + g)),
                  pl.BlockSpec((S, dh), lambda b, g: (b, vc0 + g)),
                  pl.BlockSpec(pe.shape, lambda b, g: (0, 0, 0)),
                  pl.BlockSpec(w1.shape, lambda b, g: (0, 0, 0, 0)),
                  pl.BlockSpec(w2.shape, lambda b, g: (0, 0, 0)),
                  pl.BlockSpec(kn.shape, lambda b, g: (0, 0)),
                  pl.BlockSpec((S, dh), lambda b, g: (b, 0)),
                  pl.BlockSpec((S, dh), lambda b, g: (b, 0))],
        out_specs=[pl.BlockSpec((1, 1, nr, dh), lambda b, g: (b, g, 0, 0))] * 2,
        out_shape=[jax.ShapeDtypeStruct((B, G, nr, dh), BF16)] * 2,
        scratch_shapes=[pltpu.VMEM((S, dh), F32)],
        compiler_params=_params("parallel", "parallel"),
        name="nsa_compress",
    )(proj, proj, pe, w1, w2, kn, cosn, sinn)


def _softmax_parts(s, ok):
    s = jnp.where(ok, s, MASK_NEG)
    m = jnp.max(s, axis=-1, keepdims=True)
    p = jnp.where(ok, jnp.exp(s - m), 0.0)
    return p, jnp.sum(p, axis=-1, keepdims=True)


def _nsa_attn_kernel(q_ref, gate_ref, cosq_ref, sinq_ref, qn_ref, kc_ref, vc_ref,
                     ks_ref, vs_ref, kw_ref, vw_ref, kn_ref, cosk_ref, sink_ref, mmat_ref,
                     o_ref, ks_scr, kw_scr):
    grp = pl.program_id(1)
    qi = pl.program_id(2)
    tq = q_ref.shape[0]
    dh = NSA_DH
    hg = q_ref.shape[1] // dh
    S = ks_ref.shape[0]
    tk = min(NSA_KV_TILE, S)
    q0 = qi * tq

    @pl.when(qi == 0)
    def _():
        cosk = cosk_ref[...]
        sink = sink_ref[...]
        for src, dst, r in ((ks_ref, ks_scr, 1), (kw_ref, kw_scr, 2)):
            x = _rms(src[...].astype(F32), kn_ref[r:r + 1, :])
            dst[...] = (x * cosk + pltpu.roll(x, dh // 2, axis=1) * sink).astype(BF16)

    cosq = cosq_ref[...]
    sinq = sinq_ref[...]
    qs = []
    for h in range(hg):
        x = _rms(q_ref[:, h * dh:(h + 1) * dh].astype(F32), qn_ref[...])
        x = (x * cosq + pltpu.roll(x, dh // 2, axis=1) * sinq) * (dh ** -0.5)
        qs.append(x.astype(BF16))
    qall = jnp.concatenate(qs, axis=0)

    def token_of_row(n):
        return q0 + jnp.bitwise_and(lax.broadcasted_iota(jnp.int32, (hg * tq, n), 0), tq - 1)

    ncp = kc_ref.shape[2]
    c_i = lax.broadcasted_iota(jnp.int32, (hg * tq, ncp), 1)
    valid_c = c_i * NSA_CMP_STRIDE + (NSA_CMP_LEN - 1) <= token_of_row(ncp)
    p_c, l_c = _softmax_parts(_dot_nt(qall, kc_ref[0, 0]), valid_c)
    p_c = p_c / jnp.where(l_c > 0.0, l_c, 1.0)
    o_c = _dot(p_c.astype(BF16), vc_ref[0, 0])
    p_sum = p_c[0:tq]
    for h in range(1, hg):
        p_sum = p_sum + p_c[h * tq:(h + 1) * tq]
    imp = jnp.dot(p_sum, mmat_ref[...], preferred_element_type=F32, precision=lax.Precision.HIGHEST)

    n_sel = S // NSA_SEL_LEN
    sel_rows = ((n_sel + 7) // 8) * 8
    v_imp = imp.T[0:sel_rows, :]
    jb = lax.broadcasted_iota(jnp.int32, (sel_rows, tq), 0)
    cur = (q0 + lax.broadcasted_iota(jnp.int32, (sel_rows, tq), 1)) // NSA_SEL_LEN
    forced = (jb == 0) | (jb == cur) | (jb == cur - 1)
    v_imp = jnp.where(forced, jnp.inf, jnp.where(jb > cur, -jnp.inf, v_imp))
    if sel_rows > n_sel:
        v_imp = jnp.where(jb >= n_sel, -jnp.inf, v_imp)
    rank = jnp.zeros((sel_rows, tq), jnp.int32)
    for i in range(n_sel):
        r = v_imp[i:i + 1, :]
        beats = (r > v_imp) | ((r == v_imp) & (jb > i))
        rank = rank + beats.astype(jnp.int32)
    sel_t = ((rank < min(NSA_TOP_N, n_sel)) & (jb < n_sel)).astype(F32)
    if sel_rows < LANES:
        sel_t = jnp.concatenate([sel_t, jnp.zeros((LANES - sel_rows, tq), F32)], axis=0)
    sel = sel_t.T.astype(BF16)
    sel = jnp.concatenate([sel] * hg, axis=0)

    row_t = token_of_row(tk)
    lane_k = lax.broadcasted_iota(jnp.int32, (hg * tq, tk), 1)
    e_row = lax.broadcasted_iota(jnp.int32, (LANES, tk), 0)
    e_blk = lax.broadcasted_iota(jnp.int32, (LANES, tk), 1) // NSA_SEL_LEN

    def sel_step(j, carry):
        m, l, acc = carry
        base = pl.multiple_of(j * tk, tk)
        s = _dot_nt(qall, ks_scr[pl.ds(base, tk), :])
        expand = (e_blk + j * (tk // NSA_SEL_LEN) == e_row).astype(BF16)
        chosen = _dot(sel, expand)
        ok = (chosen > 0.5) & (lane_k + base <= row_t)
        s = jnp.where(ok, s, MASK_NEG)
        m_new = jnp.maximum(m, jnp.max(s, axis=-1, keepdims=True))
        alpha = jnp.exp(m - m_new)
        p = jnp.where(ok, jnp.exp(s - m_new), 0.0)
        l = alpha * l + jnp.sum(p, axis=-1, keepdims=True)
        acc = alpha * acc + _dot(p.astype(BF16), vs_ref[pl.ds(base, tk), :])
        return m_new, l, acc

    n_kv = (q0 + tq + tk - 1) // tk
    init = (jnp.full((hg * tq, 1), MASK_NEG, F32), jnp.zeros((hg * tq, 1), F32), jnp.zeros((hg * tq, dh), F32))
    _, l_s, acc_s = lax.fori_loop(0, n_kv, sel_step, init)
    o_s = acc_s / l_s

    span = min(tq + NSA_WINDOW, S)
    start = pl.multiple_of(jnp.maximum(q0 + tq - span, 0), tq)
    dist = token_of_row(span) - (start + lax.broadcasted_iota(jnp.int32, (hg * tq, span), 1))
    ok_w = (dist >= 0) & (dist < NSA_WINDOW)
    p_w, l_w = _softmax_parts(_dot_nt(qall, kw_scr[pl.ds(start, span), :]), ok_w)
    o_w = _dot(p_w.astype(BF16), vw_ref[pl.ds(start, span), :]) / l_w

    gsig = jax.nn.sigmoid(gate_ref[...].astype(F32))
    n_br = 3
    for h in range(hg):
        def gate_col(br):
            lo = h * n_br + br
            hi = lo + hg * n_br
            return jnp.where(grp == 0, gsig[:, lo:lo + 1], gsig[:, hi:hi + 1])
        rows = slice(h * tq, (h + 1) * tq)
        o = gate_col(0) * o_c[rows] + gate_col(1) * o_s[rows] + gate_col(2) * o_w[rows]
        o_ref[:, h * dh:(h + 1) * dh] = o.astype(o_ref.dtype)


def nsa_attention(proj, kcc, vcc, qn, kn, cosn, sinn, B, S):
    H, G, dh = NSA_HEADS, NSA_GROUPS, NSA_DH
    hg = H // G
    T = proj.shape[0]
    tq = min(NSA_Q_TILE, S)
    nq = S // tq
    nr = kcc.shape[2]
    n_c = (S - NSA_CMP_LEN) // NSA_CMP_STRIDE + 1
    n_sel = S // NSA_SEL_LEN
    cs = (np.arange(n_c) * NSA_CMP_STRIDE)[:, None]
    js = (np.arange(n_sel) * NSA_SEL_LEN)[None, :]
    overlap = np.clip(np.minimum(cs + NSA_CMP_LEN, js + NSA_SEL_LEN) - np.maximum(cs, js), 0, None) / NSA_CMP_LEN
    mmat = np.zeros((nr, LANES), np.float32)
    mmat[:n_c, :n_sel] = overlap
    col = lambda base: (lambda b, g, i: (b, base + g))
    return pl.pallas_call(
        _nsa_attn_kernel,
        grid=(B, G, nq),
        in_specs=[pl.BlockSpec((tq, hg * dh), lambda b, g, i: (b * nq + i, g)),
                  pl.BlockSpec((tq, LANES), lambda b, g, i: (b * nq + i, H + 6 * G)),
                  pl.BlockSpec((tq, dh), lambda b, g, i: (b * nq + i, 0)),
                  pl.BlockSpec((tq, dh), lambda b, g, i: (b * nq + i, 0)),
                  pl.BlockSpec((1, dh), lambda b, g, i: (0, 0)),
                  pl.BlockSpec((1, 1, nr, dh), lambda b, g, i: (b, g, 0, 0)),
                  pl.BlockSpec((1, 1, nr, dh), lambda b, g, i: (b, g, 0, 0)),
                  pl.BlockSpec((S, dh), col(H + 2 * G)),
                  pl.BlockSpec((S, dh), col(H + 3 * G)),
                  pl.BlockSpec((S, dh), col(H + 4 * G)),
                  pl.BlockSpec((S, dh), col(H + 5 * G)),
                  pl.BlockSpec(kn.shape, lambda b, g, i: (0, 0)),
                  pl.BlockSpec((S, dh), lambda b, g, i: (b, 0)),
                  pl.BlockSpec((S, dh), lambda b, g, i: (b, 0)),
                  pl.BlockSpec(mmat.shape, lambda b, g, i: (0, 0))],
        out_specs=pl.BlockSpec((tq, hg * dh), lambda b, g, i: (b * nq + i, g)),
        out_shape=jax.ShapeDtypeStruct((T, H * dh), BF16),
        scratch_shapes=[pltpu.VMEM((S, dh), BF16), pltpu.VMEM((S, dh), BF16)],
        compiler_params=_params("parallel", "parallel", "arbitrary"),
        name="nsa_attention",
    )(proj, proj, cosn, sinn, qn.reshape(1, dh), kcc, vcc, proj, proj, proj, proj, kn, cosn, sinn,
      jnp.asarray(mmat))


def _s5_scan_kernel(u_ref, bbd_ref, are_ref, aim_ref, cbd_ref, d_ref, y_ref, bu_scr, xs_scr, st_scr):
    nb = st_scr.shape[0] // 2
    w = are_ref.shape[-1]
    tc = u_ref.shape[0] // nb

    @pl.when(pl.program_id(1) == 0)
    def _():
        st_scr[...] = jnp.zeros_like(st_scr)

    u = u_ref[...]
    bu_scr[...] = _dot(u.astype(BF16), bbd_ref[0])
    a_re = jnp.broadcast_to(are_ref[0], (nb, w))
    a_im = jnp.broadcast_to(aim_ref[0], (nb, w))

    def step(t, carry):
        xr, xi = carry
        r0 = pl.multiple_of(t * nb, nb)
        nxr = a_re * xr - a_im * xi + bu_scr[pl.ds(r0, nb), 0:w]
        nxi = a_re * xi + a_im * xr + bu_scr[pl.ds(r0, nb), w:2 * w]
        xs_scr[pl.ds(r0, nb), 0:w] = nxr
        xs_scr[pl.ds(r0, nb), w:2 * w] = nxi
        return nxr, nxi

    xr, xi = lax.fori_loop(0, tc, step, (st_scr[0:nb, :], st_scr[nb:2 * nb, :]), unroll=4)
    st_scr[0:nb, :] = xr
    st_scr[nb:2 * nb, :] = xi
    y = _dot(xs_scr[...].astype(BF16), cbd_ref[0]) + d_ref[...] * u
    y_ref[...] = _gelu_tanh(y).astype(y_ref.dtype)


def s5_scan(u_tm, a_re, a_im, b_re, b_im, c_re, c_im, d_skip, log_dt, B, S):
    D = u_tm.shape[1]
    Cg, P, NS = S5_GROUP, S5_STATE, S5_SET
    G = D // Cg
    K = G // NS
    w = NS * P
    tc = min(S5_TIME_CHUNK, S)
    dt = jnp.exp(log_dt.astype(F32))[:, None]
    mag = jnp.exp(dt * a_re)
    abar_re, abar_im = mag * jnp.cos(dt * a_im), mag * jnp.sin(dt * a_im)
    den = a_re * a_re + a_im * a_im
    nr_, ni_ = abar_re - 1.0, abar_im
    f_re = (nr_ * a_re + ni_ * a_im) / den
    f_im = (ni_ * a_re - nr_ * a_im) / den
    bbar_re = f_re[..., None] * b_re - f_im[..., None] * b_im
    bbar_im = f_re[..., None] * b_im + f_im[..., None] * b_re
    eye = jnp.eye(NS, dtype=F32)

    def in_blockdiag(bb):
        t = bb.reshape(K, NS, P, Cg).transpose(0, 1, 3, 2)
        return jnp.einsum('kgcp,gh->kgchp', t, eye).reshape(K, NS * Cg, NS * P)

    def out_blockdiag(cc):
        t = cc.reshape(K, NS, Cg, P).transpose(0, 1, 3, 2)
        return jnp.einsum('kgpc,gh->kgphc', t, eye).reshape(K, NS * P, NS * Cg)

    bbd = jnp.concatenate([in_blockdiag(bbar_re), in_blockdiag(bbar_im)], axis=-1).astype(BF16)
    cbd = jnp.concatenate([out_blockdiag(c_re), -out_blockdiag(c_im)], axis=1).astype(BF16)
    are = abar_re.reshape(K, 1, w)
    aim = abar_im.reshape(K, 1, w)
    rows = tc * B
    return pl.pallas_call(
        _s5_scan_kernel,
        grid=(K, S // tc),
        in_specs=[pl.BlockSpec((rows, LANES), lambda k, c: (c, k)),
                  pl.BlockSpec((1, LANES, 2 * w), lambda k, c: (k, 0, 0)),
                  pl.BlockSpec((1, 1, w), lambda k, c: (k, 0, 0)),
                  pl.BlockSpec((1, 1, w), lambda k, c: (k, 0, 0)),
                  pl.BlockSpec((1, 2 * w, LANES), lambda k, c: (k, 0, 0)),
                  pl.BlockSpec((1, LANES), lambda k, c: (0, k))],
        out_specs=pl.BlockSpec((rows, LANES), lambda k, c: (c, k)),
        out_shape=jax.ShapeDtypeStruct((S * B, D), BF16),
        scratch_shapes=[pltpu.VMEM((rows, 2 * w), F32), pltpu.VMEM((rows, 2 * w), F32),
                        pltpu.VMEM((2 * B, w), F32)],
        compiler_params=_params("parallel", "arbitrary"),
        name="s5_scan",
    )(u_tm, bbd, are, aim, cbd, d_skip.reshape(1, D))


def _glu_res_kernel(y_ref, wa_ref, wb_ref, r_ref, o_ref):
    y = y_ref[...]
    a = _dot(y, wa_ref[...])
    b = _dot(y, wb_ref[...])
    o_ref[...] = r_ref[...] + a * jax.nn.sigmoid(b)


def glu_res(y_tm, w, res, B, S, *, tm):
    D = w.shape[0]
    N = w.shape[1] // 2
    ns = S // tm
    return pl.pallas_call(
        _glu_res_kernel,
        grid=(B * ns,),
        in_specs=[pl.BlockSpec((tm, D), lambda i: (i % ns, i // ns)),
                  pl.BlockSpec((D, N), lambda i: (0, 0)),
                  pl.BlockSpec((D, N), lambda i: (0, 1)),
                  pl.BlockSpec((tm, N), lambda i: (i, 0))],
        out_specs=pl.BlockSpec((tm, N), lambda i: (i, 0)),
        out_shape=jax.ShapeDtypeStruct((B * S, N), F32),
        compiler_params=_params("parallel"),
        name="glu_res",
    )(y_tm, w, w, res)


def _ffn_kernel(x_ref, g_ref, wg_ref, wu_ref, wd_ref, o_ref, xn_scr, acc_scr):
    f = pl.program_id(1)

    @pl.when(f == 0)
    def _():
        xn_scr[...] = _rms(x_ref[...], g_ref[...]).astype(BF16)
        acc_scr[...] = jnp.zeros_like(acc_scr)

    xn = xn_scr[...]
    gg = _dot(xn, wg_ref[...])
    uu = _dot(xn, wu_ref[...])
    acc_scr[...] += _dot((gg * jax.nn.sigmoid(gg) * uu).astype(BF16), wd_ref[...])

    @pl.when(f == pl.num_programs(1) - 1)
    def _():
        o_ref[...] = x_ref[...] + acc_scr[...]


def swiglu_ffn(x, g, w_gate_up, w_down, *, tm, tf):
    T, D = x.shape
    F = w_down.shape[0]
    nf = F // tf
    return pl.pallas_call(
        _ffn_kernel,
        grid=(T // tm, nf),
        in_specs=[pl.BlockSpec((tm, D), lambda i, f: (i, 0)),
                  pl.BlockSpec((1, D), lambda i, f: (0, 0)),
                  pl.BlockSpec((D, tf), lambda i, f: (0, f)),
                  pl.BlockSpec((D, tf), lambda i, f: (0, nf + f)),
                  pl.BlockSpec((tf, D), lambda i, f: (f, 0))],
        out_specs=pl.BlockSpec((tm, D), lambda i, f: (i, 0)),
        out_shape=jax.ShapeDtypeStruct((T, D), F32),
        scratch_shapes=[pltpu.VMEM((tm, D), BF16), pltpu.VMEM((tm, D), F32)],
        compiler_params=_params("parallel", "arbitrary"),
        name="swiglu_ffn",
    )(x, g.reshape(1, D), w_gate_up, w_gate_up, w_down)


META_E0, META_E1, META_W0, META_W1, META_R0, META_R1 = range(6)


def _router_kernel(x_ref, g_ref, w_ref, meta_ref, cnt_ref, carry_scr):
    @pl.when(pl.program_id(0) == 0)
    def _():
        carry_scr[...] = jnp.zeros_like(carry_scr)

    xn = _rms(x_ref[...], g_ref[...])
    logits = jnp.dot(xn, w_ref[...], preferred_element_type=F32, precision=lax.Precision.HIGHEST)
    tm = logits.shape[0]
    lane = lax.broadcasted_iota(jnp.int32, logits.shape, 1).astype(F32)
    logits = jnp.where(lane < N_EXPERTS, logits, -jnp.inf)

    def top1(v):
        m = jnp.max(v, axis=-1, keepdims=True)
        idx = jnp.min(jnp.where(v == m, lane, float(LANES)), axis=-1, keepdims=True)
        return m, idx, lane == idx

    m1, e0, hot1 = top1(logits)
    m2, e1, hot2 = top1(jnp.where(hot1, -jnp.inf, logits))
    ex = jnp.exp(m2 - m1)
    w0 = 1.0 / (1.0 + ex)
    w1 = ex * w0

    both = jnp.where(hot1 | hot2, 1.0, 0.0)
    tri = (lax.broadcasted_iota(jnp.int32, (tm, tm), 0) > lax.broadcasted_iota(jnp.int32, (tm, tm), 1))
    before = _dot(tri.astype(BF16), both.astype(BF16)) + carry_scr[0:1, :]
    r0 = jnp.sum(jnp.where(hot1, before, 0.0), axis=-1, keepdims=True)
    r1 = jnp.sum(jnp.where(hot2, before, 0.0), axis=-1, keepdims=True)
    carry_scr[...] = carry_scr[...] + jnp.sum(both, axis=0, keepdims=True)
    cnt_ref[...] = carry_scr[...]

    meta = jnp.zeros_like(logits)
    for k, v in ((META_E0, e0), (META_E1, e1), (META_W0, w0), (META_W1, w1), (META_R0, r0), (META_R1, r1)):
        meta = jnp.where(lane == float(k), v, meta)
    meta_ref[...] = meta


def router_topk(x, g, w_router, *, tm):
    T, D = x.shape
    w = jnp.zeros((D, LANES), F32).at[:, :N_EXPERTS].set(w_router)
    return pl.pallas_call(
        _router_kernel,
        grid=(T // tm,),
        in_specs=[pl.BlockSpec((tm, D), lambda i: (i, 0)),
                  pl.BlockSpec((1, D), lambda i: (0, 0)),
                  pl.BlockSpec((D, LANES), lambda i: (0, 0))],
        out_specs=[pl.BlockSpec((tm, LANES), lambda i: (i, 0)),
                   pl.BlockSpec((8, LANES), lambda i: (0, 0))],
        out_shape=[jax.ShapeDtypeStruct((T, LANES), F32), jax.ShapeDtypeStruct((8, LANES), F32)],
        scratch_shapes=[pltpu.VMEM((8, LANES), F32)],
        compiler_params=_params("arbitrary"),
        name="router",
    )(x, g.reshape(1, D), w)


def _gather_rows(idx_ref, src_hbm, dst_ref, sem):
    n = dst_ref.shape[0]

    def issue(r, carry):
        pltpu.make_async_copy(src_hbm.at[pl.ds(idx_ref[0, 0, r], 1), :], dst_ref.at[pl.ds(r, 1), :], sem).start()
        return carry

    lax.fori_loop(0, n, issue, 0, unroll=8)
    pltpu.make_async_copy(src_hbm.at[pl.ds(0, n), :], dst_ref, sem).wait()


def _moe_expert_kernel(te_ref, nu_ref, tok_ref, x_hbm, g_ref, wg_ref, wu_ref, wd_ref, y_ref,
                       xg_scr, xn_scr, acc_scr, sem):
    i = pl.program_id(0)
    f = pl.program_id(1)
    active = i < nu_ref[0]

    @pl.when(active & (f == 0))
    def _():
        _gather_rows(tok_ref, x_hbm, xg_scr, sem)
        xn_scr[...] = _rms(xg_scr[...], g_ref[...]).astype(BF16)
        acc_scr[...] = jnp.zeros_like(acc_scr)

    @pl.when(active)
    def _():
        xn = xn_scr[...]
        gg = _dot(xn, wg_ref[0])
        uu = _dot(xn, wu_ref[0])
        acc_scr[...] += _dot((gg * jax.nn.sigmoid(gg) * uu).astype(BF16), wd_ref[0])

    @pl.when(f == pl.num_programs(1) - 1)
    def _():
        y_ref[...] = jnp.where(active, acc_scr[...], 0.0)


def moe_expert_ffn(x, g, tok_of_slot, tile_expert, n_used, w_gate_up, w_down, *, tm, tf):
    T, D = x.shape
    E, F, _ = w_down.shape
    nf = F // tf
    n_tiles = tok_of_slot.shape[0]

    def fblk(i, f, nu):
        return jnp.where(i < nu[0], f, nf - 1)

    grid_spec = pltpu.PrefetchScalarGridSpec(
        num_scalar_prefetch=2,
        grid=(n_tiles, nf),
        in_specs=[pl.BlockSpec((1, 1, tm), lambda i, f, te, nu: (i, 0, 0), memory_space=pltpu.SMEM),
                  pl.BlockSpec(memory_space=pl.ANY),
                  pl.BlockSpec((1, D), lambda i, f, te, nu: (0, 0)),
                  pl.BlockSpec((1, D, tf), lambda i, f, te, nu: (te[i], 0, fblk(i, f, nu))),
                  pl.BlockSpec((1, D, tf), lambda i, f, te, nu: (te[i], 0, nf + fblk(i, f, nu))),
                  pl.BlockSpec((1, tf, D), lambda i, f, te, nu: (te[i], fblk(i, f, nu), 0))],
        out_specs=pl.BlockSpec((tm, D), lambda i, f, te, nu: (i, 0)),
        scratch_shapes=[pltpu.VMEM((tm, D), F32), pltpu.VMEM((tm, D), BF16), pltpu.VMEM((tm, D), F32),
                        pltpu.SemaphoreType.DMA(())],
    )
    return pl.pallas_call(
        _moe_expert_kernel,
        grid_spec=grid_spec,
        out_shape=jax.ShapeDtypeStruct((n_tiles * tm, D), F32),
        compiler_params=_params("arbitrary", "arbitrary"),
        name="moe_experts",
    )(tile_expert, n_used, tok_of_slot, x, g.reshape(1, D), w_gate_up, w_gate_up, w_down)


def _moe_combine_kernel(s0_ref, s1_ref, x_ref, meta_ref, y_hbm, o_ref, y0_scr, y1_scr, sems):
    _gather_rows(s0_ref, y_hbm, y0_scr, sems.at[0])
    _gather_rows(s1_ref, y_hbm, y1_scr, sems.at[1])
    meta = meta_ref[...]
    w0 = meta[:, META_W0:META_W0 + 1]
    w1 = meta[:, META_W1:META_W1 + 1]
    o_ref[...] = x_ref[...] + w0 * y0_scr[...] + w1 * y1_scr[...]


def moe_combine(x, meta, slot0, slot1, y, *, tm):
    T, D = x.shape
    smem_idx = pl.BlockSpec((1, 1, tm), lambda i: (i, 0, 0), memory_space=pltpu.SMEM)
    return pl.pallas_call(
        _moe_combine_kernel,
        grid=(T // tm,),
        in_specs=[smem_idx, smem_idx,
                  pl.BlockSpec((tm, D), lambda i: (i, 0)),
                  pl.BlockSpec((tm, LANES), lambda i: (i, 0)),
                  pl.BlockSpec(memory_space=pl.ANY)],
        out_specs=pl.BlockSpec((tm, D), lambda i: (i, 0)),
        out_shape=jax.ShapeDtypeStruct((T, D), F32),
        scratch_shapes=[pltpu.VMEM((tm, D), F32), pltpu.VMEM((tm, D), F32), pltpu.SemaphoreType.DMA((2,))],
        compiler_params=_params("arbitrary"),
        name="moe_combine",
    )(slot0, slot1, x, meta, y)


def moe_layer(x, g, w_router, w_gate_up, w_down, *, tm_route, tm_expert, tm_combine, tf):
    T, D = x.shape
    E = w_down.shape[0]
    n_tiles = (2 * T) // tm_expert + E
    meta, cnt = router_topk(x, g, w_router, tm=tm_route)
    e0 = meta[:, META_E0].astype(jnp.int32)
    e1 = meta[:, META_E1].astype(jnp.int32)
    r0 = meta[:, META_R0].astype(jnp.int32)
    r1 = meta[:, META_R1].astype(jnp.int32)
    counts = cnt[0, :E].astype(jnp.int32)
    tiles_per = (counts + tm_expert - 1) // tm_expert
    tile_end = jnp.cumsum(tiles_per)
    group_start = (tile_end - tiles_per) * tm_expert
    slot0 = group_start[e0] + r0
    slot1 = group_start[e1] + r1
    n_used = tile_end[-1:]
    tile_ids = jnp.arange(n_tiles, dtype=jnp.int32)
    tile_expert = jnp.searchsorted(tile_end, jnp.minimum(tile_ids, n_used[0] - 1), side="right").astype(jnp.int32)
    tile_expert = jnp.minimum(tile_expert, E - 1)
    tok = jnp.arange(T, dtype=jnp.int32)
    tok_of_slot = jnp.zeros((n_tiles * tm_expert,), jnp.int32).at[slot0].set(tok).at[slot1].set(tok)
    y = moe_expert_ffn(x, g, tok_of_slot.reshape(n_tiles, 1, tm_expert), tile_expert, n_used.astype(jnp.int32),
                       w_gate_up, w_down, tm=tm_expert, tf=tf)
    return moe_combine(x, meta, slot0.reshape(T // tm_combine, 1, tm_combine),
                       slot1.reshape(T // tm_combine, 1, tm_combine), y, tm=tm_combine)


def _ple_kernel(x_ref, g_ref, p_ref, wgate_ref, wproj_ref, o_ref):
    x = x_ref[...]
    gate = jax.nn.sigmoid(_dot(_rms(x, g_ref[...]).astype(BF16), wgate_ref[...]))
    o_ref[...] = x + gate * _dot(p_ref[...].astype(BF16), wproj_ref[...])


def ple_update(x, g, p, w_gate, w_proj, *, tm):
    T, D = x.shape
    PD = p.shape[1]
    return pl.pallas_call(
        _ple_kernel,
        grid=(T // tm,),
        in_specs=[pl.BlockSpec((tm, D), lambda i: (i, 0)),
                  pl.BlockSpec((1, D), lambda i: (0, 0)),
                  pl.BlockSpec((tm, PD), lambda i: (i, 0)),
                  pl.BlockSpec((D, D), lambda i: (0, 0)),
                  pl.BlockSpec((PD, D), lambda i: (0, 0))],
        out_specs=pl.BlockSpec((tm, D), lambda i: (i, 0)),
        out_shape=jax.ShapeDtypeStruct((T, D), F32),
        compiler_params=_params("parallel"),
        name="ple",
    )(x, g.reshape(1, D), p, w_gate, w_proj)


def _row_tile(S, want):
    return min(want, S)


def retention_layer(h, tabs, B, S, g_norm, w_in, gn, w_out):
    tm = _row_tile(S, 1024)
    proj = norm_matmul(h, g_norm, w_in.astype(BF16), tm=tm, tn=min(2048, w_in.shape[1]), out_dtype=BF16)
    o = retention_core(proj, tabs["ret_cos"], tabs["ret_sin"], gn, B, S)
    return matmul_res(o, w_out.astype(BF16), h, tm=_row_tile(S, 512))


def nsa_layer(h, tabs, B, S, g_norm, w_in, q_norm, k_norm, cmp_pos, cmp_w1, cmp_w2, w_out):
    D, n_in = w_in.shape
    n_pad = -(-n_in // (7 * LANES)) * (7 * LANES)
    w_in_p = jnp.zeros((D, n_pad), BF16).at[:, :n_in].set(w_in.astype(BF16))
    proj = norm_matmul(h, g_norm, w_in_p, tm=_row_tile(S, 1024), tn=7 * LANES, out_dtype=BF16)
    kn = jnp.zeros((8, NSA_DH), F32).at[:3].set(k_norm)
    w1 = cmp_w1.astype(BF16).reshape(2, NSA_CMP_LEN, NSA_DH, cmp_w1.shape[-1])
    kcc, vcc = nsa_compress(proj, cmp_pos, w1, cmp_w2.astype(BF16), kn, tabs["nsa_cos"], tabs["nsa_sin"], B, S)
    o = nsa_attention(proj, kcc, vcc, q_norm, kn, tabs["nsa_cos"], tabs["nsa_sin"], B, S)
    return matmul_res(o, w_out.astype(BF16), h, tm=_row_tile(S, 512))


def s5_layer(h, B, S, g_norm, w_in, a_re, a_im, b_re, b_im, c_re, c_im, d_skip, log_dt, w_glu):
    D = h.shape[1]
    tm = _row_tile(S, 1024)
    u = norm_matmul(h, g_norm, w_in.astype(BF16), tm=tm, tn=D, out_dtype=F32, time_major_batch=(B, S))
    y = s5_scan(u.reshape(S * B, D), a_re, a_im, b_re, b_im, c_re, c_im, d_skip, log_dt, B, S)
    return glu_res(y.reshape(S, B * D), w_glu.astype(BF16), h, B, S, tm=_row_tile(S, 512))


def kernel(x, p, positions, norm_mix, norm_ffn, norm_ple, ret_w_in, ret_gn, ret_w_out, nsa_w_in, nsa_q_norm, nsa_k_norm, nsa_cmp_pos, nsa_cmp_w1, nsa_cmp_w2, nsa_w_out, s5_w_in, s5_a_re, s5_a_im, s5_b_re, s5_b_im, s5_c_re, s5_c_im, s5_d, s5_log_dt, s5_w_glu, ffn_w_gate_up, ffn_w_down, moe_router, moe_w_gate_up, moe_w_down, ple_w_proj, ple_w_gate):
    B, S, D = x.shape
    depth = p.shape[0]
    T = B * S
    h = x.reshape(T, D)
    pos_col = positions.reshape(T, 1)

    ones = jnp.ones((1, LANES), F32)
    dk = D // RET_HEADS
    inv_ret = (ROPE_THETA ** (-jnp.arange(0, dk, 2, dtype=F32) / dk)).reshape(1, LANES)
    inv_half = ROPE_THETA ** (-jnp.arange(0, NSA_DH, 2, dtype=F32) / NSA_DH)
    inv_nsa = jnp.concatenate([inv_half, inv_half]).reshape(1, LANES)
    sign_nsa = jnp.concatenate([-jnp.ones((NSA_DH // 2,), F32), jnp.ones((NSA_DH // 2,), F32)]).reshape(1, LANES)
    tabs = {}
    tabs["ret_cos"], tabs["ret_sin"] = rope_tables(pos_col, inv_ret, ones, _row_tile(T, 1024))
    if depth > 1:
        tabs["nsa_cos"], tabs["nsa_sin"] = rope_tables(pos_col, inv_nsa, sign_nsa, _row_tile(T, 1024))

    for i in range(depth):
        m, j = i % 3, i // 3
        if m == 0:
            h = retention_layer(h, tabs, B, S, norm_mix[i], ret_w_in[j], ret_gn[j], ret_w_out[j])
        elif m == 1:
            h = nsa_layer(h, tabs, B, S, norm_mix[i], nsa_w_in[j], nsa_q_norm[j], nsa_k_norm[j],
                          nsa_cmp_pos[j], nsa_cmp_w1[j], nsa_cmp_w2[j], nsa_w_out[j])
        else:
            h = s5_layer(h, B, S, norm_mix[i], s5_w_in[j], s5_a_re[j], s5_a_im[j], s5_b_re[j], s5_b_im[j],
                         s5_c_re[j], s5_c_im[j], s5_d[j], s5_log_dt[j], s5_w_glu[j])
        tm_ffn = _row_tile(T, 1024)
        if i % 2 == 0:
            h = swiglu_ffn(h, norm_ffn[i], ffn_w_gate_up[i // 2].astype(BF16), ffn_w_down[i // 2].astype(BF16),
                           tm=tm_ffn, tf=256)
        else:
            h = moe_layer(h, norm_ffn[i], moe_router[i // 2], moe_w_gate_up[i // 2].astype(BF16),
                          moe_w_down[i // 2].astype(BF16), tm_route=tm_ffn, tm_expert=tm_ffn,
                          tm_combine=_row_tile(T, 512), tf=512)
        h = ple_update(h, norm_ple[i], p[i].reshape(T, -1), ple_w_gate[i].astype(BF16),
                       ple_w_proj[i].astype(BF16), tm=_row_tile(T, 512))
    return h.reshape(B, S, D)
```

```python
import functools
import math

import numpy as np
import jax
import jax.numpy as jnp
from jax import lax
from jax.experimental import pallas as pl
from jax.experimental.pallas import tpu as pltpu

F32 = jnp.float32
BF16 = jnp.bfloat16

NORM_EPS = 1e-6
ROPE_THETA = 10000.0

RET_HEADS = 4
RET_CHUNK = 128
RET_BLOCK = 512

NSA_HEADS = 8
NSA_GROUPS = 2
NSA_DH = 128
NSA_CMP_LEN = 32
NSA_CMP_STRIDE = 16
NSA_SEL_LEN = 64
NSA_TOP_N = 16
NSA_WINDOW = 512
NSA_Q_TILE = 128
NSA_KV_TILE = 512

S5_GROUP = 16
S5_STATE = 64
S5_SET = 8
S5_TIME_CHUNK = 64

N_EXPERTS = 8
MOE_NF = 7
LANES = 128
MASK_NEG = -1e30

VMEM_LIMIT = 56 * 1024 * 1024


def _params(*sem):
    return pltpu.CompilerParams(dimension_semantics=sem, vmem_limit_bytes=VMEM_LIMIT)


def _dot(a, b):
    return jnp.dot(a, b, preferred_element_type=F32)


def _dot_nt(a, b):
    return lax.dot_general(a, b, (((1,), (1,)), ((), ())), preferred_element_type=F32)


def _rms(x, g):
    return x * lax.rsqrt(jnp.mean(x * x, axis=-1, keepdims=True) + NORM_EPS) * g


def _gelu_tanh(x):
    return 0.5 * x * (1.0 + jnp.tanh(math.sqrt(2.0 / math.pi) * (x + 0.044715 * (x * x * x))))


def _rope_table_kernel(pos_ref, inv_ref, sign_ref, cos_ref, sin_ref):
    ang = pos_ref[...].astype(F32) * inv_ref[...]
    cos_ref[...] = jnp.cos(ang)
    sin_ref[...] = jnp.sin(ang) * sign_ref[...]


def rope_tables(pos_col, inv, sign, tm):
    T = pos_col.shape[0]
    return pl.pallas_call(
        _rope_table_kernel,
        grid=(T // tm,),
        in_specs=[pl.BlockSpec((tm, 1), lambda i: (i, 0)),
                  pl.BlockSpec((1, LANES), lambda i: (0, 0)),
                  pl.BlockSpec((1, LANES), lambda i: (0, 0))],
        out_specs=[pl.BlockSpec((tm, LANES), lambda i: (i, 0))] * 2,
        out_shape=[jax.ShapeDtypeStruct((T, LANES), F32)] * 2,
        compiler_params=_params("parallel"),
        name="rope_tables",
    )(pos_col, inv, sign)


def _norm_matmul_kernel(x_ref, g_ref, w_ref, o_ref, xn_ref):
    @pl.when(pl.program_id(1) == 0)
    def _():
        xn_ref[...] = _rms(x_ref[...], g_ref[...]).astype(BF16)

    o_ref[...] = _dot(xn_ref[...], w_ref[...]).astype(o_ref.dtype)


def norm_matmul(x, g, w, *, tm, tn, out_dtype):
    T, D = x.shape
    N = w.shape[1]
    return pl.pallas_call(
        _norm_matmul_kernel,
        grid=(T // tm, N // tn),
        in_specs=[pl.BlockSpec((tm, D), lambda i, j: (i, 0)),
                  pl.BlockSpec((1, D), lambda i, j: (0, 0)),
                  pl.BlockSpec((D, tn), lambda i, j: (0, j))],
        out_specs=pl.BlockSpec((tm, tn), lambda i, j: (i, j)),
        out_shape=jax.ShapeDtypeStruct((T, N), out_dtype),
        scratch_shapes=[pltpu.VMEM((tm, D), BF16)],
        compiler_params=_params("parallel", "arbitrary"),
        name="norm_matmul",
    )(x, g.reshape(1, D), w)


def _matmul_res_kernel(x_ref, w_ref, r_ref, o_ref):
    o_ref[...] = r_ref[...] + _dot(x_ref[...], w_ref[...])


def matmul_res(x, w, res, *, tm):
    T, K = x.shape
    N = w.shape[1]
    return pl.pallas_call(
        _matmul_res_kernel,
        grid=(T // tm,),
        in_specs=[pl.BlockSpec((tm, K), lambda i: (i, 0)),
                  pl.BlockSpec((K, N), lambda i: (0, 0)),
                  pl.BlockSpec((tm, N), lambda i: (i, 0))],
        out_specs=pl.BlockSpec((tm, N), lambda i: (i, 0)),
        out_shape=jax.ShapeDtypeStruct((T, N), F32),
        compiler_params=_params("parallel"),
        name="matmul_res",
    )(x, w, res)


def _retention_kernel(q_ref, k_ref, v_ref, g_ref, cos_ref, sin_ref, dmask_ref, qdec_ref, kdec_ref,
                      cdec_ref, gn_ref, o_ref, r_scr):
    H, dk, dv = r_scr.shape
    C = dmask_ref.shape[1]
    half = dk // 2

    @pl.when(pl.program_id(1) == 0)
    def _():
        r_scr[...] = jnp.zeros_like(r_scr)

    for c in range(q_ref.shape[0] // C):
        rows = slice(c * C, (c + 1) * C)
        cos = cos_ref[rows, :]
        sin = sin_ref[rows, :]

        def rope(x):
            x1, x2 = x[:, :half], x[:, half:]
            return jnp.concatenate([x1 * cos - x2 * sin, x2 * cos + x1 * sin], axis=-1)

        for h in range(H):
            q = rope(q_ref[rows, h * dk:(h + 1) * dk].astype(F32))
            k = rope(k_ref[rows, h * dk:(h + 1) * dk].astype(F32))
            v = v_ref[rows, h * dv:(h + 1) * dv]
            qb = q.astype(BF16)
            scores = _dot_nt(qb, k.astype(BF16)) * dmask_ref[h]
            intra = _dot(scores.astype(BF16), v)
            r_old = r_scr[h]
            cross = _dot(qb, r_old.astype(BF16)) * qdec_ref[h]
            kd_t = (k * kdec_ref[h]).T.astype(BF16)
            r_scr[h] = r_old * cdec_ref[h] + _dot(kd_t, v)

            o = intra + cross
            mu = jnp.mean(o, axis=-1, keepdims=True)
            oc = o - mu
            var = jnp.mean(oc * oc, axis=-1, keepdims=True)
            on = oc * lax.rsqrt(var + NORM_EPS) * gn_ref[:, h * dv:(h + 1) * dv]
            g = g_ref[rows, h * dv:(h + 1) * dv].astype(F32)
            o_ref[rows, h * dv:(h + 1) * dv] = (on * (g * jax.nn.sigmoid(g))).astype(o_ref.dtype)


def retention_core(proj, cos, sin, gn, B, S):
    H, C = RET_HEADS, RET_CHUNK
    T = proj.shape[0]
    dk = proj.shape[1] // (6 * H)
    dv = 2 * dk
    tb = min(RET_BLOCK, S)
    nb = S // tb
    scale = dk ** -0.5
    log_gamma = jnp.log1p(-jnp.exp2(-5.0 - jnp.arange(H, dtype=F32)))
    idx = jnp.arange(C, dtype=F32)
    diff = idx[:, None] - idx[None, :]
    dmask = jnp.where(diff >= 0, jnp.exp(jnp.maximum(diff, 0.0) * log_gamma[:, None, None]), 0.0) * scale
    qdec = jnp.exp((idx + 1.0) * log_gamma[:, None])[..., None]
    kdec = jnp.exp((C - 1.0 - idx) * log_gamma[:, None])[..., None] * scale
    cdec = jnp.exp(C * log_gamma).reshape(H, 1, 1)
    whole = lambda shape: pl.BlockSpec(shape, lambda b, n: (0,) * len(shape))
    return pl.pallas_call(
        _retention_kernel,
        grid=(B, nb),
        in_specs=[pl.BlockSpec((tb, H * dk), lambda b, n: (b * nb + n, 0)),
                  pl.BlockSpec((tb, H * dk), lambda b, n: (b * nb + n, 1)),
                  pl.BlockSpec((tb, H * dv), lambda b, n: (b * nb + n, 1)),
                  pl.BlockSpec((tb, H * dv), lambda b, n: (b * nb + n, 2)),
                  pl.BlockSpec((tb, dk // 2), lambda b, n: (b * nb + n, 0)),
                  pl.BlockSpec((tb, dk // 2), lambda b, n: (b * nb + n, 0)),
                  whole((H, C, C)), whole((H, C, 1)), whole((H, C, 1)), whole((H, 1, 1)),
                  whole((1, H * dv))],
        out_specs=pl.BlockSpec((tb, H * dv), lambda b, n: (b * nb + n, 0)),
        out_shape=jax.ShapeDtypeStruct((T, H * dv), BF16),
        scratch_shapes=[pltpu.VMEM((H, dk, dv), F32)],
        compiler_params=_params("parallel", "arbitrary"),
        name="retention",
    )(proj, proj, proj, proj, cos, sin, dmask, qdec, kdec, cdec, gn.reshape(1, H * dv))


def _nsa_compress_kernel(kc_ref, vc_ref, pe_ref, w1_ref, w2_ref, kn_ref, cos_ref, sin_ref,
                         kco_ref, vco_ref, xs_scr):
    nh = NSA_CMP_LEN // NSA_CMP_STRIDE
    n_rows = xs_scr.shape[0] // NSA_CMP_STRIDE
    for br, (src, dst) in enumerate(((kc_ref, kco_ref), (vc_ref, vco_ref))):
        xs_scr[...] = src[...].astype(F32)
        acc = [jnp.zeros((n_rows, w1_ref.shape[-1]), F32) for _ in range(nh)]
        for l in range(NSA_CMP_STRIDE):
            piece = xs_scr[pl.ds(l, n_rows, stride=NSA_CMP_STRIDE), :]
            for a in range(nh):
                ll = a * NSA_CMP_STRIDE + l
                acc[a] = acc[a] + _dot((piece + pe_ref[br, ll:ll + 1, :]).astype(BF16), w1_ref[br, ll])
        hid = acc[0] + pltpu.roll(acc[1], n_rows - 1, axis=0)
        z = _dot(_gelu_tanh(hid).astype(BF16), w2_ref[br])
        if br == 0:
            z = _rms(z, kn_ref[0:1, :])
            cos = cos_ref[pl.ds(0, n_rows, stride=NSA_CMP_STRIDE), :]
            sin = sin_ref[pl.ds(0, n_rows, stride=NSA_CMP_STRIDE), :]
            z = z * cos + pltpu.roll(z, NSA_DH // 2, axis=1) * sin
        dst[0, 0] = z.astype(dst.dtype)


def nsa_compress(proj, pe, w1, w2, kn, cosn, sinn, B, S):
    G, dh = NSA_GROUPS, NSA_DH
    nr = S // NSA_CMP_STRIDE
    kc0 = NSA_HEADS
    vc0 = NSA_HEADS + G
    return pl.pallas_call(
        _nsa_compress_kernel,
        grid=(B, G),
        in_specs=[pl.BlockSpec((S, dh), lambda b, g: (b, kc0 + g)),
                  pl.BlockSpec((S, dh), lambda b, g: (b, vc0 + g)),
                  pl.BlockSpec(pe.shape, lambda b, g: (0, 0, 0)),
                  pl.BlockSpec(w1.shape, lambda b, g: (0, 0, 0, 0)),
                  pl.BlockSpec(w2.shape, lambda b, g: (0, 0, 0)),
                  pl.BlockSpec(kn.shape, lambda b, g: (0, 0)),
                  pl.BlockSpec((S, dh), lambda b, g: (b, 0)),
                  pl.BlockSpec((S, dh), lambda b, g: (b, 0))],
        out_specs=[pl.BlockSpec((1, 1, nr, dh), lambda b, g: (b, g, 0, 0))] * 2,
        out_shape=[jax.ShapeDtypeStruct((B, G, nr, dh), BF16)] * 2,
        scratch_shapes=[pltpu.VMEM((S, dh), F32)],
        compiler_params=_params("parallel", "parallel"),
        name="nsa_compress",
    )(proj, proj, pe, w1, w2, kn, cosn, sinn)


def _softmax_parts(s, ok):
    s = jnp.where(ok, s, MASK_NEG)
    m = jnp.max(s, axis=-1, keepdims=True)
    p = jnp.where(ok, jnp.exp(s - m), 0.0)
    return p, jnp.sum(p, axis=-1, keepdims=True)


def _nsa_attn_kernel(q_ref, gate_ref, cosq_ref, sinq_ref, qn_ref, kc_ref, vc_ref,
                     ks_ref, vs_ref, kw_ref, vw_ref, kn_ref, cosk_ref, sink_ref, mmat_ref,
                     o_ref, ks_scr, kw_scr):
    grp = pl.program_id(1)
    qi = pl.program_id(2)
    tq = q_ref.shape[0]
    dh = NSA_DH
    hg = q_ref.shape[1] // dh
    S = ks_ref.shape[0]
    tk = min(NSA_KV_TILE, S)
    q0 = qi * tq

    @pl.when(qi == 0)
    def _():
        cosk = cosk_ref[...]
        sink = sink_ref[...]
        for src, dst, r in ((ks_ref, ks_scr, 1), (kw_ref, kw_scr, 2)):
            x = _rms(src[...].astype(F32), kn_ref[r:r + 1, :])
            dst[...] = (x * cosk + pltpu.roll(x, dh // 2, axis=1) * sink).astype(BF16)

    cosq = cosq_ref[...]
    sinq = sinq_ref[...]
    qs = []
    for h in range(hg):
        x = _rms(q_ref[:, h * dh:(h + 1) * dh].astype(F32), qn_ref[...])
        x = (x * cosq + pltpu.roll(x, dh // 2, axis=1) * sinq) * (dh ** -0.5)
        qs.append(x.astype(BF16))
    qall = jnp.concatenate(qs, axis=0)

    def token_of_row(n):
        return q0 + jnp.bitwise_and(lax.broadcasted_iota(jnp.int32, (hg * tq, n), 0), tq - 1)

    ncp = kc_ref.shape[2]
    c_i = lax.broadcasted_iota(jnp.int32, (hg * tq, ncp), 1)
    valid_c = c_i * NSA_CMP_STRIDE + (NSA_CMP_LEN - 1) <= token_of_row(ncp)
    p_c, l_c = _softmax_parts(_dot_nt(qall, kc_ref[0, 0]), valid_c)
    p_c = p_c / jnp.where(l_c > 0.0, l_c, 1.0)
    o_c = _dot(p_c.astype(BF16), vc_ref[0, 0])
    p_sum = p_c[0:tq]
    for h in range(1, hg):
        p_sum = p_sum + p_c[h * tq:(h + 1) * tq]
    imp = jnp.dot(p_sum, mmat_ref[...], preferred_element_type=F32, precision=lax.Precision.HIGHEST)

    n_sel = S // NSA_SEL_LEN
    sel_rows = ((n_sel + 7) // 8) * 8
    v_imp = imp.T[0:sel_rows, :]
    jb = lax.broadcasted_iota(jnp.int32, (sel_rows, tq), 0)
    cur = (q0 + lax.broadcasted_iota(jnp.int32, (sel_rows, tq), 1)) // NSA_SEL_LEN
    forced = (jb == 0) | (jb == cur) | (jb == cur - 1)
    v_imp = jnp.where(forced, jnp.inf, jnp.where(jb > cur, -jnp.inf, v_imp))
    if sel_rows > n_sel:
        v_imp = jnp.where(jb >= n_sel, -jnp.inf, v_imp)
    rank = jnp.zeros((sel_rows, tq), jnp.int32)
    for i in range(n_sel):
        r = v_imp[i:i + 1, :]
        beats = (r > v_imp) | ((r == v_imp) & (jb > i))
        rank = rank + beats.astype(jnp.int32)
    sel_t = ((rank < min(NSA_TOP_N, n_sel)) & (jb < n_sel)).astype(F32)
    if sel_rows < LANES:
        sel_t = jnp.concatenate([sel_t, jnp.zeros((LANES - sel_rows, tq), F32)], axis=0)
    sel = sel_t.T.astype(BF16)
    sel = jnp.concatenate([sel] * hg, axis=0)

    row_t = token_of_row(tk)
    lane_k = lax.broadcasted_iota(jnp.int32, (hg * tq, tk), 1)
    e_row = lax.broadcasted_iota(jnp.int32, (LANES, tk), 0)
    e_blk = lax.broadcasted_iota(jnp.int32, (LANES, tk), 1) // NSA_SEL_LEN

    def sel_step(j, carry):
        m, l, acc = carry
        base = pl.multiple_of(j * tk, tk)
        s = _dot_nt(qall, ks_scr[pl.ds(base, tk), :])
        expand = (e_blk + j * (tk // NSA_SEL_LEN) == e_row).astype(BF16)
        chosen = _dot(sel, expand)
        ok = (chosen > 0.5) & (lane_k + base <= row_t)
        s = jnp.where(ok, s, MASK_NEG)
        m_new = jnp.maximum(m, jnp.max(s, axis=-1, keepdims=True))
        alpha = jnp.exp(m - m_new)
        p = jnp.where(ok, jnp.exp(s - m_new), 0.0)
        l = alpha * l + jnp.sum(p, axis=-1, keepdims=True)
        acc = alpha * acc + _dot(p.astype(BF16), vs_ref[pl.ds(base, tk), :])
        return m_new, l, acc

    n_kv = (q0 + tq + tk - 1) // tk
    init = (jnp.full((hg * tq, 1), MASK_NEG, F32), jnp.zeros((hg * tq, 1), F32), jnp.zeros((hg * tq, dh), F32))
    _, l_s, acc_s = lax.fori_loop(0, n_kv, sel_step, init)
    o_s = acc_s / l_s

    span = min(tq + NSA_WINDOW, S)
    start = pl.multiple_of(jnp.maximum(q0 + tq - span, 0), tq)
    dist = token_of_row(span) - (start + lax.broadcasted_iota(jnp.int32, (hg * tq, span), 1))
    ok_w = (dist >= 0) & (dist < NSA_WINDOW)
    p_w, l_w = _softmax_parts(_dot_nt(qall, kw_scr[pl.ds(start, span), :]), ok_w)
    o_w = _dot(p_w.astype(BF16), vw_ref[pl.ds(start, span), :]) / l_w

    gsig = jax.nn.sigmoid(gate_ref[...].astype(F32))
    n_br = 3
    for h in range(hg):
        def gate_col(br):
            lo = h * n_br + br
            hi = lo + hg * n_br
            return jnp.where(grp == 0, gsig[:, lo:lo + 1], gsig[:, hi:hi + 1])
        rows = slice(h * tq, (h + 1) * tq)
        o = gate_col(0) * o_c[rows] + gate_col(1) * o_s[rows] + gate_col(2) * o_w[rows]
        o_ref[:, h * dh:(h + 1) * dh] = o.astype(o_ref.dtype)


def nsa_attention(proj, kcc, vcc, qn, kn, cosn, sinn, B, S):
    H, G, dh = NSA_HEADS, NSA_GROUPS, NSA_DH
    hg = H // G
    T = proj.shape[0]
    tq = min(NSA_Q_TILE, S)
    nq = S // tq
    nr = kcc.shape[2]
    n_c = (S - NSA_CMP_LEN) // NSA_CMP_STRIDE + 1
    n_sel = S // NSA_SEL_LEN
    cs = (np.arange(n_c) * NSA_CMP_STRIDE)[:, None]
    js = (np.arange(n_sel) * NSA_SEL_LEN)[None, :]
    overlap = np.clip(np.minimum(cs + NSA_CMP_LEN, js + NSA_SEL_LEN) - np.maximum(cs, js), 0, None) / NSA_CMP_LEN
    mmat = np.zeros((nr, LANES), np.float32)
    mmat[:n_c, :n_sel] = overlap
    col = lambda base: (lambda b, g, i: (b, base + g))
    return pl.pallas_call(
        _nsa_attn_kernel,
        grid=(B, G, nq),
        in_specs=[pl.BlockSpec((tq, hg * dh), lambda b, g, i: (b * nq + i, g)),
                  pl.BlockSpec((tq, LANES), lambda b, g, i: (b * nq + i, H + 6 * G)),
                  pl.BlockSpec((tq, dh), lambda b, g, i: (b * nq + i, 0)),
                  pl.BlockSpec((tq, dh), lambda b, g, i: (b * nq + i, 0)),
                  pl.BlockSpec((1, dh), lambda b, g, i: (0, 0)),
                  pl.BlockSpec((1, 1, nr, dh), lambda b, g, i: (b, g, 0, 0)),
                  pl.BlockSpec((1, 1, nr, dh), lambda b, g, i: (b, g, 0, 0)),
                  pl.BlockSpec((S, dh), col(H + 2 * G)),
                  pl.BlockSpec((S, dh), col(H + 3 * G)),
                  pl.BlockSpec((S, dh), col(H + 4 * G)),
                  pl.BlockSpec((S, dh), col(H + 5 * G)),
                  pl.BlockSpec(kn.shape, lambda b, g, i: (0, 0)),
                  pl.BlockSpec((S, dh), lambda b, g, i: (b, 0)),
                  pl.BlockSpec((S, dh), lambda b, g, i: (b, 0)),
                  pl.BlockSpec(mmat.shape, lambda b, g, i: (0, 0))],
        out_specs=pl.BlockSpec((tq, hg * dh), lambda b, g, i: (b * nq + i, g)),
        out_shape=jax.ShapeDtypeStruct((T, H * dh), BF16),
        scratch_shapes=[pltpu.VMEM((S, dh), BF16), pltpu.VMEM((S, dh), BF16)],
        compiler_params=_params("parallel", "parallel", "arbitrary"),
        name="nsa_attention",
    )(proj, proj, cosn, sinn, qn.reshape(1, dh), kcc, vcc, proj, proj, proj, proj, kn, cosn, sinn,
      jnp.asarray(mmat))


def _s5_scan_kernel(u_ref, bbd_ref, are_ref, aim_ref, cbd_ref, d_ref, y_ref, bu_scr, xs_scr, st_scr):
    nb, tc, lanes = u_ref.shape
    w = are_ref.shape[-1]

    @pl.when(pl.program_id(1) == 0)
    def _():
        st_scr[...] = jnp.zeros_like(st_scr)

    nw = w // lanes
    u = u_ref[...].reshape(nb * tc, lanes)
    bu = _dot(u.astype(BF16), bbd_ref[0])
    for j in range(2 * nw):
        bu_scr[j] = bu[:, j * lanes:(j + 1) * lanes]
    a_re = [jnp.broadcast_to(are_ref[0, :, j * lanes:(j + 1) * lanes], (nb, lanes)) for j in range(nw)]
    a_im = [jnp.broadcast_to(aim_ref[0, :, j * lanes:(j + 1) * lanes], (nb, lanes)) for j in range(nw)]

    def step(t, carry):
        rows = pl.ds(t, nb, stride=tc)
        new = []
        for j in range(nw):
            xr, xi = carry[j], carry[nw + j]
            nxr = a_re[j] * xr - a_im[j] * xi + bu_scr[j, rows, :]
            nxi = a_re[j] * xi + a_im[j] * xr + bu_scr[nw + j, rows, :]
            xs_scr[j, rows, :] = nxr
            xs_scr[nw + j, rows, :] = nxi
            new.append((nxr, nxi))
        return tuple(n[0] for n in new) + tuple(n[1] for n in new)

    init = tuple(st_scr[j] for j in range(2 * nw))
    final = lax.fori_loop(0, tc, step, init, unroll=4)
    for j in range(2 * nw):
        st_scr[j] = final[j]
    xs = jnp.concatenate([xs_scr[j].astype(BF16) for j in range(2 * nw)], axis=-1)
    y = _dot(xs, cbd_ref[0]) + d_ref[...] * u
    y_ref[...] = _gelu_tanh(y).astype(y_ref.dtype).reshape(nb, tc, lanes)


def s5_scan(u, a_re, a_im, b_re, b_im, c_re, c_im, d_skip, log_dt):
    B, S, D = u.shape
    Cg, P, NS = S5_GROUP, S5_STATE, S5_SET
    G = D // Cg
    K = G // NS
    w = NS * P
    tc = min(S5_TIME_CHUNK, S)
    dt = jnp.exp(log_dt.astype(F32))[:, None]
    mag = jnp.exp(dt * a_re)
    abar_re, abar_im = mag * jnp.cos(dt * a_im), mag * jnp.sin(dt * a_im)
    den = a_re * a_re + a_im * a_im
    nr_, ni_ = abar_re - 1.0, abar_im
    f_re = (nr_ * a_re + ni_ * a_im) / den
    f_im = (ni_ * a_re - nr_ * a_im) / den
    bbar_re = f_re[..., None] * b_re - f_im[..., None] * b_im
    bbar_im = f_re[..., None] * b_im + f_im[..., None] * b_re
    eye = jnp.eye(NS, dtype=F32)

    def in_blockdiag(bb):
        t = bb.reshape(K, NS, P, Cg).transpose(0, 1, 3, 2)
        return jnp.einsum('kgcp,gh->kgchp', t, eye).reshape(K, NS * Cg, NS * P)

    def out_blockdiag(cc):
        t = cc.reshape(K, NS, Cg, P).transpose(0, 1, 3, 2)
        return jnp.einsum('kgpc,gh->kgphc', t, eye).reshape(K, NS * P, NS * Cg)

    bbd = jnp.concatenate([in_blockdiag(bbar_re), in_blockdiag(bbar_im)], axis=-1).astype(BF16)
    cbd = jnp.concatenate([out_blockdiag(c_re), -out_blockdiag(c_im)], axis=1).astype(BF16)
    are = abar_re.reshape(K, 1, w)
    aim = abar_im.reshape(K, 1, w)
    rows = tc * B
    return pl.pallas_call(
        _s5_scan_kernel,
        grid=(K, S // tc),
        in_specs=[pl.BlockSpec((B, tc, LANES), lambda k, c: (0, c, k)),
                  pl.BlockSpec((1, LANES, 2 * w), lambda k, c: (k, 0, 0)),
                  pl.BlockSpec((1, 1, w), lambda k, c: (k, 0, 0)),
                  pl.BlockSpec((1, 1, w), lambda k, c: (k, 0, 0)),
                  pl.BlockSpec((1, 2 * w, LANES), lambda k, c: (k, 0, 0)),
                  pl.BlockSpec((1, LANES), lambda k, c: (0, k))],
        out_specs=pl.BlockSpec((B, tc, LANES), lambda k, c: (0, c, k)),
        out_shape=jax.ShapeDtypeStruct((B, S, D), BF16),
        scratch_shapes=[pltpu.VMEM((2 * w // LANES, rows, LANES), F32), pltpu.VMEM((2 * w // LANES, rows, LANES), F32),
                        pltpu.VMEM((2 * w // LANES, B, LANES), F32)],
        compiler_params=_params("parallel", "arbitrary"),
        name="s5_scan",
    )(u, bbd, are, aim, cbd, d_skip.reshape(1, D))


def _glu_res_kernel(y_ref, wa_ref, wb_ref, r_ref, o_ref):
    y = y_ref[...]
    a = _dot(y, wa_ref[...])
    b = _dot(y, wb_ref[...])
    o_ref[...] = r_ref[...] + a * jax.nn.sigmoid(b)


def glu_res(y, w, res, *, tm):
    T, D = y.shape
    N = w.shape[1] // 2
    return pl.pallas_call(
        _glu_res_kernel,
        grid=(T // tm,),
        in_specs=[pl.BlockSpec((tm, D), lambda i: (i, 0)),
                  pl.BlockSpec((D, N), lambda i: (0, 0)),
                  pl.BlockSpec((D, N), lambda i: (0, 1)),
                  pl.BlockSpec((tm, N), lambda i: (i, 0))],
        out_specs=pl.BlockSpec((tm, N), lambda i: (i, 0)),
        out_shape=jax.ShapeDtypeStruct((T, N), F32),
        compiler_params=_params("parallel"),
        name="glu_res",
    )(y, w, w, res)


def _ffn_kernel(x_ref, g_ref, wg_ref, wu_ref, wd_ref, o_ref, xn_scr, acc_scr):
    f = pl.program_id(1)

    @pl.when(f == 0)
    def _():
        xn_scr[...] = _rms(x_ref[...], g_ref[...]).astype(BF16)
        acc_scr[...] = jnp.zeros_like(acc_scr)

    xn = xn_scr[...]
    gg = _dot(xn, wg_ref[...])
    uu = _dot(xn, wu_ref[...])
    acc_scr[...] += _dot((gg * jax.nn.sigmoid(gg) * uu).astype(BF16), wd_ref[...])

    @pl.when(f == pl.num_programs(1) - 1)
    def _():
        o_ref[...] = x_ref[...] + acc_scr[...]


def swiglu_ffn(x, g, w_gate_up, w_down, *, tm, tf):
    T, D = x.shape
    F = w_down.shape[0]
    nf = F // tf
    return pl.pallas_call(
        _ffn_kernel,
        grid=(T // tm, nf),
        in_specs=[pl.BlockSpec((tm, D), lambda i, f: (i, 0)),
                  pl.BlockSpec((1, D), lambda i, f: (0, 0)),
                  pl.BlockSpec((D, tf), lambda i, f: (0, f)),
                  pl.BlockSpec((D, tf), lambda i, f: (0, nf + f)),
                  pl.BlockSpec((tf, D), lambda i, f: (f, 0))],
        out_specs=pl.BlockSpec((tm, D), lambda i, f: (i, 0)),
        out_shape=jax.ShapeDtypeStruct((T, D), F32),
        scratch_shapes=[pltpu.VMEM((tm, D), BF16), pltpu.VMEM((tm, D), F32)],
        compiler_params=_params("parallel", "arbitrary"),
        name="swiglu_ffn",
    )(x, g.reshape(1, D), w_gate_up, w_gate_up, w_down)


META_E0, META_E1, META_W0, META_W1, META_R0, META_R1 = range(6)


def _router_kernel(x_ref, g_ref, w_ref, meta_ref, cnt_ref, carry_scr):
    @pl.when(pl.program_id(0) == 0)
    def _():
        carry_scr[...] = jnp.zeros_like(carry_scr)

    xn = _rms(x_ref[...], g_ref[...])
    logits = jnp.dot(xn, w_ref[...], preferred_element_type=F32, precision=lax.Precision.HIGHEST)
    tm = logits.shape[0]
    lane = lax.broadcasted_iota(jnp.int32, logits.shape, 1).astype(F32)
    logits = jnp.where(lane < N_EXPERTS, logits, -jnp.inf)

    def top1(v):
        m = jnp.max(v, axis=-1, keepdims=True)
        idx = jnp.min(jnp.where(v == m, lane, float(LANES)), axis=-1, keepdims=True)
        return m, idx, lane == idx

    m1, e0, hot1 = top1(logits)
    m2, e1, hot2 = top1(jnp.where(hot1, -jnp.inf, logits))
    ex = jnp.exp(m2 - m1)
    w0 = 1.0 / (1.0 + ex)
    w1 = ex * w0

    both = jnp.where(hot1 | hot2, 1.0, 0.0)
    tri = (lax.broadcasted_iota(jnp.int32, (tm, tm), 0) > lax.broadcasted_iota(jnp.int32, (tm, tm), 1))
    before = _dot(tri.astype(BF16), both.astype(BF16)) + carry_scr[0:1, :]
    r0 = jnp.sum(jnp.where(hot1, before, 0.0), axis=-1, keepdims=True)
    r1 = jnp.sum(jnp.where(hot2, before, 0.0), axis=-1, keepdims=True)
    carry_scr[...] = carry_scr[...] + jnp.sum(both, axis=0, keepdims=True)
    cnt_ref[...] = carry_scr[...]

    meta = jnp.zeros_like(logits)
    for k, v in ((META_E0, e0), (META_E1, e1), (META_W0, w0), (META_W1, w1), (META_R0, r0), (META_R1, r1)):
        meta = jnp.where(lane == float(k), v, meta)
    meta_ref[...] = meta


def router_topk(x, g, w_router, *, tm):
    T, D = x.shape
    w = jnp.zeros((D, LANES), F32).at[:, :N_EXPERTS].set(w_router)
    return pl.pallas_call(
        _router_kernel,
        grid=(T // tm,),
        in_specs=[pl.BlockSpec((tm, D), lambda i: (i, 0)),
                  pl.BlockSpec((1, D), lambda i: (0, 0)),
                  pl.BlockSpec((D, LANES), lambda i: (0, 0))],
        out_specs=[pl.BlockSpec((tm, LANES), lambda i: (i, 0)),
                   pl.BlockSpec((8, LANES), lambda i: (0, 0))],
        out_shape=[jax.ShapeDtypeStruct((T, LANES), F32), jax.ShapeDtypeStruct((8, LANES), F32)],
        scratch_shapes=[pltpu.VMEM((8, LANES), F32)],
        compiler_params=_params("arbitrary"),
        name="router",
    )(x, g.reshape(1, D), w)


def _start_row_gathers(idx_ref, src_hbm, dst_ref, sem, lo, hi):
    for r in range(lo, hi):
        pltpu.make_async_copy(src_hbm.at[pl.ds(idx_ref[0, 0, r], 1), :], dst_ref.at[pl.ds(r, 1), :], sem).start()


def _wait_row_gathers(src_hbm, dst_ref, sem):
    pltpu.make_async_copy(src_hbm.at[pl.ds(0, dst_ref.shape[0]), :], dst_ref, sem).wait()


def _moe_expert_kernel(te_ref, nu_ref, tok_ref, tok_next_ref, x_hbm, g_ref, wg_ref, wu_ref, wd_ref, y_ref,
                       xg_scr, xn_scr, acc_scr, sems):
    i = pl.program_id(0)
    f = pl.program_id(1)
    n_tiles = pl.num_programs(0)
    nf = pl.num_programs(1)
    n_used = nu_ref[0]
    active = i < n_used
    tm = xn_scr.shape[0]
    cur = xg_scr.at[i % 2]
    nxt = xg_scr.at[(i + 1) % 2]
    sem_cur = sems.at[i % 2]
    sem_nxt = sems.at[(i + 1) % 2]

    @pl.when((i == 0) & (f == 0))
    def _():
        def issue(r, carry):
            pltpu.make_async_copy(x_hbm.at[pl.ds(tok_ref[0, 0, r], 1), :], cur.at[pl.ds(r, 1), :], sem_cur).start()
            return carry
        lax.fori_loop(0, tm, issue, 0, unroll=8)

    @pl.when((f == 0) & (i <= n_used))
    def _():
        _wait_row_gathers(x_hbm, cur, sem_cur)

    @pl.when(active & (f == 0))
    def _():
        xn_scr[...] = _rms(cur[...], g_ref[...]).astype(BF16)
        acc_scr[...] = jnp.zeros_like(acc_scr)

    def step(lo, hi):
        xn = xn_scr[...]
        gg = _dot(xn, wg_ref[0])
        uu = _dot(xn, wu_ref[0])
        acc_scr[...] += _dot((gg * jax.nn.sigmoid(gg) * uu).astype(BF16), wd_ref[0])
        _start_row_gathers(tok_next_ref, x_hbm, nxt, sem_nxt, lo, hi)

    per = -(-tm // MOE_NF)
    for fs in range(MOE_NF):
        @pl.when(active & (f == fs))
        def _(fs=fs):
            step(min(fs * per, tm), min((fs + 1) * per, tm))

    @pl.when(f == nf - 1)
    def _():
        y_ref[...] = jnp.where(active, acc_scr[...], 0.0)

    @pl.when(active & (i == n_tiles - 1) & (f == nf - 1))
    def _():
        _wait_row_gathers(x_hbm, nxt, sem_nxt)


def moe_expert_ffn(x, g, tok_of_slot, tile_expert, n_used, w_gate_up, w_down, layer, *, tm):
    T, D = x.shape
    _, E, F, _ = w_down.shape
    nf = MOE_NF
    tf = F // nf
    n_tiles = tok_of_slot.shape[0]

    def fblk(i, f, nu):
        return jnp.where(i < nu[0], f, nf - 1)

    tok_spec = lambda off: pl.BlockSpec((1, 1, tm), lambda i, f, te, nu: (jnp.minimum(i + off, n_tiles - 1), 0, 0),
                                        memory_space=pltpu.SMEM)
    grid_spec = pltpu.PrefetchScalarGridSpec(
        num_scalar_prefetch=2,
        grid=(n_tiles, nf),
        in_specs=[tok_spec(0), tok_spec(1),
                  pl.BlockSpec(memory_space=pl.ANY),
                  pl.BlockSpec((1, D), lambda i, f, te, nu: (0, 0)),
                  pl.BlockSpec((None, 1, D, tf), lambda i, f, te, nu: (layer, te[i], 0, fblk(i, f, nu))),
                  pl.BlockSpec((None, 1, D, tf), lambda i, f, te, nu: (layer, te[i], 0, nf + fblk(i, f, nu))),
                  pl.BlockSpec((None, 1, tf, D), lambda i, f, te, nu: (layer, te[i], fblk(i, f, nu), 0))],
        out_specs=pl.BlockSpec((tm, D), lambda i, f, te, nu: (i, 0)),
        scratch_shapes=[pltpu.VMEM((2, tm, D), F32), pltpu.VMEM((tm, D), BF16), pltpu.VMEM((tm, D), F32),
                        pltpu.SemaphoreType.DMA((2,))],
    )
    return pl.pallas_call(
        _moe_expert_kernel,
        grid_spec=grid_spec,
        out_shape=jax.ShapeDtypeStruct((n_tiles * tm, D), F32),
        compiler_params=_params("arbitrary", "arbitrary"),
        name="moe_experts",
    )(tile_expert, n_used, tok_of_slot, tok_of_slot, x, g.reshape(1, D), w_gate_up, w_gate_up, w_down)


def _moe_combine_kernel(s0_ref, s1_ref, x_ref, meta_ref, y_hbm, o_ref, y0_scr, y1_scr, sems):
    n = y0_scr.shape[0]
    _start_row_gathers(s0_ref, y_hbm, y0_scr, sems.at[0], 0, n)
    _start_row_gathers(s1_ref, y_hbm, y1_scr, sems.at[1], 0, n)
    _wait_row_gathers(y_hbm, y0_scr, sems.at[0])
    _wait_row_gathers(y_hbm, y1_scr, sems.at[1])
    meta = meta_ref[...]
    w0 = meta[:, META_W0:META_W0 + 1]
    w1 = meta[:, META_W1:META_W1 + 1]
    o_ref[...] = x_ref[...] + w0 * y0_scr[...] + w1 * y1_scr[...]


def moe_combine(x, meta, slot0, slot1, y, *, tm):
    T, D = x.shape
    smem_idx = pl.BlockSpec((1, 1, tm), lambda i: (i, 0, 0), memory_space=pltpu.SMEM)
    return pl.pallas_call(
        _moe_combine_kernel,
        grid=(T // tm,),
        in_specs=[smem_idx, smem_idx,
                  pl.BlockSpec((tm, D), lambda i: (i, 0)),
                  pl.BlockSpec((tm, LANES), lambda i: (i, 0)),
                  pl.BlockSpec(memory_space=pl.ANY)],
        out_specs=pl.BlockSpec((tm, D), lambda i: (i, 0)),
        out_shape=jax.ShapeDtypeStruct((T, D), F32),
        scratch_shapes=[pltpu.VMEM((tm, D), F32), pltpu.VMEM((tm, D), F32), pltpu.SemaphoreType.DMA((2,))],
        compiler_params=_params("arbitrary"),
        name="moe_combine",
    )(slot0, slot1, x, meta, y)


def moe_layer(x, g, w_router, w_gate_up, w_down, layer, *, tm_route, tm_expert, tm_combine):
    T, D = x.shape
    E = w_down.shape[1]
    n_tiles = (2 * T) // tm_expert + E
    meta, cnt = router_topk(x, g, w_router, tm=tm_route)
    e0 = meta[:, META_E0].astype(jnp.int32)
    e1 = meta[:, META_E1].astype(jnp.int32)
    r0 = meta[:, META_R0].astype(jnp.int32)
    r1 = meta[:, META_R1].astype(jnp.int32)
    counts = cnt[0, :E].astype(jnp.int32)
    tiles_per = (counts + tm_expert - 1) // tm_expert
    tile_end = jnp.cumsum(tiles_per)
    group_start = (tile_end - tiles_per) * tm_expert
    slot0 = group_start[e0] + r0
    slot1 = group_start[e1] + r1
    n_used = tile_end[-1:]
    tile_ids = jnp.arange(n_tiles, dtype=jnp.int32)
    tile_expert = jnp.searchsorted(tile_end, jnp.minimum(tile_ids, n_used[0] - 1), side="right").astype(jnp.int32)
    tile_expert = jnp.minimum(tile_expert, E - 1)
    tok = jnp.arange(T, dtype=jnp.int32)
    tok_of_slot = jnp.zeros((n_tiles * tm_expert,), jnp.int32).at[slot0].set(tok).at[slot1].set(tok)
    y = moe_expert_ffn(x, g, tok_of_slot.reshape(n_tiles, 1, tm_expert), tile_expert, n_used.astype(jnp.int32),
                       w_gate_up, w_down, layer, tm=tm_expert)
    return moe_combine(x, meta, slot0.reshape(T // tm_combine, 1, tm_combine),
                       slot1.reshape(T // tm_combine, 1, tm_combine), y, tm=tm_combine)


def _ple_kernel(x_ref, g_ref, p_ref, wgate_ref, wproj_ref, o_ref):
    x = x_ref[...]
    gate = jax.nn.sigmoid(_dot(_rms(x, g_ref[...]).astype(BF16), wgate_ref[...]))
    o_ref[...] = x + gate * _dot(p_ref[...].astype(BF16), wproj_ref[...])


def ple_update(x, g, p_all, layer, w_gate, w_proj, *, tm):
    T, D = x.shape
    PD = p_all.shape[2]
    return pl.pallas_call(
        _ple_kernel,
        grid=(T // tm,),
        in_specs=[pl.BlockSpec((tm, D), lambda i: (i, 0)),
                  pl.BlockSpec((1, D), lambda i: (0, 0)),
                  pl.BlockSpec((None, tm, PD), lambda i: (layer, i, 0)),
                  pl.BlockSpec((D, D), lambda i: (0, 0)),
                  pl.BlockSpec((PD, D), lambda i: (0, 0))],
        out_specs=pl.BlockSpec((tm, D), lambda i: (i, 0)),
        out_shape=jax.ShapeDtypeStruct((T, D), F32),
        compiler_params=_params("parallel"),
        name="ple",
    )(x, g.reshape(1, D), p_all, w_gate, w_proj)


def _row_tile(S, want):
    return min(want, S)


def retention_layer(h, tabs, B, S, g_norm, w_in, gn, w_out):
    tm = _row_tile(S, 1024)
    proj = norm_matmul(h, g_norm, w_in.astype(BF16), tm=tm, tn=min(2048, w_in.shape[1]), out_dtype=BF16)
    o = retention_core(proj, tabs["ret_cos"], tabs["ret_sin"], gn, B, S)
    return matmul_res(o, w_out.astype(BF16), h, tm=_row_tile(S, 512))


def nsa_layer(h, tabs, B, S, g_norm, w_in, q_norm, k_norm, cmp_pos, cmp_w1, cmp_w2, w_out):
    D, n_in = w_in.shape
    n_pad = -(-n_in // (7 * LANES)) * (7 * LANES)
    w_in_p = jnp.zeros((D, n_pad), BF16).at[:, :n_in].set(w_in.astype(BF16))
    proj = norm_matmul(h, g_norm, w_in_p, tm=_row_tile(S, 1024), tn=7 * LANES, out_dtype=BF16)
    kn = jnp.zeros((8, NSA_DH), F32).at[:3].set(k_norm)
    w1 = cmp_w1.astype(BF16).reshape(2, NSA_CMP_LEN, NSA_DH, cmp_w1.shape[-1])
    kcc, vcc = nsa_compress(proj, cmp_pos, w1, cmp_w2.astype(BF16), kn, tabs["nsa_cos"], tabs["nsa_sin"], B, S)
    o = nsa_attention(proj, kcc, vcc, q_norm, kn, tabs["nsa_cos"], tabs["nsa_sin"], B, S)
    return matmul_res(o, w_out.astype(BF16), h, tm=_row_tile(S, 512))


def s5_layer(h, B, S, g_norm, w_in, a_re, a_im, b_re, b_im, c_re, c_im, d_skip, log_dt, w_glu):
    D = h.shape[1]
    tm = _row_tile(S, 1024)
    u = norm_matmul(h, g_norm, w_in.astype(BF16), tm=tm, tn=D, out_dtype=F32)
    y = s5_scan(u.reshape(B, S, D), a_re, a_im, b_re, b_im, c_re, c_im, d_skip, log_dt)
    return glu_res(y.reshape(B * S, D), w_glu.astype(BF16), h, tm=_row_tile(S, 512))


def kernel(x, p, positions, norm_mix, norm_ffn, norm_ple, ret_w_in, ret_gn, ret_w_out, nsa_w_in, nsa_q_norm, nsa_k_norm, nsa_cmp_pos, nsa_cmp_w1, nsa_cmp_w2, nsa_w_out, s5_w_in, s5_a_re, s5_a_im, s5_b_re, s5_b_im, s5_c_re, s5_c_im, s5_d, s5_log_dt, s5_w_glu, ffn_w_gate_up, ffn_w_down, moe_router, moe_w_gate_up, moe_w_down, ple_w_proj, ple_w_gate):
    B, S, D = x.shape
    depth = p.shape[0]
    T = B * S
    h = x.reshape(T, D)
    pos_col = positions.reshape(T, 1)
    p_all = p.reshape(depth, T, p.shape[-1])
    moe_gu = moe_w_gate_up.astype(BF16)
    moe_dn = moe_w_down.astype(BF16)

    ones = jnp.ones((1, LANES), F32)
    dk = D // RET_HEADS
    inv_ret = (ROPE_THETA ** (-jnp.arange(0, dk, 2, dtype=F32) / dk)).reshape(1, LANES)
    inv_half = ROPE_THETA ** (-jnp.arange(0, NSA_DH, 2, dtype=F32) / NSA_DH)
    inv_nsa = jnp.concatenate([inv_half, inv_half]).reshape(1, LANES)
    sign_nsa = jnp.concatenate([-jnp.ones((NSA_DH // 2,), F32), jnp.ones((NSA_DH // 2,), F32)]).reshape(1, LANES)
    tabs = {}
    tabs["ret_cos"], tabs["ret_sin"] = rope_tables(pos_col, inv_ret, ones, _row_tile(T, 1024))
    if depth > 1:
        tabs["nsa_cos"], tabs["nsa_sin"] = rope_tables(pos_col, inv_nsa, sign_nsa, _row_tile(T, 1024))

    for i in range(depth):
        m, j = i % 3, i // 3
        if m == 0:
            h = retention_layer(h, tabs, B, S, norm_mix[i], ret_w_in[j], ret_gn[j], ret_w_out[j])
        elif m == 1:
            h = nsa_layer(h, tabs, B, S, norm_mix[i], nsa_w_in[j], nsa_q_norm[j], nsa_k_norm[j],
                          nsa_cmp_pos[j], nsa_cmp_w1[j], nsa_cmp_w2[j], nsa_w_out[j])
        else:
            h = s5_layer(h, B, S, norm_mix[i], s5_w_in[j], s5_a_re[j], s5_a_im[j], s5_b_re[j], s5_b_im[j],
                         s5_c_re[j], s5_c_im[j], s5_d[j], s5_log_dt[j], s5_w_glu[j])
        tm_ffn = _row_tile(T, 1024)
        if i % 2 == 0:
            h = swiglu_ffn(h, norm_ffn[i], ffn_w_gate_up[i // 2].astype(BF16), ffn_w_down[i // 2].astype(BF16),
                           tm=tm_ffn, tf=256)
        else:
            h = moe_layer(h, norm_ffn[i], moe_router[i // 2], moe_gu, moe_dn, i // 2,
                          tm_route=tm_ffn, tm_expert=tm_ffn, tm_combine=_row_tile(T, 512))
        h = ple_update(h, norm_ple[i], p_all, i, ple_w_gate[i].astype(BF16),
                       ple_w_proj[i].astype(BF16), tm=_row_tile(T, 512))
    return h.reshape(B, S, D)
```

```python
import functools
import math

import numpy as np
import jax
import jax.numpy as jnp
from jax import lax
from jax.experimental import pallas as pl
from jax.experimental.pallas import tpu as pltpu

F32 = jnp.float32
BF16 = jnp.bfloat16

NORM_EPS = 1e-6
ROPE_THETA = 10000.0

RET_HEADS = 4
RET_CHUNK = 128
RET_BLOCK = 512

NSA_HEADS = 8
NSA_GROUPS = 2
NSA_DH = 128
NSA_CMP_LEN = 32
NSA_CMP_STRIDE = 16
NSA_SEL_LEN = 64
NSA_TOP_N = 16
NSA_WINDOW = 512
NSA_Q_TILE = 128
NSA_KV_TILE = 512

S5_GROUP = 16
S5_STATE = 64
S5_SET = 8
S5_TIME_CHUNK = 64

N_EXPERTS = 8
MOE_NF = 7
LANES = 128
MASK_NEG = -1e30
LOG2_E = 1.4426950408889634

VMEM_LIMIT = 56 * 1024 * 1024


def _params(*sem):
    return pltpu.CompilerParams(dimension_semantics=sem, vmem_limit_bytes=VMEM_LIMIT)


def _dot(a, b):
    return jnp.dot(a, b, preferred_element_type=F32)


def _dot_nt(a, b):
    return lax.dot_general(a, b, (((1,), (1,)), ((), ())), preferred_element_type=F32)


def _rms(x, g):
    return x * lax.rsqrt(jnp.mean(x * x, axis=-1, keepdims=True) + NORM_EPS) * g


def _gelu_tanh(x):
    return 0.5 * x * (1.0 + jnp.tanh(math.sqrt(2.0 / math.pi) * (x + 0.044715 * (x * x * x))))


def _rope_table_kernel(pos_ref, inv_ref, sign_ref, cos_ref, sin_ref):
    ang = pos_ref[...].astype(F32) * inv_ref[...]
    cos_ref[...] = jnp.cos(ang)
    sin_ref[...] = jnp.sin(ang) * sign_ref[...]


def rope_tables(pos_col, inv, sign, tm):
    T = pos_col.shape[0]
    return pl.pallas_call(
        _rope_table_kernel,
        grid=(T // tm,),
        in_specs=[pl.BlockSpec((tm, 1), lambda i: (i, 0)),
                  pl.BlockSpec((1, LANES), lambda i: (0, 0)),
                  pl.BlockSpec((1, LANES), lambda i: (0, 0))],
        out_specs=[pl.BlockSpec((tm, LANES), lambda i: (i, 0))] * 2,
        out_shape=[jax.ShapeDtypeStruct((T, LANES), F32)] * 2,
        compiler_params=_params("parallel"),
        name="rope_tables",
    )(pos_col, inv, sign)


def _norm_matmul_kernel(x_ref, g_ref, w_ref, o_ref, xn_ref):
    @pl.when(pl.program_id(1) == 0)
    def _():
        xn_ref[...] = _rms(x_ref[...], g_ref[...]).astype(BF16)

    o_ref[...] = _dot(xn_ref[...], w_ref[...]).astype(o_ref.dtype)


def norm_matmul(x, g, w, *, tm, tn, out_dtype, time_major_batch=None):
    T, D = x.shape
    N = w.shape[1]
    nj = N // tn
    if time_major_batch is None:
        out_shape = jax.ShapeDtypeStruct((T, N), out_dtype)
        out_spec = pl.BlockSpec((tm, tn), lambda i, j: (i, j))
    else:
        B, S = time_major_batch
        ns = S // tm
        out_shape = jax.ShapeDtypeStruct((S, B * N), out_dtype)
        out_spec = pl.BlockSpec((tm, tn), lambda i, j: (i % ns, (i // ns) * nj + j))
    return pl.pallas_call(
        _norm_matmul_kernel,
        grid=(T // tm, nj),
        in_specs=[pl.BlockSpec((tm, D), lambda i, j: (i, 0)),
                  pl.BlockSpec((1, D), lambda i, j: (0, 0)),
                  pl.BlockSpec((D, tn), lambda i, j: (0, j))],
        out_specs=out_spec,
        out_shape=out_shape,
        scratch_shapes=[pltpu.VMEM((tm, D), BF16)],
        compiler_params=_params("parallel", "arbitrary"),
        name="norm_matmul",
    )(x, g.reshape(1, D), w)


def _matmul_res_kernel(x_ref, w_ref, r_ref, o_ref):
    o_ref[...] = r_ref[...] + _dot(x_ref[...], w_ref[...])


def matmul_res(x, w, res, *, tm):
    T, K = x.shape
    N = w.shape[1]
    return pl.pallas_call(
        _matmul_res_kernel,
        grid=(T // tm,),
        in_specs=[pl.BlockSpec((tm, K), lambda i: (i, 0)),
                  pl.BlockSpec((K, N), lambda i: (0, 0)),
                  pl.BlockSpec((tm, N), lambda i: (i, 0))],
        out_specs=pl.BlockSpec((tm, N), lambda i: (i, 0)),
        out_shape=jax.ShapeDtypeStruct((T, N), F32),
        compiler_params=_params("parallel"),
        name="matmul_res",
    )(x, w, res)


def _retention_kernel(q_ref, k_ref, v_ref, g_ref, cos_ref, sin_ref, dmask_ref, qdec_ref, kdec_ref,
                      cdec_ref, gn_ref, o_ref, r_scr):
    H, dk, dv = r_scr.shape
    C = dmask_ref.shape[1]
    half = dk // 2

    @pl.when(pl.program_id(1) == 0)
    def _():
        r_scr[...] = jnp.zeros_like(r_scr)

    for c in range(q_ref.shape[0] // C):
        rows = slice(c * C, (c + 1) * C)
        cos = cos_ref[rows, :]
        sin = sin_ref[rows, :]

        def rope(x):
            x1, x2 = x[:, :half], x[:, half:]
            return jnp.concatenate([x1 * cos - x2 * sin, x2 * cos + x1 * sin], axis=-1)

        for h in range(H):
            q = rope(q_ref[rows, h * dk:(h + 1) * dk].astype(F32))
            k = rope(k_ref[rows, h * dk:(h + 1) * dk].astype(F32))
            v = v_ref[rows, h * dv:(h + 1) * dv]
            qb = q.astype(BF16)
            scores = _dot_nt(qb, k.astype(BF16)) * dmask_ref[h]
            intra = _dot(scores.astype(BF16), v)
            r_old = r_scr[h]
            cross = _dot(qb, r_old.astype(BF16)) * qdec_ref[h]
            kd_t = (k * kdec_ref[h]).T.astype(BF16)
            r_scr[h] = r_old * cdec_ref[h] + _dot(kd_t, v)

            o = intra + cross
            mu = jnp.mean(o, axis=-1, keepdims=True)
            oc = o - mu
            var = jnp.mean(oc * oc, axis=-1, keepdims=True)
            on = oc * lax.rsqrt(var + NORM_EPS) * gn_ref[:, h * dv:(h + 1) * dv]
            g = g_ref[rows, h * dv:(h + 1) * dv].astype(F32)
            o_ref[rows, h * dv:(h + 1) * dv] = (on * (g * jax.nn.sigmoid(g))).astype(o_ref.dtype)


def retention_core(proj, cos, sin, gn, B, S):
    H, C = RET_HEADS, RET_CHUNK
    T = proj.shape[0]
    dk = proj.shape[1] // (6 * H)
    dv = 2 * dk
    tb = min(RET_BLOCK, S)
    nb = S // tb
    scale = dk ** -0.5
    log_gamma = jnp.log1p(-jnp.exp2(-5.0 - jnp.arange(H, dtype=F32)))
    idx = jnp.arange(C, dtype=F32)
    diff = idx[:, None] - idx[None, :]
    dmask = jnp.where(diff >= 0, jnp.exp(jnp.maximum(diff, 0.0) * log_gamma[:, None, None]), 0.0) * scale
    qdec = jnp.exp((idx + 1.0) * log_gamma[:, None])[..., None]
    kdec = jnp.exp((C - 1.0 - idx) * log_gamma[:, None])[..., None] * scale
    cdec = jnp.exp(C * log_gamma).reshape(H, 1, 1)
    whole = lambda shape: pl.BlockSpec(shape, lambda b, n: (0,) * len(shape))
    return pl.pallas_call(
        _retention_kernel,
        grid=(B, nb),
        in_specs=[pl.BlockSpec((tb, H * dk), lambda b, n: (b * nb + n, 0)),
                  pl.BlockSpec((tb, H * dk), lambda b, n: (b * nb + n, 1)),
                  pl.BlockSpec((tb, H * dv), lambda b, n: (b * nb + n, 1)),
                  pl.BlockSpec((tb, H * dv), lambda b, n: (b * nb + n, 2)),
                  pl.BlockSpec((tb, dk // 2), lambda b, n: (b * nb + n, 0)),
                  pl.BlockSpec((tb, dk // 2), lambda b, n: (b * nb + n, 0)),
                  whole((H, C, C)), whole((H, C, 1)), whole((H, C, 1)), whole((H, 1, 1)),
                  whole((1, H * dv))],
        out_specs=pl.BlockSpec((tb, H * dv), lambda b, n: (b * nb + n, 0)),
        out_shape=jax.ShapeDtypeStruct((T, H * dv), BF16),
        scratch_shapes=[pltpu.VMEM((H, dk, dv), F32)],
        compiler_params=_params("parallel", "arbitrary"),
        name="retention",
    )(proj, proj, proj, proj, cos, sin, dmask, qdec, kdec, cdec, gn.reshape(1, H * dv))


def _nsa_compress_kernel(kc_ref, vc_ref, pe_ref, w1_ref, w2_ref, kn_ref, cos_ref, sin_ref,
                         kco_ref, vco_ref, xs_scr):
    nh = NSA_CMP_LEN // NSA_CMP_STRIDE
    n_rows = xs_scr.shape[0] // NSA_CMP_STRIDE
    for br, (src, dst) in enumerate(((kc_ref, kco_ref), (vc_ref, vco_ref))):
        xs_scr[...] = src[...].astype(F32)
        acc = [jnp.zeros((n_rows, w1_ref.shape[-1]), F32) for _ in range(nh)]
        for l in range(NSA_CMP_STRIDE):
            piece = xs_scr[pl.ds(l, n_rows, stride=NSA_CMP_STRIDE), :]
            for a in range(nh):
                ll = a * NSA_CMP_STRIDE + l
                acc[a] = acc[a] + _dot((piece + pe_ref[br, ll:ll + 1, :]).astype(BF16), w1_ref[br, ll])
        hid = acc[0] + pltpu.roll(acc[1], n_rows - 1, axis=0)
        z = _dot(_gelu_tanh(hid).astype(BF16), w2_ref[br])
        if br == 0:
            z = _rms(z, kn_ref[0:1, :])
            cos = cos_ref[pl.ds(0, n_rows, stride=NSA_CMP_STRIDE), :]
            sin = sin_ref[pl.ds(0, n_rows, stride=NSA_CMP_STRIDE), :]
            z = z * cos + pltpu.roll(z, NSA_DH // 2, axis=1) * sin
        dst[0, 0] = z.astype(dst.dtype)


def nsa_compress(proj, pe, w1, w2, kn, cosn, sinn, B, S):
    G, dh = NSA_GROUPS, NSA_DH
    nr = S // NSA_CMP_STRIDE
    kc0 = NSA_HEADS
    vc0 = NSA_HEADS + G
    return pl.pallas_call(
        _nsa_compress_kernel,
        grid=(B, G),
        in_specs=[pl.BlockSpec((S, dh), lambda b, g: (b, kc0 + g)),
                  pl.BlockSpec((S, dh), lambda b, g: (b, vc0 + g)),
                  pl.BlockSpec(pe.shape, lambda b, g: (0, 0, 0)),
                  pl.BlockSpec(w1.shape, lambda b, g: (0, 0, 0, 0)),
                  pl.BlockSpec(w2.shape, lambda b, g: (0, 0, 0)),
                  pl.BlockSpec(kn.shape, lambda b, g: (0, 0)),
                  pl.BlockSpec((S, dh), lambda b, g: (b, 0)),
                  pl.BlockSpec((S, dh), lambda b, g: (b, 0))],
        out_specs=[pl.BlockSpec((1, 1, nr, dh), lambda b, g: (b, g, 0, 0))] * 2,
        out_shape=[jax.ShapeDtypeStruct((B, G, nr, dh), BF16)] * 2,
        scratch_shapes=[pltpu.VMEM((S, dh), F32)],
        compiler_params=_params("parallel", "parallel"),
        name="nsa_compress",
    )(proj, proj, pe, w1, w2, kn, cosn, sinn)


def _softmax2_parts(s, bias, hg):
    tq, n = bias.shape
    s3 = s.reshape(hg, tq, n) + bias[None]
    p = jnp.exp2(s3 - jnp.max(s3, axis=-1, keepdims=True))
    return p.reshape(hg * tq, n), jnp.sum(p, axis=-1, keepdims=True).reshape(hg * tq, 1)


def _nsa_attn_kernel(q_ref, gate_ref, cosq_ref, sinq_ref, qn_ref, kc_ref, vc_ref,
                     ks_ref, vs_ref, kw_ref, vw_ref, kn_ref, cosk_ref, sink_ref, mmat_ref,
                     o_ref, ks_scr, kw_scr):
    grp = pl.program_id(1)
    qi = pl.program_id(2)
    tq = q_ref.shape[0]
    dh = NSA_DH
    hg = q_ref.shape[1] // dh
    S = ks_ref.shape[0]
    tk = min(NSA_KV_TILE, S)
    q0 = qi * tq

    @pl.when(qi == 0)
    def _():
        cosk = cosk_ref[...]
        sink = sink_ref[...]
        for src, dst, r in ((ks_ref, ks_scr, 1), (kw_ref, kw_scr, 2)):
            x = _rms(src[...].astype(F32), kn_ref[r:r + 1, :])
            dst[...] = (x * cosk + pltpu.roll(x, dh // 2, axis=1) * sink).astype(BF16)

    cosq = cosq_ref[...]
    sinq = sinq_ref[...]
    qs = []
    for h in range(hg):
        x = _rms(q_ref[:, h * dh:(h + 1) * dh].astype(F32), qn_ref[...])
        x = (x * cosq + pltpu.roll(x, dh // 2, axis=1) * sinq) * (dh ** -0.5 * LOG2_E)
        qs.append(x.astype(BF16))
    qall = jnp.concatenate(qs, axis=0)

    def tok(n):
        return q0 + lax.broadcasted_iota(jnp.int32, (tq, n), 0)

    def key(n):
        return lax.broadcasted_iota(jnp.int32, (tq, n), 1)

    ncp = kc_ref.shape[2]
    valid_c = key(ncp) * NSA_CMP_STRIDE + (NSA_CMP_LEN - 1) <= tok(ncp)
    p_c, l_c = _softmax2_parts(_dot_nt(qall, kc_ref[0, 0]), jnp.where(valid_c, 0.0, MASK_NEG), hg)
    t_row = q0 + jnp.bitwise_and(lax.broadcasted_iota(jnp.int32, (hg * tq, 1), 0), tq - 1)
    p_c = jnp.where(t_row >= NSA_CMP_LEN - 1, p_c / l_c, 0.0)
    o_c = _dot(p_c.astype(BF16), vc_ref[0, 0])
    p_sum = p_c[0:tq]
    for h in range(1, hg):
        p_sum = p_sum + p_c[h * tq:(h + 1) * tq]
    imp = jnp.dot(p_sum, mmat_ref[...], preferred_element_type=F32, precision=lax.Precision.HIGHEST)

    n_sel = S // NSA_SEL_LEN
    sel_rows = ((n_sel + 7) // 8) * 8
    v_imp = imp.T[0:sel_rows, :]
    jb = lax.broadcasted_iota(jnp.int32, (sel_rows, tq), 0)
    cur = (q0 + lax.broadcasted_iota(jnp.int32, (sel_rows, tq), 1)) // NSA_SEL_LEN
    forced = (jb == 0) | (jb == cur) | (jb == cur - 1)
    v_imp = jnp.where(forced, jnp.inf, jnp.where(jb > cur, -jnp.inf, v_imp))
    if sel_rows > n_sel:
        v_imp = jnp.where(jb >= n_sel, -jnp.inf, v_imp)
    rank = jnp.zeros((sel_rows, tq), jnp.int32)
    for i in range(n_sel):
        r = v_imp[i:i + 1, :]
        beats = (r > v_imp) | ((r == v_imp) & (jb > i))
        rank = rank + beats.astype(jnp.int32)
    sel_t = ((rank < min(NSA_TOP_N, n_sel)) & (jb < n_sel)).astype(F32)
    if sel_rows < LANES:
        sel_t = jnp.concatenate([sel_t, jnp.zeros((LANES - sel_rows, tq), F32)], axis=0)
    sel = sel_t.T.astype(BF16)

    row_t = tok(tk)
    lane_k = key(tk)
    e_row = lax.broadcasted_iota(jnp.int32, (LANES, tk), 0)
    e_blk = lax.broadcasted_iota(jnp.int32, (LANES, tk), 1) // NSA_SEL_LEN

    def sel_step(j, carry):
        m, l, acc = carry
        base = pl.multiple_of(j * tk, tk)
        s = _dot_nt(qall, ks_scr[pl.ds(base, tk), :])
        expand = (e_blk + j * (tk // NSA_SEL_LEN) == e_row).astype(BF16)
        chosen = _dot(sel, expand)
        bias = jnp.where((chosen > 0.5) & (lane_k + base <= row_t), 0.0, MASK_NEG)
        s3 = s.reshape(hg, tq, tk) + bias[None]
        m_new = jnp.maximum(m, jnp.max(s3, axis=-1, keepdims=True))
        alpha = jnp.exp2(m - m_new)
        p = jnp.exp2(s3 - m_new)
        l = alpha * l + jnp.sum(p, axis=-1, keepdims=True)
        pv = _dot(p.reshape(hg * tq, tk).astype(BF16), vs_ref[pl.ds(base, tk), :])
        acc = alpha.reshape(hg * tq, 1) * acc + pv
        return m_new, l, acc

    n_kv = (q0 + tq + tk - 1) // tk
    init = (jnp.full((hg, tq, 1), MASK_NEG, F32), jnp.zeros((hg, tq, 1), F32), jnp.zeros((hg * tq, dh), F32))
    _, l_s, acc_s = lax.fori_loop(0, n_kv, sel_step, init)
    o_s = acc_s / l_s.reshape(hg * tq, 1)

    span = min(tq + NSA_WINDOW, S)
    start = pl.multiple_of(jnp.maximum(q0 + tq - span, 0), tq)
    dist = tok(span) - (start + key(span))
    bias_w = jnp.where((dist >= 0) & (dist < NSA_WINDOW), 0.0, MASK_NEG)
    p_w, l_w = _softmax2_parts(_dot_nt(qall, kw_scr[pl.ds(start, span), :]), bias_w, hg)
    o_w = _dot(p_w.astype(BF16), vw_ref[pl.ds(start, span), :]) / l_w

    gsig = jax.nn.sigmoid(gate_ref[...].astype(F32))
    n_br = 3
    for h in range(hg):
        def gate_col(br):
            lo = h * n_br + br
            hi = lo + hg * n_br
            return jnp.where(grp == 0, gsig[:, lo:lo + 1], gsig[:, hi:hi + 1])
        rows = slice(h * tq, (h + 1) * tq)
        o = gate_col(0) * o_c[rows] + gate_col(1) * o_s[rows] + gate_col(2) * o_w[rows]
        o_ref[:, h * dh:(h + 1) * dh] = o.astype(o_ref.dtype)


def nsa_attention(proj, kcc, vcc, qn, kn, cosn, sinn, B, S):
    H, G, dh = NSA_HEADS, NSA_GROUPS, NSA_DH
    hg = H // G
    T = proj.shape[0]
    tq = min(NSA_Q_TILE, S)
    nq = S // tq
    nr = kcc.shape[2]
    n_c = (S - NSA_CMP_LEN) // NSA_CMP_STRIDE + 1
    n_sel = S // NSA_SEL_LEN
    cs = (np.arange(n_c) * NSA_CMP_STRIDE)[:, None]
    js = (np.arange(n_sel) * NSA_SEL_LEN)[None, :]
    overlap = np.clip(np.minimum(cs + NSA_CMP_LEN, js + NSA_SEL_LEN) - np.maximum(cs, js), 0, None) / NSA_CMP_LEN
    mmat = np.zeros((nr, LANES), np.float32)
    mmat[:n_c, :n_sel] = overlap
    col = lambda base: (lambda b, g, i: (b, base + g))
    return pl.pallas_call(
        _nsa_attn_kernel,
        grid=(B, G, nq),
        in_specs=[pl.BlockSpec((tq, hg * dh), lambda b, g, i: (b * nq + i, g)),
                  pl.BlockSpec((tq, LANES), lambda b, g, i: (b * nq + i, H + 6 * G)),
                  pl.BlockSpec((tq, dh), lambda b, g, i: (b * nq + i, 0)),
                  pl.BlockSpec((tq, dh), lambda b, g, i: (b * nq + i, 0)),
                  pl.BlockSpec((1, dh), lambda b, g, i: (0, 0)),
                  pl.BlockSpec((1, 1, nr, dh), lambda b, g, i: (b, g, 0, 0)),
                  pl.BlockSpec((1, 1, nr, dh), lambda b, g, i: (b, g, 0, 0)),
                  pl.BlockSpec((S, dh), col(H + 2 * G)),
                  pl.BlockSpec((S, dh), col(H + 3 * G)),
                  pl.BlockSpec((S, dh), col(H + 4 * G)),
                  pl.BlockSpec((S, dh), col(H + 5 * G)),
                  pl.BlockSpec(kn.shape, lambda b, g, i: (0, 0)),
                  pl.BlockSpec((S, dh), lambda b, g, i: (b, 0)),
                  pl.BlockSpec((S, dh), lambda b, g, i: (b, 0)),
                  pl.BlockSpec(mmat.shape, lambda b, g, i: (0, 0))],
        out_specs=pl.BlockSpec((tq, hg * dh), lambda b, g, i: (b * nq + i, g)),
        out_shape=jax.ShapeDtypeStruct((T, H * dh), BF16),
        scratch_shapes=[pltpu.VMEM((S, dh), BF16), pltpu.VMEM((S, dh), BF16)],
        compiler_params=_params("parallel", "parallel", "arbitrary"),
        name="nsa_attention",
    )(proj, proj, cosn, sinn, qn.reshape(1, dh), kcc, vcc, proj, proj, proj, proj, kn, cosn, sinn,
      jnp.asarray(mmat))


def _s5_scan_kernel(u_ref, bbd_ref, are_ref, aim_ref, cbd_ref, d_ref, y_ref, bu_scr, xs_scr, st_scr):
    nb = st_scr.shape[0] // 2
    w = are_ref.shape[-1]
    tc = u_ref.shape[0] // nb

    @pl.when(pl.program_id(1) == 0)
    def _():
        st_scr[...] = jnp.zeros_like(st_scr)

    u = u_ref[...]
    bu_scr[...] = _dot(u.astype(BF16), bbd_ref[0])
    a_re = jnp.broadcast_to(are_ref[0], (nb, w))
    a_im = jnp.broadcast_to(aim_ref[0], (nb, w))

    def step(t, carry):
        xr, xi = carry
        r0 = pl.multiple_of(t * nb, nb)
        nxr = a_re * xr - a_im * xi + bu_scr[pl.ds(r0, nb), 0:w]
        nxi = a_re * xi + a_im * xr + bu_scr[pl.ds(r0, nb), w:2 * w]
        xs_scr[pl.ds(r0, nb), 0:w] = nxr
        xs_scr[pl.ds(r0, nb), w:2 * w] = nxi
        return nxr, nxi

    xr, xi = lax.fori_loop(0, tc, step, (st_scr[0:nb, :], st_scr[nb:2 * nb, :]), unroll=4)
    st_scr[0:nb, :] = xr
    st_scr[nb:2 * nb, :] = xi
    y = _dot(xs_scr[...].astype(BF16), cbd_ref[0]) + d_ref[...] * u
    y_ref[...] = _gelu_tanh(y).astype(y_ref.dtype)


def s5_scan(u_tm, a_re, a_im, b_re, b_im, c_re, c_im, d_skip, log_dt, B, S):
    D = u_tm.shape[1]
    Cg, P, NS = S5_GROUP, S5_STATE, S5_SET
    G = D // Cg
    K = G // NS
    w = NS * P
    tc = min(S5_TIME_CHUNK, S)
    dt = jnp.exp(log_dt.astype(F32))[:, None]
    mag = jnp.exp(dt * a_re)
    abar_re, abar_im = mag * jnp.cos(dt * a_im), mag * jnp.sin(dt * a_im)
    den = a_re * a_re + a_im * a_im
    nr_, ni_ = abar_re - 1.0, abar_im
    f_re = (nr_ * a_re + ni_ * a_im) / den
    f_im = (ni_ * a_re - nr_ * a_im) / den
    bbar_re = f_re[..., None] * b_re - f_im[..., None] * b_im
    bbar_im = f_re[..., None] * b_im + f_im[..., None] * b_re
    eye = jnp.eye(NS, dtype=F32)

    def in_blockdiag(bb):
        t = bb.reshape(K, NS, P, Cg).transpose(0, 1, 3, 2)
        return jnp.einsum('kgcp,gh->kgchp', t, eye).reshape(K, NS * Cg, NS * P)

    def out_blockdiag(cc):
        t = cc.reshape(K, NS, Cg, P).transpose(0, 1, 3, 2)
        return jnp.einsum('kgpc,gh->kgphc', t, eye).reshape(K, NS * P, NS * Cg)

    bbd = jnp.concatenate([in_blockdiag(bbar_re), in_blockdiag(bbar_im)], axis=-1).astype(BF16)
    cbd = jnp.concatenate([out_blockdiag(c_re), -out_blockdiag(c_im)], axis=1).astype(BF16)
    are = abar_re.reshape(K, 1, w)
    aim = abar_im.reshape(K, 1, w)
    rows = tc * B
    return pl.pallas_call(
        _s5_scan_kernel,
        grid=(K, S // tc),
        in_specs=[pl.BlockSpec((rows, LANES), lambda k, c: (c, k)),
                  pl.BlockSpec((1, LANES, 2 * w), lambda k, c: (k, 0, 0)),
                  pl.BlockSpec((1, 1, w), lambda k, c: (k, 0, 0)),
                  pl.BlockSpec((1, 1, w), lambda k, c: (k, 0, 0)),
                  pl.BlockSpec((1, 2 * w, LANES), lambda k, c: (k, 0, 0)),
                  pl.BlockSpec((1, LANES), lambda k, c: (0, k))],
        out_specs=pl.BlockSpec((rows, LANES), lambda k, c: (c, k)),
        out_shape=jax.ShapeDtypeStruct((S * B, D), BF16),
        scratch_shapes=[pltpu.VMEM((rows, 2 * w), F32), pltpu.VMEM((rows, 2 * w), F32),
                        pltpu.VMEM((2 * B, w), F32)],
        compiler_params=_params("parallel", "arbitrary"),
        name="s5_scan",
    )(u_tm, bbd, are, aim, cbd, d_skip.reshape(1, D))


def _glu_res_kernel(y_ref, wa_ref, wb_ref, r_ref, o_ref):
    y = y_ref[...]
    a = _dot(y, wa_ref[...])
    b = _dot(y, wb_ref[...])
    o_ref[...] = r_ref[...] + a * jax.nn.sigmoid(b)


def glu_res(y_tm, w, res, B, S, *, tm):
    D = w.shape[0]
    N = w.shape[1] // 2
    ns = S // tm
    return pl.pallas_call(
        _glu_res_kernel,
        grid=(B * ns,),
        in_specs=[pl.BlockSpec((tm, D), lambda i: (i % ns, i // ns)),
                  pl.BlockSpec((D, N), lambda i: (0, 0)),
                  pl.BlockSpec((D, N), lambda i: (0, 1)),
                  pl.BlockSpec((tm, N), lambda i: (i, 0))],
        out_specs=pl.BlockSpec((tm, N), lambda i: (i, 0)),
        out_shape=jax.ShapeDtypeStruct((B * S, N), F32),
        compiler_params=_params("parallel"),
        name="glu_res",
    )(y_tm, w, w, res)


def _ple(h, g_ref, p_ref, wgate_ref, wproj_ref):
    gate = jax.nn.sigmoid(_dot(_rms(h, g_ref[...]).astype(BF16), wgate_ref[...]))
    return h + gate * _dot(p_ref[...].astype(BF16), wproj_ref[...])


def _ffn_kernel(x_ref, g_ref, wg_ref, wu_ref, wd_ref, gp_ref, p_ref, wgate_ref, wproj_ref, o_ref,
                xn_scr, acc_scr):
    f = pl.program_id(1)

    @pl.when(f == 0)
    def _():
        xn_scr[...] = _rms(x_ref[...], g_ref[...]).astype(BF16)
        acc_scr[...] = jnp.zeros_like(acc_scr)

    xn = xn_scr[...]
    gg = _dot(xn, wg_ref[...])
    uu = _dot(xn, wu_ref[...])
    acc_scr[...] += _dot((gg * jax.nn.sigmoid(gg) * uu).astype(BF16), wd_ref[...])

    @pl.when(f == pl.num_programs(1) - 1)
    def _():
        o_ref[...] = _ple(x_ref[...] + acc_scr[...], gp_ref, p_ref, wgate_ref, wproj_ref)


def swiglu_ffn_ple(x, g, w_gate_up, w_down, g_ple, p_all, layer, w_pgate, w_pproj, *, tm, tf):
    T, D = x.shape
    F = w_down.shape[0]
    PD = p_all.shape[2]
    nf = F // tf
    return pl.pallas_call(
        _ffn_kernel,
        grid=(T // tm, nf),
        in_specs=[pl.BlockSpec((tm, D), lambda i, f: (i, 0)),
                  pl.BlockSpec((1, D), lambda i, f: (0, 0)),
                  pl.BlockSpec((D, tf), lambda i, f: (0, f)),
                  pl.BlockSpec((D, tf), lambda i, f: (0, nf + f)),
                  pl.BlockSpec((tf, D), lambda i, f: (f, 0)),
                  pl.BlockSpec((1, D), lambda i, f: (0, 0)),
                  pl.BlockSpec((None, tm, PD), lambda i, f: (layer, i, 0)),
                  pl.BlockSpec((D, D), lambda i, f: (0, 0)),
                  pl.BlockSpec((PD, D), lambda i, f: (0, 0))],
        out_specs=pl.BlockSpec((tm, D), lambda i, f: (i, 0)),
        out_shape=jax.ShapeDtypeStruct((T, D), F32),
        scratch_shapes=[pltpu.VMEM((tm, D), BF16), pltpu.VMEM((tm, D), F32)],
        compiler_params=_params("parallel", "arbitrary"),
        name="swiglu_ffn_ple",
    )(x, g.reshape(1, D), w_gate_up, w_gate_up, w_down, g_ple.reshape(1, D), p_all, w_pgate, w_pproj)


META_E0, META_E1, META_W0, META_W1, META_R0, META_R1 = range(6)


def _router_kernel(x_ref, g_ref, w_ref, meta_ref, cnt_ref, carry_scr):
    @pl.when(pl.program_id(0) == 0)
    def _():
        carry_scr[...] = jnp.zeros_like(carry_scr)

    xn = _rms(x_ref[...], g_ref[...])
    logits = jnp.dot(xn, w_ref[...], preferred_element_type=F32, precision=lax.Precision.HIGHEST)
    tm = logits.shape[0]
    lane = lax.broadcasted_iota(jnp.int32, logits.shape, 1).astype(F32)
    logits = jnp.where(lane < N_EXPERTS, logits, -jnp.inf)

    def top1(v):
        m = jnp.max(v, axis=-1, keepdims=True)
        idx = jnp.min(jnp.where(v == m, lane, float(LANES)), axis=-1, keepdims=True)
        return m, idx, lane == idx

    m1, e0, hot1 = top1(logits)
    m2, e1, hot2 = top1(jnp.where(hot1, -jnp.inf, logits))
    ex = jnp.exp(m2 - m1)
    w0 = 1.0 / (1.0 + ex)
    w1 = ex * w0

    both = jnp.where(hot1 | hot2, 1.0, 0.0)
    tri = (lax.broadcasted_iota(jnp.int32, (tm, tm), 0) > lax.broadcasted_iota(jnp.int32, (tm, tm), 1))
    before = _dot(tri.astype(BF16), both.astype(BF16)) + carry_scr[0:1, :]
    r0 = jnp.sum(jnp.where(hot1, before, 0.0), axis=-1, keepdims=True)
    r1 = jnp.sum(jnp.where(hot2, before, 0.0), axis=-1, keepdims=True)
    carry_scr[...] = carry_scr[...] + jnp.sum(both, axis=0, keepdims=True)
    cnt_ref[...] = carry_scr[...]

    meta = jnp.zeros_like(logits)
    for k, v in ((META_E0, e0), (META_E1, e1), (META_W0, w0), (META_W1, w1), (META_R0, r0), (META_R1, r1)):
        meta = jnp.where(lane == float(k), v, meta)
    meta_ref[...] = meta


def router_topk(x, g, w_router, *, tm):
    T, D = x.shape
    w = jnp.zeros((D, LANES), F32).at[:, :N_EXPERTS].set(w_router)
    return pl.pallas_call(
        _router_kernel,
        grid=(T // tm,),
        in_specs=[pl.BlockSpec((tm, D), lambda i: (i, 0)),
                  pl.BlockSpec((1, D), lambda i: (0, 0)),
                  pl.BlockSpec((D, LANES), lambda i: (0, 0))],
        out_specs=[pl.BlockSpec((tm, LANES), lambda i: (i, 0)),
                   pl.BlockSpec((8, LANES), lambda i: (0, 0))],
        out_shape=[jax.ShapeDtypeStruct((T, LANES), F32), jax.ShapeDtypeStruct((8, LANES), F32)],
        scratch_shapes=[pltpu.VMEM((8, LANES), F32)],
        compiler_params=_params("arbitrary"),
        name="router",
    )(x, g.reshape(1, D), w)


def _start_row_gathers(idx_ref, src_hbm, dst_ref, sem, lo, hi):
    for r in range(lo, hi):
        pltpu.make_async_copy(src_hbm.at[pl.ds(idx_ref[0, 0, r], 1), :], dst_ref.at[pl.ds(r, 1), :], sem).start()


def _wait_row_gathers(src_hbm, dst_ref, sem):
    pltpu.make_async_copy(src_hbm.at[pl.ds(0, dst_ref.shape[0]), :], dst_ref, sem).wait()


def _moe_expert_kernel(te_ref, nu_ref, tok_ref, tok_next_ref, x_hbm, g_ref, wg_ref, wu_ref, wd_ref, y_ref,
                       xg_scr, xn_scr, acc_scr, sems):
    i = pl.program_id(0)
    f = pl.program_id(1)
    n_tiles = pl.num_programs(0)
    nf = pl.num_programs(1)
    n_used = nu_ref[0]
    active = i < n_used
    tm = xn_scr.shape[0]
    cur = xg_scr.at[i % 2]
    nxt = xg_scr.at[(i + 1) % 2]
    sem_cur = sems.at[i % 2]
    sem_nxt = sems.at[(i + 1) % 2]

    @pl.when((i == 0) & (f == 0))
    def _():
        def issue(r, carry):
            pltpu.make_async_copy(x_hbm.at[pl.ds(tok_ref[0, 0, r], 1), :], cur.at[pl.ds(r, 1), :], sem_cur).start()
            return carry
        lax.fori_loop(0, tm, issue, 0, unroll=8)

    @pl.when((f == 0) & (i <= n_used))
    def _():
        _wait_row_gathers(x_hbm, cur, sem_cur)

    @pl.when(active & (f == 0))
    def _():
        xn_scr[...] = _rms(cur[...], g_ref[...]).astype(BF16)
        acc_scr[...] = jnp.zeros_like(acc_scr)

    def step(lo, hi):
        xn = xn_scr[...]
        gg = _dot(xn, wg_ref[0])
        uu = _dot(xn, wu_ref[0])
        acc_scr[...] += _dot((gg * jax.nn.sigmoid(gg) * uu).astype(BF16), wd_ref[0])
        _start_row_gathers(tok_next_ref, x_hbm, nxt, sem_nxt, lo, hi)

    per = -(-tm // MOE_NF)
    for fs in range(MOE_NF):
        @pl.when(active & (f == fs))
        def _(fs=fs):
            step(min(fs * per, tm), min((fs + 1) * per, tm))

    @pl.when(f == nf - 1)
    def _():
        y_ref[...] = jnp.where(active, acc_scr[...], 0.0)

    @pl.when(active & (i == n_tiles - 1) & (f == nf - 1))
    def _():
        _wait_row_gathers(x_hbm, nxt, sem_nxt)


def moe_expert_ffn(x, g, tok_of_slot, tile_expert, n_used, w_gate_up, w_down, layer, *, tm):
    T, D = x.shape
    _, E, F, _ = w_down.shape
    nf = MOE_NF
    tf = F // nf
    n_tiles = tok_of_slot.shape[0]

    def fblk(i, f, nu):
        return jnp.where(i < nu[0], f, nf - 1)

    tok_spec = lambda off: pl.BlockSpec((1, 1, tm), lambda i, f, te, nu: (jnp.minimum(i + off, n_tiles - 1), 0, 0),
                                        memory_space=pltpu.SMEM)
    grid_spec = pltpu.PrefetchScalarGridSpec(
        num_scalar_prefetch=2,
        grid=(n_tiles, nf),
        in_specs=[tok_spec(0), tok_spec(1),
                  pl.BlockSpec(memory_space=pl.ANY),
                  pl.BlockSpec((1, D), lambda i, f, te, nu: (0, 0)),
                  pl.BlockSpec((None, 1, D, tf), lambda i, f, te, nu: (layer, te[i], 0, fblk(i, f, nu))),
                  pl.BlockSpec((None, 1, D, tf), lambda i, f, te, nu: (layer, te[i], 0, nf + fblk(i, f, nu))),
                  pl.BlockSpec((None, 1, tf, D), lambda i, f, te, nu: (layer, te[i], fblk(i, f, nu), 0))],
        out_specs=pl.BlockSpec((tm, D), lambda i, f, te, nu: (i, 0)),
        scratch_shapes=[pltpu.VMEM((2, tm, D), F32), pltpu.VMEM((tm, D), BF16), pltpu.VMEM((tm, D), F32),
                        pltpu.SemaphoreType.DMA((2,))],
    )
    return pl.pallas_call(
        _moe_expert_kernel,
        grid_spec=grid_spec,
        out_shape=jax.ShapeDtypeStruct((n_tiles * tm, D), F32),
        compiler_params=_params("arbitrary", "arbitrary"),
        name="moe_experts",
    )(tile_expert, n_used, tok_of_slot, tok_of_slot, x, g.reshape(1, D), w_gate_up, w_gate_up, w_down)


def _moe_combine_kernel(s0_ref, s1_ref, s0n_ref, s1n_ref, x_ref, meta_ref, y_hbm, gp_ref, p_ref, wgate_ref,
                        wproj_ref, o_ref, y_scr, sems):
    i = pl.program_id(0)
    n = y_scr.shape[2]
    cur, nxt = i % 2, (i + 1) % 2

    @pl.when(i == 0)
    def _():
        _start_row_gathers(s0_ref, y_hbm, y_scr.at[0, 0], sems.at[0, 0], 0, n)
        _start_row_gathers(s1_ref, y_hbm, y_scr.at[0, 1], sems.at[0, 1], 0, n)

    _wait_row_gathers(y_hbm, y_scr.at[cur, 0], sems.at[cur, 0])
    _wait_row_gathers(y_hbm, y_scr.at[cur, 1], sems.at[cur, 1])
    _start_row_gathers(s0n_ref, y_hbm, y_scr.at[nxt, 0], sems.at[nxt, 0], 0, n)
    _start_row_gathers(s1n_ref, y_hbm, y_scr.at[nxt, 1], sems.at[nxt, 1], 0, n)
    meta = meta_ref[...]
    w0 = meta[:, META_W0:META_W0 + 1]
    w1 = meta[:, META_W1:META_W1 + 1]
    h = x_ref[...] + w0 * y_scr[cur, 0] + w1 * y_scr[cur, 1]
    o_ref[...] = _ple(h, gp_ref, p_ref, wgate_ref, wproj_ref)

    @pl.when(i == pl.num_programs(0) - 1)
    def _():
        _wait_row_gathers(y_hbm, y_scr.at[nxt, 0], sems.at[nxt, 0])
        _wait_row_gathers(y_hbm, y_scr.at[nxt, 1], sems.at[nxt, 1])


def moe_combine_ple(x, meta, slot0, slot1, y, g_ple, p_all, layer, w_pgate, w_pproj, *, tm):
    T, D = x.shape
    PD = p_all.shape[2]
    nt = T // tm
    idx = lambda off: pl.BlockSpec((1, 1, tm), lambda i: (jnp.minimum(i + off, nt - 1), 0, 0),
                                   memory_space=pltpu.SMEM)
    return pl.pallas_call(
        _moe_combine_kernel,
        grid=(nt,),
        in_specs=[idx(0), idx(0), idx(1), idx(1),
                  pl.BlockSpec((tm, D), lambda i: (i, 0)),
                  pl.BlockSpec((tm, LANES), lambda i: (i, 0)),
                  pl.BlockSpec(memory_space=pl.ANY),
                  pl.BlockSpec((1, D), lambda i: (0, 0)),
                  pl.BlockSpec((None, tm, PD), lambda i: (layer, i, 0)),
                  pl.BlockSpec((D, D), lambda i: (0, 0)),
                  pl.BlockSpec((PD, D), lambda i: (0, 0))],
        out_specs=pl.BlockSpec((tm, D), lambda i: (i, 0)),
        out_shape=jax.ShapeDtypeStruct((T, D), F32),
        scratch_shapes=[pltpu.VMEM((2, 2, tm, D), F32), pltpu.SemaphoreType.DMA((2, 2))],
        compiler_params=_params("arbitrary"),
        name="moe_combine_ple",
    )(slot0, slot1, slot0, slot1, x, meta, y, g_ple.reshape(1, D), p_all, w_pgate, w_pproj)


def moe_layer(x, g, w_router, w_gate_up, w_down, layer, ple, *, tm_route, tm_expert, tm_combine):
    T, D = x.shape
    E = w_down.shape[1]
    n_tiles = (2 * T) // tm_expert + E
    meta, cnt = router_topk(x, g, w_router, tm=tm_route)
    e0 = meta[:, META_E0].astype(jnp.int32)
    e1 = meta[:, META_E1].astype(jnp.int32)
    r0 = meta[:, META_R0].astype(jnp.int32)
    r1 = meta[:, META_R1].astype(jnp.int32)
    counts = cnt[0, :E].astype(jnp.int32)
    tiles_per = (counts + tm_expert - 1) // tm_expert
    tile_end = jnp.cumsum(tiles_per)
    group_start = (tile_end - tiles_per) * tm_expert
    slot0 = group_start[e0] + r0
    slot1 = group_start[e1] + r1
    n_used = tile_end[-1:]
    tile_ids = jnp.arange(n_tiles, dtype=jnp.int32)
    tile_expert = jnp.searchsorted(tile_end, jnp.minimum(tile_ids, n_used[0] - 1), side="right").astype(jnp.int32)
    tile_expert = jnp.minimum(tile_expert, E - 1)
    tok = jnp.arange(T, dtype=jnp.int32)
    tok_of_slot = jnp.zeros((n_tiles * tm_expert,), jnp.int32).at[slot0].set(tok).at[slot1].set(tok)
    y = moe_expert_ffn(x, g, tok_of_slot.reshape(n_tiles, 1, tm_expert), tile_expert, n_used.astype(jnp.int32),
                       w_gate_up, w_down, layer, tm=tm_expert)
    return moe_combine_ple(x, meta, slot0.reshape(T // tm_combine, 1, tm_combine),
                           slot1.reshape(T // tm_combine, 1, tm_combine), y, *ple, tm=tm_combine)


def _row_tile(S, want):
    return min(want, S)


def retention_layer(h, tabs, B, S, g_norm, w_in, gn, w_out):
    tm = _row_tile(S, 1024)
    proj = norm_matmul(h, g_norm, w_in.astype(BF16), tm=tm, tn=min(2048, w_in.shape[1]), out_dtype=BF16)
    o = retention_core(proj, tabs["ret_cos"], tabs["ret_sin"], gn, B, S)
    return matmul_res(o, w_out.astype(BF16), h, tm=_row_tile(S, 512))


def nsa_layer(h, tabs, B, S, g_norm, w_in, q_norm, k_norm, cmp_pos, cmp_w1, cmp_w2, w_out):
    D, n_in = w_in.shape
    n_pad = -(-n_in // (7 * LANES)) * (7 * LANES)
    w_in_p = jnp.zeros((D, n_pad), BF16).at[:, :n_in].set(w_in.astype(BF16))
    proj = norm_matmul(h, g_norm, w_in_p, tm=_row_tile(S, 1024), tn=7 * LANES, out_dtype=BF16)
    kn = jnp.zeros((8, NSA_DH), F32).at[:3].set(k_norm)
    w1 = cmp_w1.astype(BF16).reshape(2, NSA_CMP_LEN, NSA_DH, cmp_w1.shape[-1])
    kcc, vcc = nsa_compress(proj, cmp_pos, w1, cmp_w2.astype(BF16), kn, tabs["nsa_cos"], tabs["nsa_sin"], B, S)
    o = nsa_attention(proj, kcc, vcc, q_norm, kn, tabs["nsa_cos"], tabs["nsa_sin"], B, S)
    return matmul_res(o, w_out.astype(BF16), h, tm=_row_tile(S, 512))


def s5_layer(h, B, S, g_norm, w_in, a_re, a_im, b_re, b_im, c_re, c_im, d_skip, log_dt, w_glu):
    D = h.shape[1]
    tm = _row_tile(S, 1024)
    u = norm_matmul(h, g_norm, w_in.astype(BF16), tm=tm, tn=D, out_dtype=F32, time_major_batch=(B, S))
    y = s5_scan(u.reshape(S * B, D), a_re, a_im, b_re, b_im, c_re, c_im, d_skip, log_dt, B, S)
    return glu_res(y.reshape(S, B * D), w_glu.astype(BF16), h, B, S, tm=_row_tile(S, 512))


def kernel(x, p, positions, norm_mix, norm_ffn, norm_ple, ret_w_in, ret_gn, ret_w_out, nsa_w_in, nsa_q_norm, nsa_k_norm, nsa_cmp_pos, nsa_cmp_w1, nsa_cmp_w2, nsa_w_out, s5_w_in, s5_a_re, s5_a_im, s5_b_re, s5_b_im, s5_c_re, s5_c_im, s5_d, s5_log_dt, s5_w_glu, ffn_w_gate_up, ffn_w_down, moe_router, moe_w_gate_up, moe_w_down, ple_w_proj, ple_w_gate):
    B, S, D = x.shape
    depth = p.shape[0]
    T = B * S
    h = x.reshape(T, D)
    pos_col = positions.reshape(T, 1)
    p_all = p.reshape(depth, T, p.shape[-1])
    moe_gu = moe_w_gate_up.astype(BF16)
    moe_dn = moe_w_down.astype(BF16)

    ones = jnp.ones((1, LANES), F32)
    dk = D // RET_HEADS
    inv_ret = (ROPE_THETA ** (-jnp.arange(0, dk, 2, dtype=F32) / dk)).reshape(1, LANES)
    inv_half = ROPE_THETA ** (-jnp.arange(0, NSA_DH, 2, dtype=F32) / NSA_DH)
    inv_nsa = jnp.concatenate([inv_half, inv_half]).reshape(1, LANES)
    sign_nsa = jnp.concatenate([-jnp.ones((NSA_DH // 2,), F32), jnp.ones((NSA_DH // 2,), F32)]).reshape(1, LANES)
    tabs = {}
    tabs["ret_cos"], tabs["ret_sin"] = rope_tables(pos_col, inv_ret, ones, _row_tile(T, 1024))
    if depth > 1:
        tabs["nsa_cos"], tabs["nsa_sin"] = rope_tables(pos_col, inv_nsa, sign_nsa, _row_tile(T, 1024))

    for i in range(depth):
        m, j = i % 3, i // 3
        if m == 0:
            h = retention_layer(h, tabs, B, S, norm_mix[i], ret_w_in[j], ret_gn[j], ret_w_out[j])
        elif m == 1:
            h = nsa_layer(h, tabs, B, S, norm_mix[i], nsa_w_in[j], nsa_q_norm[j], nsa_k_norm[j],
                          nsa_cmp_pos[j], nsa_cmp_w1[j], nsa_cmp_w2[j], nsa_w_out[j])
        else:
            h = s5_layer(h, B, S, norm_mix[i], s5_w_in[j], s5_a_re[j], s5_a_im[j], s5_b_re[j], s5_b_im[j],
                         s5_c_re[j], s5_c_im[j], s5_d[j], s5_log_dt[j], s5_w_glu[j])
        tm_ffn = _row_tile(T, 1024)
        ple = (norm_ple[i], p_all, i, ple_w_gate[i].astype(BF16), ple_w_proj[i].astype(BF16))
        if i % 2 == 0:
            h = swiglu_ffn_ple(h, norm_ffn[i], ffn_w_gate_up[i // 2].astype(BF16),
                               ffn_w_down[i // 2].astype(BF16), *ple, tm=tm_ffn, tf=256)
        else:
            h = moe_layer(h, norm_ffn[i], moe_router[i // 2], moe_gu, moe_dn, i // 2, ple,
                          tm_route=tm_ffn, tm_expert=tm_ffn, tm_combine=_row_tile(T, 512))
    return h.reshape(B, S, D)
```

```python
import functools
import math

import numpy as np
import jax
import jax.numpy as jnp
from jax import lax
from jax.experimental import pallas as pl
from jax.experimental.pallas import tpu as pltpu

F32 = jnp.float32
BF16 = jnp.bfloat16

NORM_EPS = 1e-6
ROPE_THETA = 10000.0

RET_HEADS = 4
RET_CHUNK = 128
RET_BLOCK = 512

NSA_HEADS = 8
NSA_GROUPS = 2
NSA_DH = 128
NSA_CMP_LEN = 32
NSA_CMP_STRIDE = 16
NSA_SEL_LEN = 64
NSA_TOP_N = 16
NSA_WINDOW = 512
NSA_Q_TILE = 128
NSA_KV_TILE = 512

S5_GROUP = 16
S5_STATE = 64
S5_SET = 8
S5_TIME_CHUNK = 64

N_EXPERTS = 8
MOE_NF = 7
LANES = 128
MASK_NEG = -1e30
LOG2_E = 1.4426950408889634

VMEM_LIMIT = 56 * 1024 * 1024


def _params(*sem):
    return pltpu.CompilerParams(dimension_semantics=sem, vmem_limit_bytes=VMEM_LIMIT)


def _dot(a, b):
    return jnp.dot(a, b, preferred_element_type=F32)


def _dot_nt(a, b):
    return lax.dot_general(a, b, (((1,), (1,)), ((), ())), preferred_element_type=F32)


def _rms(x, g):
    return x * lax.rsqrt(jnp.mean(x * x, axis=-1, keepdims=True) + NORM_EPS) * g


def _gelu_tanh(x):
    return 0.5 * x * (1.0 + jnp.tanh(math.sqrt(2.0 / math.pi) * (x + 0.044715 * (x * x * x))))


def _rope_table_kernel(pos_ref, inv_ref, sign_ref, cos_ref, sin_ref):
    ang = pos_ref[...].astype(F32) * inv_ref[...]
    cos_ref[...] = jnp.cos(ang)
    sin_ref[...] = jnp.sin(ang) * sign_ref[...]


def rope_tables(pos_col, inv, sign, tm):
    T = pos_col.shape[0]
    return pl.pallas_call(
        _rope_table_kernel,
        grid=(T // tm,),
        in_specs=[pl.BlockSpec((tm, 1), lambda i: (i, 0)),
                  pl.BlockSpec((1, LANES), lambda i: (0, 0)),
                  pl.BlockSpec((1, LANES), lambda i: (0, 0))],
        out_specs=[pl.BlockSpec((tm, LANES), lambda i: (i, 0))] * 2,
        out_shape=[jax.ShapeDtypeStruct((T, LANES), F32)] * 2,
        compiler_params=_params("parallel"),
        name="rope_tables",
    )(pos_col, inv, sign)


def _norm_matmul_kernel(x_ref, g_ref, w_ref, o_ref, xn_ref):
    @pl.when(pl.program_id(1) == 0)
    def _():
        xn_ref[...] = _rms(x_ref[...], g_ref[...]).astype(BF16)

    o_ref[...] = _dot(xn_ref[...], w_ref[...]).astype(o_ref.dtype)


def norm_matmul(x, g, w, *, tm, tn, out_dtype, time_major_batch=None):
    T, D = x.shape
    N = w.shape[1]
    nj = N // tn
    if time_major_batch is None:
        out_shape = jax.ShapeDtypeStruct((T, N), out_dtype)
        out_spec = pl.BlockSpec((tm, tn), lambda i, j: (i, j))
    else:
        B, S = time_major_batch
        ns = S // tm
        out_shape = jax.ShapeDtypeStruct((S, B * N), out_dtype)
        out_spec = pl.BlockSpec((tm, tn), lambda i, j: (i % ns, (i // ns) * nj + j))
    return pl.pallas_call(
        _norm_matmul_kernel,
        grid=(T // tm, nj),
        in_specs=[pl.BlockSpec((tm, D), lambda i, j: (i, 0)),
                  pl.BlockSpec((1, D), lambda i, j: (0, 0)),
                  pl.BlockSpec((D, tn), lambda i, j: (0, j))],
        out_specs=out_spec,
        out_shape=out_shape,
        scratch_shapes=[pltpu.VMEM((tm, D), BF16)],
        compiler_params=_params("parallel", "arbitrary"),
        name="norm_matmul",
    )(x, g.reshape(1, D), w)


def _matmul_res_kernel(x_ref, w_ref, r_ref, o_ref):
    o_ref[...] = r_ref[...] + _dot(x_ref[...], w_ref[...])


def matmul_res(x, w, res, *, tm):
    T, K = x.shape
    N = w.shape[1]
    return pl.pallas_call(
        _matmul_res_kernel,
        grid=(T // tm,),
        in_specs=[pl.BlockSpec((tm, K), lambda i: (i, 0)),
                  pl.BlockSpec((K, N), lambda i: (0, 0)),
                  pl.BlockSpec((tm, N), lambda i: (i, 0))],
        out_specs=pl.BlockSpec((tm, N), lambda i: (i, 0)),
        out_shape=jax.ShapeDtypeStruct((T, N), F32),
        compiler_params=_params("parallel"),
        name="matmul_res",
    )(x, w, res)


def _retention_kernel(q_ref, k_ref, v_ref, g_ref, cos_ref, sin_ref, dmask_ref, qdec_ref, kdec_ref,
                      cdec_ref, gn_ref, o_ref, r_scr):
    H, dk, dv = r_scr.shape
    C = dmask_ref.shape[1]
    half = dk // 2

    @pl.when(pl.program_id(1) == 0)
    def _():
        r_scr[...] = jnp.zeros_like(r_scr)

    for c in range(q_ref.shape[0] // C):
        rows = slice(c * C, (c + 1) * C)
        cos = cos_ref[rows, :]
        sin = sin_ref[rows, :]

        def rope(x):
            x1, x2 = x[:, :half], x[:, half:]
            return jnp.concatenate([x1 * cos - x2 * sin, x2 * cos + x1 * sin], axis=-1)

        for h in range(H):
            q = rope(q_ref[rows, h * dk:(h + 1) * dk].astype(F32))
            k = rope(k_ref[rows, h * dk:(h + 1) * dk].astype(F32))
            v = v_ref[rows, h * dv:(h + 1) * dv]
            qb = q.astype(BF16)
            scores = _dot_nt(qb, k.astype(BF16)) * dmask_ref[h]
            intra = _dot(scores.astype(BF16), v)
            r_old = r_scr[h]
            cross = _dot(qb, r_old.astype(BF16)) * qdec_ref[h]
            kd_t = (k * kdec_ref[h]).T.astype(BF16)
            r_scr[h] = r_old * cdec_ref[h] + _dot(kd_t, v)

            o = intra + cross
            mu = jnp.mean(o, axis=-1, keepdims=True)
            oc = o - mu
            var = jnp.mean(oc * oc, axis=-1, keepdims=True)
            on = oc * lax.rsqrt(var + NORM_EPS) * gn_ref[:, h * dv:(h + 1) * dv]
            g = g_ref[rows, h * dv:(h + 1) * dv].astype(F32)
            o_ref[rows, h * dv:(h + 1) * dv] = (on * (g * jax.nn.sigmoid(g))).astype(o_ref.dtype)


def retention_core(proj, cos, sin, gn, B, S):
    H, C = RET_HEADS, RET_CHUNK
    T = proj.shape[0]
    dk = proj.shape[1] // (6 * H)
    dv = 2 * dk
    tb = min(RET_BLOCK, S)
    nb = S // tb
    scale = dk ** -0.5
    log_gamma = jnp.log1p(-jnp.exp2(-5.0 - jnp.arange(H, dtype=F32)))
    idx = jnp.arange(C, dtype=F32)
    diff = idx[:, None] - idx[None, :]
    dmask = jnp.where(diff >= 0, jnp.exp(jnp.maximum(diff, 0.0) * log_gamma[:, None, None]), 0.0) * scale
    qdec = jnp.exp((idx + 1.0) * log_gamma[:, None])[..., None]
    kdec = jnp.exp((C - 1.0 - idx) * log_gamma[:, None])[..., None] * scale
    cdec = jnp.exp(C * log_gamma).reshape(H, 1, 1)
    whole = lambda shape: pl.BlockSpec(shape, lambda b, n: (0,) * len(shape))
    return pl.pallas_call(
        _retention_kernel,
        grid=(B, nb),
        in_specs=[pl.BlockSpec((tb, H * dk), lambda b, n: (b * nb + n, 0)),
                  pl.BlockSpec((tb, H * dk), lambda b, n: (b * nb + n, 1)),
                  pl.BlockSpec((tb, H * dv), lambda b, n: (b * nb + n, 1)),
                  pl.BlockSpec((tb, H * dv), lambda b, n: (b * nb + n, 2)),
                  pl.BlockSpec((tb, dk // 2), lambda b, n: (b * nb + n, 0)),
                  pl.BlockSpec((tb, dk // 2), lambda b, n: (b * nb + n, 0)),
                  whole((H, C, C)), whole((H, C, 1)), whole((H, C, 1)), whole((H, 1, 1)),
                  whole((1, H * dv))],
        out_specs=pl.BlockSpec((tb, H * dv), lambda b, n: (b * nb + n, 0)),
        out_shape=jax.ShapeDtypeStruct((T, H * dv), BF16),
        scratch_shapes=[pltpu.VMEM((H, dk, dv), F32)],
        compiler_params=_params("parallel", "arbitrary"),
        name="retention",
    )(proj, proj, proj, proj, cos, sin, dmask, qdec, kdec, cdec, gn.reshape(1, H * dv))


def _nsa_compress_kernel(kc_ref, vc_ref, pe_ref, w1_ref, w2_ref, kn_ref, cos_ref, sin_ref,
                         kco_ref, vco_ref, xs_scr):
    nh = NSA_CMP_LEN // NSA_CMP_STRIDE
    n_rows = xs_scr.shape[0] // NSA_CMP_STRIDE
    for br, (src, dst) in enumerate(((kc_ref, kco_ref), (vc_ref, vco_ref))):
        xs_scr[...] = src[...].astype(F32)
        acc = [jnp.zeros((n_rows, w1_ref.shape[-1]), F32) for _ in range(nh)]
        for l in range(NSA_CMP_STRIDE):
            piece = xs_scr[pl.ds(l, n_rows, stride=NSA_CMP_STRIDE), :]
            for a in range(nh):
                ll = a * NSA_CMP_STRIDE + l
                acc[a] = acc[a] + _dot((piece + pe_ref[br, ll:ll + 1, :]).astype(BF16), w1_ref[br, ll])
        hid = acc[0] + pltpu.roll(acc[1], n_rows - 1, axis=0)
        z = _dot(_gelu_tanh(hid).astype(BF16), w2_ref[br])
        if br == 0:
            z = _rms(z, kn_ref[0:1, :])
            cos = cos_ref[pl.ds(0, n_rows, stride=NSA_CMP_STRIDE), :]
            sin = sin_ref[pl.ds(0, n_rows, stride=NSA_CMP_STRIDE), :]
            z = z * cos + pltpu.roll(z, NSA_DH // 2, axis=1) * sin
        dst[0, 0] = z.astype(dst.dtype)


def nsa_compress(proj, pe, w1, w2, kn, cosn, sinn, B, S):
    G, dh = NSA_GROUPS, NSA_DH
    nr = S // NSA_CMP_STRIDE
    kc0 = NSA_HEADS
    vc0 = NSA_HEADS + G
    return pl.pallas_call(
        _nsa_compress_kernel,
        grid=(B, G),
        in_specs=[pl.BlockSpec((S, dh), lambda b, g: (b, kc0 + g)),
                  pl.BlockSpec((S, dh), lambda b, g: (b, vc0 + g)),
                  pl.BlockSpec(pe.shape, lambda b, g: (0, 0, 0)),
                  pl.BlockSpec(w1.shape, lambda b, g: (0, 0, 0, 0)),
                  pl.BlockSpec(w2.shape, lambda b, g: (0, 0, 0)),
                  pl.BlockSpec(kn.shape, lambda b, g: (0, 0)),
                  pl.BlockSpec((S, dh), lambda b, g: (b, 0)),
                  pl.BlockSpec((S, dh), lambda b, g: (b, 0))],
        out_specs=[pl.BlockSpec((1, 1, nr, dh), lambda b, g: (b, g, 0, 0))] * 2,
        out_shape=[jax.ShapeDtypeStruct((B, G, nr, dh), BF16)] * 2,
        scratch_shapes=[pltpu.VMEM((S, dh), F32)],
        compiler_params=_params("parallel", "parallel"),
        name="nsa_compress",
    )(proj, proj, pe, w1, w2, kn, cosn, sinn)


def _softmax2_parts(s, bias, hg):
    tq, n = bias.shape
    s3 = s.reshape(hg, tq, n) + bias[None]
    p = jnp.exp2(s3 - jnp.max(s3, axis=-1, keepdims=True))
    return p.reshape(hg * tq, n), jnp.sum(p, axis=-1, keepdims=True).reshape(hg * tq, 1)


def _nsa_attn_kernel(q_ref, gate_ref, cosq_ref, sinq_ref, qn_ref, kc_ref, vc_ref,
                     ks_ref, vs_ref, kw_ref, vw_ref, kn_ref, cosk_ref, sink_ref, mmat_ref,
                     o_ref, ks_scr, kw_scr):
    grp = pl.program_id(1)
    qi = pl.program_id(2)
    tq = q_ref.shape[0]
    dh = NSA_DH
    hg = q_ref.shape[1] // dh
    S = ks_ref.shape[0]
    tk = min(NSA_KV_TILE, S)
    q0 = qi * tq

    @pl.when(qi == 0)
    def _():
        cosk = cosk_ref[...]
        sink = sink_ref[...]
        for src, dst, r in ((ks_ref, ks_scr, 1), (kw_ref, kw_scr, 2)):
            x = _rms(src[...].astype(F32), kn_ref[r:r + 1, :])
            dst[...] = (x * cosk + pltpu.roll(x, dh // 2, axis=1) * sink).astype(BF16)

    cosq = cosq_ref[...]
    sinq = sinq_ref[...]
    qs = []
    for h in range(hg):
        x = _rms(q_ref[:, h * dh:(h + 1) * dh].astype(F32), qn_ref[...])
        x = (x * cosq + pltpu.roll(x, dh // 2, axis=1) * sinq) * (dh ** -0.5 * LOG2_E)
        qs.append(x.astype(BF16))
    qall = jnp.concatenate(qs, axis=0)

    def tok(n):
        return q0 + lax.broadcasted_iota(jnp.int32, (tq, n), 0)

    def key(n):
        return lax.broadcasted_iota(jnp.int32, (tq, n), 1)

    ncp = kc_ref.shape[2]
    valid_c = key(ncp) * NSA_CMP_STRIDE + (NSA_CMP_LEN - 1) <= tok(ncp)
    p_c, l_c = _softmax2_parts(_dot_nt(qall, kc_ref[0, 0]), jnp.where(valid_c, 0.0, MASK_NEG), hg)
    t_row = q0 + jnp.bitwise_and(lax.broadcasted_iota(jnp.int32, (hg * tq, 1), 0), tq - 1)
    p_c = jnp.where(t_row >= NSA_CMP_LEN - 1, p_c / l_c, 0.0)
    o_c = _dot(p_c.astype(BF16), vc_ref[0, 0])
    p_sum = p_c[0:tq]
    for h in range(1, hg):
        p_sum = p_sum + p_c[h * tq:(h + 1) * tq]
    imp = jnp.dot(p_sum, mmat_ref[...], preferred_element_type=F32, precision=lax.Precision.HIGHEST)

    n_sel = S // NSA_SEL_LEN
    sel_rows = ((n_sel + 7) // 8) * 8
    v_imp = imp.T[0:sel_rows, :]
    jb = lax.broadcasted_iota(jnp.int32, (sel_rows, tq), 0)
    cur = (q0 + lax.broadcasted_iota(jnp.int32, (sel_rows, tq), 1)) // NSA_SEL_LEN
    forced = (jb == 0) | (jb == cur) | (jb == cur - 1)
    v_imp = jnp.where(forced, jnp.inf, jnp.where(jb > cur, -jnp.inf, v_imp))
    if sel_rows > n_sel:
        v_imp = jnp.where(jb >= n_sel, -jnp.inf, v_imp)
    rank = jnp.zeros((sel_rows, tq), jnp.int32)
    for i in range(n_sel):
        r = v_imp[i:i + 1, :]
        beats = (r > v_imp) | ((r == v_imp) & (jb > i))
        rank = rank + beats.astype(jnp.int32)
    sel_t = ((rank < min(NSA_TOP_N, n_sel)) & (jb < n_sel)).astype(F32)
    if sel_rows < LANES:
        sel_t = jnp.concatenate([sel_t, jnp.zeros((LANES - sel_rows, tq), F32)], axis=0)
    sel = sel_t.T.astype(BF16)

    row_t = tok(tk)
    lane_k = key(tk)
    e_row = lax.broadcasted_iota(jnp.int32, (LANES, tk), 0)
    e_blk = lax.broadcasted_iota(jnp.int32, (LANES, tk), 1) // NSA_SEL_LEN

    def sel_step(j, carry):
        m, l, acc = carry
        base = pl.multiple_of(j * tk, tk)
        s = _dot_nt(qall, ks_scr[pl.ds(base, tk), :])
        expand = (e_blk + j * (tk // NSA_SEL_LEN) == e_row).astype(BF16)
        chosen = _dot(sel, expand)
        bias = jnp.where((chosen > 0.5) & (lane_k + base <= row_t), 0.0, MASK_NEG)
        s3 = s.reshape(hg, tq, tk) + bias[None]
        m_new = jnp.maximum(m, jnp.max(s3, axis=-1, keepdims=True))
        alpha = jnp.exp2(m - m_new)
        p = jnp.exp2(s3 - m_new)
        l = alpha * l + jnp.sum(p, axis=-1, keepdims=True)
        pv = _dot(p.reshape(hg * tq, tk).astype(BF16), vs_ref[pl.ds(base, tk), :])
        acc = alpha.reshape(hg * tq, 1) * acc + pv
        return m_new, l, acc

    n_kv = (q0 + tq + tk - 1) // tk
    init = (jnp.full((hg, tq, 1), MASK_NEG, F32), jnp.zeros((hg, tq, 1), F32), jnp.zeros((hg * tq, dh), F32))
    _, l_s, acc_s = lax.fori_loop(0, n_kv, sel_step, init)
    o_s = acc_s / l_s.reshape(hg * tq, 1)

    span = min(tq + NSA_WINDOW, S)
    start = pl.multiple_of(jnp.maximum(q0 + tq - span, 0), tq)
    dist = tok(span) - (start + key(span))
    bias_w = jnp.where((dist >= 0) & (dist < NSA_WINDOW), 0.0, MASK_NEG)
    p_w, l_w = _softmax2_parts(_dot_nt(qall, kw_scr[pl.ds(start, span), :]), bias_w, hg)
    o_w = _dot(p_w.astype(BF16), vw_ref[pl.ds(start, span), :]) / l_w

    gsig = jax.nn.sigmoid(gate_ref[...].astype(F32))
    n_br = 3
    for h in range(hg):
        def gate_col(br):
            lo = h * n_br + br
            hi = lo + hg * n_br
            return jnp.where(grp == 0, gsig[:, lo:lo + 1], gsig[:, hi:hi + 1])
        rows = slice(h * tq, (h + 1) * tq)
        o = gate_col(0) * o_c[rows] + gate_col(1) * o_s[rows] + gate_col(2) * o_w[rows]
        o_ref[:, h * dh:(h + 1) * dh] = o.astype(o_ref.dtype)


def nsa_attention(proj, kcc, vcc, qn, kn, cosn, sinn, B, S):
    H, G, dh = NSA_HEADS, NSA_GROUPS, NSA_DH
    hg = H // G
    T = proj.shape[0]
    tq = min(NSA_Q_TILE, S)
    nq = S // tq
    nr = kcc.shape[2]
    n_c = (S - NSA_CMP_LEN) // NSA_CMP_STRIDE + 1
    n_sel = S // NSA_SEL_LEN
    cs = (np.arange(n_c) * NSA_CMP_STRIDE)[:, None]
    js = (np.arange(n_sel) * NSA_SEL_LEN)[None, :]
    overlap = np.clip(np.minimum(cs + NSA_CMP_LEN, js + NSA_SEL_LEN) - np.maximum(cs, js), 0, None) / NSA_CMP_LEN
    mmat = np.zeros((nr, LANES), np.float32)
    mmat[:n_c, :n_sel] = overlap
    col = lambda base: (lambda b, g, i: (b, base + g))
    return pl.pallas_call(
        _nsa_attn_kernel,
        grid=(B, G, nq),
        in_specs=[pl.BlockSpec((tq, hg * dh), lambda b, g, i: (b * nq + i, g)),
                  pl.BlockSpec((tq, LANES), lambda b, g, i: (b * nq + i, H + 6 * G)),
                  pl.BlockSpec((tq, dh), lambda b, g, i: (b * nq + i, 0)),
                  pl.BlockSpec((tq, dh), lambda b, g, i: (b * nq + i, 0)),
                  pl.BlockSpec((1, dh), lambda b, g, i: (0, 0)),
                  pl.BlockSpec((1, 1, nr, dh), lambda b, g, i: (b, g, 0, 0)),
                  pl.BlockSpec((1, 1, nr, dh), lambda b, g, i: (b, g, 0, 0)),
                  pl.BlockSpec((S, dh), col(H + 2 * G)),
                  pl.BlockSpec((S, dh), col(H + 3 * G)),
                  pl.BlockSpec((S, dh), col(H + 4 * G)),
                  pl.BlockSpec((S, dh), col(H + 5 * G)),
                  pl.BlockSpec(kn.shape, lambda b, g, i: (0, 0)),
                  pl.BlockSpec((S, dh), lambda b, g, i: (b, 0)),
                  pl.BlockSpec((S, dh), lambda b, g, i: (b, 0)),
                  pl.BlockSpec(mmat.shape, lambda b, g, i: (0, 0))],
        out_specs=pl.BlockSpec((tq, hg * dh), lambda b, g, i: (b * nq + i, g)),
        out_shape=jax.ShapeDtypeStruct((T, H * dh), BF16),
        scratch_shapes=[pltpu.VMEM((S, dh), BF16), pltpu.VMEM((S, dh), BF16)],
        compiler_params=_params("parallel", "parallel", "arbitrary"),
        name="nsa_attention",
    )(proj, proj, cosn, sinn, qn.reshape(1, dh), kcc, vcc, proj, proj, proj, proj, kn, cosn, sinn,
      jnp.asarray(mmat))


def _s5_scan_kernel(u_ref, bbd_ref, are_ref, aim_ref, cbd_ref, d_ref, y_ref, bu_scr, xs_scr, st_scr):
    nb = st_scr.shape[0] // 2
    w = are_ref.shape[-1]
    tc = u_ref.shape[0] // nb

    @pl.when(pl.program_id(1) == 0)
    def _():
        st_scr[...] = jnp.zeros_like(st_scr)

    u = u_ref[...]
    bu_scr[...] = _dot(u, bbd_ref[0])
    a_re = jnp.broadcast_to(are_ref[0], (nb, w))
    a_im = jnp.broadcast_to(aim_ref[0], (nb, w))

    def step(t, carry):
        xr, xi = carry
        r0 = pl.multiple_of(t * nb, nb)
        nxr = a_re * xr - a_im * xi + bu_scr[pl.ds(r0, nb), 0:w]
        nxi = a_re * xi + a_im * xr + bu_scr[pl.ds(r0, nb), w:2 * w]
        xs_scr[pl.ds(r0, nb), 0:w] = nxr
        xs_scr[pl.ds(r0, nb), w:2 * w] = nxi
        return nxr, nxi

    xr, xi = lax.fori_loop(0, tc, step, (st_scr[0:nb, :], st_scr[nb:2 * nb, :]), unroll=4)
    st_scr[0:nb, :] = xr
    st_scr[nb:2 * nb, :] = xi
    y = _dot(xs_scr[...].astype(BF16), cbd_ref[0]) + d_ref[...] * u.astype(F32)
    y_ref[...] = _gelu_tanh(y).astype(y_ref.dtype)


def s5_scan(u_tm, a_re, a_im, b_re, b_im, c_re, c_im, d_skip, log_dt, B, S):
    D = u_tm.shape[1]
    Cg, P, NS = S5_GROUP, S5_STATE, S5_SET
    G = D // Cg
    K = G // NS
    w = NS * P
    tc = min(S5_TIME_CHUNK, S)
    dt = jnp.exp(log_dt.astype(F32))[:, None]
    mag = jnp.exp(dt * a_re)
    abar_re, abar_im = mag * jnp.cos(dt * a_im), mag * jnp.sin(dt * a_im)
    den = a_re * a_re + a_im * a_im
    nr_, ni_ = abar_re - 1.0, abar_im
    f_re = (nr_ * a_re + ni_ * a_im) / den
    f_im = (ni_ * a_re - nr_ * a_im) / den
    bbar_re = f_re[..., None] * b_re - f_im[..., None] * b_im
    bbar_im = f_re[..., None] * b_im + f_im[..., None] * b_re
    eye = jnp.eye(NS, dtype=F32)

    def in_blockdiag(bb):
        t = bb.reshape(K, NS, P, Cg).transpose(0, 1, 3, 2)
        return jnp.einsum('kgcp,gh->kgchp', t, eye).reshape(K, NS * Cg, NS * P)

    def out_blockdiag(cc):
        t = cc.reshape(K, NS, Cg, P).transpose(0, 1, 3, 2)
        return jnp.einsum('kgpc,gh->kgphc', t, eye).reshape(K, NS * P, NS * Cg)

    bbd = jnp.concatenate([in_blockdiag(bbar_re), in_blockdiag(bbar_im)], axis=-1).astype(BF16)
    cbd = jnp.concatenate([out_blockdiag(c_re), -out_blockdiag(c_im)], axis=1).astype(BF16)
    are = abar_re.reshape(K, 1, w)
    aim = abar_im.reshape(K, 1, w)
    rows = tc * B
    return pl.pallas_call(
        _s5_scan_kernel,
        grid=(K, S // tc),
        in_specs=[pl.BlockSpec((rows, LANES), lambda k, c: (c, k)),
                  pl.BlockSpec((1, LANES, 2 * w), lambda k, c: (k, 0, 0)),
                  pl.BlockSpec((1, 1, w), lambda k, c: (k, 0, 0)),
                  pl.BlockSpec((1, 1, w), lambda k, c: (k, 0, 0)),
                  pl.BlockSpec((1, 2 * w, LANES), lambda k, c: (k, 0, 0)),
                  pl.BlockSpec((1, LANES), lambda k, c: (0, k))],
        out_specs=pl.BlockSpec((rows, LANES), lambda k, c: (c, k)),
        out_shape=jax.ShapeDtypeStruct((S * B, D), BF16),
        scratch_shapes=[pltpu.VMEM((rows, 2 * w), F32), pltpu.VMEM((rows, 2 * w), F32),
                        pltpu.VMEM((2 * B, w), F32)],
        compiler_params=_params("parallel", "arbitrary"),
        name="s5_scan",
    )(u_tm, bbd, are, aim, cbd, d_skip.reshape(1, D))


def _glu_res_kernel(y_ref, wa_ref, wb_ref, r_ref, o_ref):
    y = y_ref[...]
    a = _dot(y, wa_ref[...])
    b = _dot(y, wb_ref[...])
    o_ref[...] = r_ref[...] + a * jax.nn.sigmoid(b)


def glu_res(y_tm, w, res, B, S, *, tm):
    D = w.shape[0]
    N = w.shape[1] // 2
    ns = S // tm
    return pl.pallas_call(
        _glu_res_kernel,
        grid=(B * ns,),
        in_specs=[pl.BlockSpec((tm, D), lambda i: (i % ns, i // ns)),
                  pl.BlockSpec((D, N), lambda i: (0, 0)),
                  pl.BlockSpec((D, N), lambda i: (0, 1)),
                  pl.BlockSpec((tm, N), lambda i: (i, 0))],
        out_specs=pl.BlockSpec((tm, N), lambda i: (i, 0)),
        out_shape=jax.ShapeDtypeStruct((B * S, N), F32),
        compiler_params=_params("parallel"),
        name="glu_res",
    )(y_tm, w, w, res)


def _ple(h, g_ref, p_ref, wgate_ref, wproj_ref):
    gate = jax.nn.sigmoid(_dot(_rms(h, g_ref[...]).astype(BF16), wgate_ref[...]))
    return h + gate * _dot(p_ref[...].astype(BF16), wproj_ref[...])


def _ffn_kernel(x_ref, g_ref, wg_ref, wu_ref, wd_ref, gp_ref, p_ref, wgate_ref, wproj_ref, o_ref,
                xn_scr, acc_scr):
    f = pl.program_id(1)

    @pl.when(f == 0)
    def _():
        xn_scr[...] = _rms(x_ref[...], g_ref[...]).astype(BF16)
        acc_scr[...] = jnp.zeros_like(acc_scr)

    xn = xn_scr[...]
    gg = _dot(xn, wg_ref[...])
    uu = _dot(xn, wu_ref[...])
    acc_scr[...] += _dot((gg * jax.nn.sigmoid(gg) * uu).astype(BF16), wd_ref[...])

    @pl.when(f == pl.num_programs(1) - 1)
    def _():
        o_ref[...] = _ple(x_ref[...] + acc_scr[...], gp_ref, p_ref, wgate_ref, wproj_ref)


def swiglu_ffn_ple(x, g, w_gate_up, w_down, g_ple, p_all, layer, w_pgate, w_pproj, *, tm, tf):
    T, D = x.shape
    F = w_down.shape[0]
    PD = p_all.shape[2]
    nf = F // tf
    return pl.pallas_call(
        _ffn_kernel,
        grid=(T // tm, nf),
        in_specs=[pl.BlockSpec((tm, D), lambda i, f: (i, 0)),
                  pl.BlockSpec((1, D), lambda i, f: (0, 0)),
                  pl.BlockSpec((D, tf), lambda i, f: (0, f)),
                  pl.BlockSpec((D, tf), lambda i, f: (0, nf + f)),
                  pl.BlockSpec((tf, D), lambda i, f: (f, 0)),
                  pl.BlockSpec((1, D), lambda i, f: (0, 0)),
                  pl.BlockSpec((None, tm, PD), lambda i, f: (layer, i, 0)),
                  pl.BlockSpec((D, D), lambda i, f: (0, 0)),
                  pl.BlockSpec((PD, D), lambda i, f: (0, 0))],
        out_specs=pl.BlockSpec((tm, D), lambda i, f: (i, 0)),
        out_shape=jax.ShapeDtypeStruct((T, D), F32),
        scratch_shapes=[pltpu.VMEM((tm, D), BF16), pltpu.VMEM((tm, D), F32)],
        compiler_params=_params("parallel", "arbitrary"),
        name="swiglu_ffn_ple",
    )(x, g.reshape(1, D), w_gate_up, w_gate_up, w_down, g_ple.reshape(1, D), p_all, w_pgate, w_pproj)


META_E0, META_E1, META_W0, META_W1, META_R0, META_R1 = range(6)


def _router_kernel(x_ref, g_ref, w_ref, meta_ref, cnt_ref, carry_scr, tri_scr):
    @pl.when(pl.program_id(0) == 0)
    def _():
        carry_scr[...] = jnp.zeros_like(carry_scr)
        n = tri_scr.shape[0]
        tri = lax.broadcasted_iota(jnp.int32, (n, n), 0) > lax.broadcasted_iota(jnp.int32, (n, n), 1)
        tri_scr[...] = tri.astype(BF16)

    xn = _rms(x_ref[...], g_ref[...])
    x_hi = xn.astype(BF16)
    x_lo = (xn - x_hi.astype(F32)).astype(BF16)
    w = w_ref[...]
    w_hi = w.astype(BF16)
    w_lo = (w - w_hi.astype(F32)).astype(BF16)
    logits = _dot(x_hi, w_hi) + (_dot(x_hi, w_lo) + _dot(x_lo, w_hi))
    lane = lax.broadcasted_iota(jnp.int32, logits.shape, 1).astype(F32)
    logits = jnp.where(lane < N_EXPERTS, logits, -jnp.inf)

    def top1(v):
        m = jnp.max(v, axis=-1, keepdims=True)
        idx = jnp.min(jnp.where(v == m, lane, float(LANES)), axis=-1, keepdims=True)
        return m, idx, lane == idx

    m1, e0, hot1 = top1(logits)
    m2, e1, hot2 = top1(jnp.where(hot1, -jnp.inf, logits))
    ex = jnp.exp(m2 - m1)
    w0 = 1.0 / (1.0 + ex)
    w1 = ex * w0

    both = jnp.where(hot1 | hot2, 1.0, 0.0)
    before = _dot(tri_scr[...], both.astype(BF16)) + carry_scr[0:1, :]
    r0 = jnp.sum(jnp.where(hot1, before, 0.0), axis=-1, keepdims=True)
    r1 = jnp.sum(jnp.where(hot2, before, 0.0), axis=-1, keepdims=True)
    carry_scr[...] = carry_scr[...] + jnp.sum(both, axis=0, keepdims=True)
    cnt_ref[...] = carry_scr[...]

    meta = jnp.zeros_like(logits)
    for k, v in ((META_E0, e0), (META_E1, e1), (META_W0, w0), (META_W1, w1), (META_R0, r0), (META_R1, r1)):
        meta = jnp.where(lane == float(k), v, meta)
    meta_ref[...] = meta


def router_topk(x, g, w_router, *, tm):
    T, D = x.shape
    w = jnp.zeros((D, LANES), F32).at[:, :N_EXPERTS].set(w_router)
    return pl.pallas_call(
        _router_kernel,
        grid=(T // tm,),
        in_specs=[pl.BlockSpec((tm, D), lambda i: (i, 0)),
                  pl.BlockSpec((1, D), lambda i: (0, 0)),
                  pl.BlockSpec((D, LANES), lambda i: (0, 0))],
        out_specs=[pl.BlockSpec((tm, LANES), lambda i: (i, 0)),
                   pl.BlockSpec((8, LANES), lambda i: (0, 0))],
        out_shape=[jax.ShapeDtypeStruct((T, LANES), F32), jax.ShapeDtypeStruct((8, LANES), F32)],
        scratch_shapes=[pltpu.VMEM((8, LANES), F32), pltpu.VMEM((tm, tm), BF16)],
        compiler_params=_params("arbitrary"),
        name="router",
    )(x, g.reshape(1, D), w)


def _slot_table_kernel(s0_ref, s1_ref, tok_ref):
    i = pl.program_id(0)
    tm = s0_ref.shape[2]
    n_rows, width = tok_ref.shape

    @pl.when(i == 0)
    def _():
        def zero(j, carry):
            tok_ref[j // width, j % width] = 0
            return carry
        lax.fori_loop(0, n_rows * width, zero, 0, unroll=8)

    def scatter(r, carry):
        t = i * tm + r
        a = s0_ref[0, 0, r]
        b = s1_ref[0, 0, r]
        tok_ref[a // width, a % width] = t
        tok_ref[b // width, b % width] = t
        return carry

    lax.fori_loop(0, tm, scatter, 0, unroll=8)


def moe_slot_table(slot0, slot1, n_tiles, width):
    nt, _, tm = slot0.shape
    idx = pl.BlockSpec((1, 1, tm), lambda i: (i, 0, 0), memory_space=pltpu.SMEM)
    return pl.pallas_call(
        _slot_table_kernel,
        grid=(nt,),
        in_specs=[idx, idx],
        out_specs=pl.BlockSpec(memory_space=pltpu.SMEM),
        out_shape=jax.ShapeDtypeStruct((n_tiles, width), jnp.int32),
        compiler_params=_params("arbitrary"),
        name="moe_slot_table",
    )(slot0, slot1)


def _start_row_gathers(idx_ref, src_hbm, dst_ref, sem, lo, hi):
    for r in range(lo, hi):
        pltpu.make_async_copy(src_hbm.at[pl.ds(idx_ref[0, 0, r], 1), :], dst_ref.at[pl.ds(r, 1), :], sem).start()


def _wait_row_gathers(src_hbm, dst_ref, sem):
    pltpu.make_async_copy(src_hbm.at[pl.ds(0, dst_ref.shape[0]), :], dst_ref, sem).wait()


def _moe_expert_kernel(te_ref, nu_ref, tok_ref, tok_next_ref, x_hbm, g_ref, wg_ref, wu_ref, wd_ref, y_ref,
                       xg_scr, xn_scr, acc_scr, sems):
    i = pl.program_id(0)
    f = pl.program_id(1)
    n_tiles = pl.num_programs(0)
    nf = pl.num_programs(1)
    n_used = nu_ref[0]
    active = i < n_used
    tm = xn_scr.shape[0]
    cur = xg_scr.at[i % 2]
    nxt = xg_scr.at[(i + 1) % 2]
    sem_cur = sems.at[i % 2]
    sem_nxt = sems.at[(i + 1) % 2]

    @pl.when((i == 0) & (f == 0))
    def _():
        def issue(r, carry):
            pltpu.make_async_copy(x_hbm.at[pl.ds(tok_ref[0, 0, r], 1), :], cur.at[pl.ds(r, 1), :], sem_cur).start()
            return carry
        lax.fori_loop(0, tm, issue, 0, unroll=8)

    @pl.when((f == 0) & (i <= n_used))
    def _():
        _wait_row_gathers(x_hbm, cur, sem_cur)

    @pl.when(active & (f == 0))
    def _():
        xn_scr[...] = _rms(cur[...], g_ref[...]).astype(BF16)
        acc_scr[...] = jnp.zeros_like(acc_scr)

    def step(lo, hi):
        xn = xn_scr[...]
        gg = _dot(xn, wg_ref[0].astype(BF16))
        uu = _dot(xn, wu_ref[0].astype(BF16))
        acc_scr[...] += _dot((gg * jax.nn.sigmoid(gg) * uu).astype(BF16), wd_ref[0].astype(BF16))
        _start_row_gathers(tok_next_ref, x_hbm, nxt, sem_nxt, lo, hi)

    per = -(-tm // MOE_NF)
    for fs in range(MOE_NF):
        @pl.when(active & (f == fs))
        def _(fs=fs):
            step(min(fs * per, tm), min((fs + 1) * per, tm))

    @pl.when(f == nf - 1)
    def _():
        y_ref[...] = jnp.where(active, acc_scr[...], 0.0)

    @pl.when(active & (i == n_tiles - 1) & (f == nf - 1))
    def _():
        _wait_row_gathers(x_hbm, nxt, sem_nxt)


def moe_expert_ffn(x, g, tok_of_slot, tile_expert, n_used, w_gate_up, w_down, layer, *, tm):
    T, D = x.shape
    _, E, F, _ = w_down.shape
    nf = MOE_NF
    tf = F // nf
    n_tiles = tok_of_slot.shape[0]

    def fblk(i, f, nu):
        return jnp.where(i < nu[0], f, nf - 1)

    tok_spec = lambda off: pl.BlockSpec((1, 1, tm), lambda i, f, te, nu: (jnp.minimum(i + off, n_tiles - 1), 0, 0),
                                        memory_space=pltpu.SMEM)
    grid_spec = pltpu.PrefetchScalarGridSpec(
        num_scalar_prefetch=2,
        grid=(n_tiles, nf),
        in_specs=[tok_spec(0), tok_spec(1),
                  pl.BlockSpec(memory_space=pl.ANY),
                  pl.BlockSpec((1, D), lambda i, f, te, nu: (0, 0)),
                  pl.BlockSpec((None, 1, D, tf), lambda i, f, te, nu: (layer, te[i], 0, fblk(i, f, nu))),
                  pl.BlockSpec((None, 1, D, tf), lambda i, f, te, nu: (layer, te[i], 0, nf + fblk(i, f, nu))),
                  pl.BlockSpec((None, 1, tf, D), lambda i, f, te, nu: (layer, te[i], fblk(i, f, nu), 0))],
        out_specs=pl.BlockSpec((tm, D), lambda i, f, te, nu: (i, 0)),
        scratch_shapes=[pltpu.VMEM((2, tm, D), F32), pltpu.VMEM((tm, D), BF16), pltpu.VMEM((tm, D), F32),
                        pltpu.SemaphoreType.DMA((2,))],
    )
    return pl.pallas_call(
        _moe_expert_kernel,
        grid_spec=grid_spec,
        out_shape=jax.ShapeDtypeStruct((n_tiles * tm, D), F32),
        compiler_params=_params("arbitrary", "arbitrary"),
        name="moe_experts",
    )(tile_expert, n_used, tok_of_slot, tok_of_slot, x, g.reshape(1, D), w_gate_up, w_gate_up, w_down)


def _moe_combine_kernel(s0_ref, s1_ref, s0n_ref, s1n_ref, x_ref, meta_ref, y_hbm, gp_ref, p_ref, wgate_ref,
                        wproj_ref, o_ref, y_scr, sems):
    i = pl.program_id(0)
    n = y_scr.shape[2]
    cur, nxt = i % 2, (i + 1) % 2

    @pl.when(i == 0)
    def _():
        _start_row_gathers(s0_ref, y_hbm, y_scr.at[0, 0], sems.at[0, 0], 0, n)
        _start_row_gathers(s1_ref, y_hbm, y_scr.at[0, 1], sems.at[0, 1], 0, n)

    _wait_row_gathers(y_hbm, y_scr.at[cur, 0], sems.at[cur, 0])
    _wait_row_gathers(y_hbm, y_scr.at[cur, 1], sems.at[cur, 1])
    _start_row_gathers(s0n_ref, y_hbm, y_scr.at[nxt, 0], sems.at[nxt, 0], 0, n)
    _start_row_gathers(s1n_ref, y_hbm, y_scr.at[nxt, 1], sems.at[nxt, 1], 0, n)
    meta = meta_ref[...]
    w0 = meta[:, META_W0:META_W0 + 1]
    w1 = meta[:, META_W1:META_W1 + 1]
    h = x_ref[...] + w0 * y_scr[cur, 0] + w1 * y_scr[cur, 1]
    o_ref[...] = _ple(h, gp_ref, p_ref, wgate_ref, wproj_ref)

    @pl.when(i == pl.num_programs(0) - 1)
    def _():
        _wait_row_gathers(y_hbm, y_scr.at[nxt, 0], sems.at[nxt, 0])
        _wait_row_gathers(y_hbm, y_scr.at[nxt, 1], sems.at[nxt, 1])


def moe_combine_ple(x, meta, slot0, slot1, y, g_ple, p_all, layer, w_pgate, w_pproj, *, tm):
    T, D = x.shape
    PD = p_all.shape[2]
    nt = T // tm
    idx = lambda off: pl.BlockSpec((1, 1, tm), lambda i: (jnp.minimum(i + off, nt - 1), 0, 0),
                                   memory_space=pltpu.SMEM)
    return pl.pallas_call(
        _moe_combine_kernel,
        grid=(nt,),
        in_specs=[idx(0), idx(0), idx(1), idx(1),
                  pl.BlockSpec((tm, D), lambda i: (i, 0)),
                  pl.BlockSpec((tm, LANES), lambda i: (i, 0)),
                  pl.BlockSpec(memory_space=pl.ANY),
                  pl.BlockSpec((1, D), lambda i: (0, 0)),
                  pl.BlockSpec((None, tm, PD), lambda i: (layer, i, 0)),
                  pl.BlockSpec((D, D), lambda i: (0, 0)),
                  pl.BlockSpec((PD, D), lambda i: (0, 0))],
        out_specs=pl.BlockSpec((tm, D), lambda i: (i, 0)),
        out_shape=jax.ShapeDtypeStruct((T, D), F32),
        scratch_shapes=[pltpu.VMEM((2, 2, tm, D), F32), pltpu.SemaphoreType.DMA((2, 2))],
        compiler_params=_params("arbitrary"),
        name="moe_combine_ple",
    )(slot0, slot1, slot0, slot1, x, meta, y, g_ple.reshape(1, D), p_all, w_pgate, w_pproj)


def moe_layer(x, g, w_router, w_gate_up, w_down, layer, ple, *, tm_route, tm_expert, tm_combine):
    T, D = x.shape
    E = w_down.shape[1]
    n_tiles = (2 * T) // tm_expert + E
    meta, cnt = router_topk(x, g, w_router, tm=tm_route)
    e0 = meta[:, META_E0].astype(jnp.int32)
    e1 = meta[:, META_E1].astype(jnp.int32)
    r0 = meta[:, META_R0].astype(jnp.int32)
    r1 = meta[:, META_R1].astype(jnp.int32)
    counts = cnt[0, :E].astype(jnp.int32)
    tiles_per = (counts + tm_expert - 1) // tm_expert
    tile_end = jnp.cumsum(tiles_per)
    group_start = (tile_end - tiles_per) * tm_expert
    slot0 = group_start[e0] + r0
    slot1 = group_start[e1] + r1
    n_used = tile_end[-1:]
    tile_ids = jnp.arange(n_tiles, dtype=jnp.int32)
    tile_expert = jnp.searchsorted(tile_end, jnp.minimum(tile_ids, n_used[0] - 1), side="right").astype(jnp.int32)
    tile_expert = jnp.minimum(tile_expert, E - 1)
    slot0 = slot0.reshape(T // tm_combine, 1, tm_combine)
    slot1 = slot1.reshape(T // tm_combine, 1, tm_combine)
    tok_of_slot = moe_slot_table(slot0, slot1, n_tiles, tm_expert)
    y = moe_expert_ffn(x, g, tok_of_slot.reshape(n_tiles, 1, tm_expert), tile_expert, n_used.astype(jnp.int32),
                       w_gate_up, w_down, layer, tm=tm_expert)
    return moe_combine_ple(x, meta, slot0, slot1, y, *ple, tm=tm_combine)


def _row_tile(S, want):
    return min(want, S)


def retention_layer(h, tabs, B, S, g_norm, w_in, gn, w_out):
    tm = _row_tile(S, 1024)
    proj = norm_matmul(h, g_norm, w_in.astype(BF16), tm=tm, tn=min(2048, w_in.shape[1]), out_dtype=BF16)
    o = retention_core(proj, tabs["ret_cos"], tabs["ret_sin"], gn, B, S)
    return matmul_res(o, w_out.astype(BF16), h, tm=_row_tile(S, 512))


def nsa_layer(h, tabs, B, S, g_norm, w_in, q_norm, k_norm, cmp_pos, cmp_w1, cmp_w2, w_out):
    D, n_in = w_in.shape
    n_pad = -(-n_in // (7 * LANES)) * (7 * LANES)
    w_in_p = jnp.zeros((D, n_pad), BF16).at[:, :n_in].set(w_in.astype(BF16))
    proj = norm_matmul(h, g_norm, w_in_p, tm=_row_tile(S, 1024), tn=7 * LANES, out_dtype=BF16)
    kn = jnp.zeros((8, NSA_DH), F32).at[:3].set(k_norm)
    w1 = cmp_w1.astype(BF16).reshape(2, NSA_CMP_LEN, NSA_DH, cmp_w1.shape[-1])
    kcc, vcc = nsa_compress(proj, cmp_pos, w1, cmp_w2.astype(BF16), kn, tabs["nsa_cos"], tabs["nsa_sin"], B, S)
    o = nsa_attention(proj, kcc, vcc, q_norm, kn, tabs["nsa_cos"], tabs["nsa_sin"], B, S)
    return matmul_res(o, w_out.astype(BF16), h, tm=_row_tile(S, 512))


def s5_layer(h, B, S, g_norm, w_in, a_re, a_im, b_re, b_im, c_re, c_im, d_skip, log_dt, w_glu):
    D = h.shape[1]
    tm = _row_tile(S, 1024)
    u = norm_matmul(h, g_norm, w_in.astype(BF16), tm=tm, tn=D, out_dtype=BF16, time_major_batch=(B, S))
    y = s5_scan(u.reshape(S * B, D), a_re, a_im, b_re, b_im, c_re, c_im, d_skip, log_dt, B, S)
    return glu_res(y.reshape(S, B * D), w_glu.astype(BF16), h, B, S, tm=_row_tile(S, 512))


def kernel(x, p, positions, norm_mix, norm_ffn, norm_ple, ret_w_in, ret_gn, ret_w_out, nsa_w_in, nsa_q_norm, nsa_k_norm, nsa_cmp_pos, nsa_cmp_w1, nsa_cmp_w2, nsa_w_out, s5_w_in, s5_a_re, s5_a_im, s5_b_re, s5_b_im, s5_c_re, s5_c_im, s5_d, s5_log_dt, s5_w_glu, ffn_w_gate_up, ffn_w_down, moe_router, moe_w_gate_up, moe_w_down, ple_w_proj, ple_w_gate):
    B, S, D = x.shape
    depth = p.shape[0]
    T = B * S
    h = x.reshape(T, D)
    pos_col = positions.reshape(T, 1)
    p_all = p.reshape(depth, T, p.shape[-1])
    moe_gu, moe_dn = moe_w_gate_up, moe_w_down

    ones = jnp.ones((1, LANES), F32)
    dk = D // RET_HEADS
    inv_ret = (ROPE_THETA ** (-jnp.arange(0, dk, 2, dtype=F32) / dk)).reshape(1, LANES)
    inv_half = ROPE_THETA ** (-jnp.arange(0, NSA_DH, 2, dtype=F32) / NSA_DH)
    inv_nsa = jnp.concatenate([inv_half, inv_half]).reshape(1, LANES)
    sign_nsa = jnp.concatenate([-jnp.ones((NSA_DH // 2,), F32), jnp.ones((NSA_DH // 2,), F32)]).reshape(1, LANES)
    tabs = {}
    tabs["ret_cos"], tabs["ret_sin"] = rope_tables(pos_col, inv_ret, ones, _row_tile(T, 1024))
    if depth > 1:
        tabs["nsa_cos"], tabs["nsa_sin"] = rope_tables(pos_col, inv_nsa, sign_nsa, _row_tile(T, 1024))

    for i in range(depth):
        m, j = i % 3, i // 3
        if m == 0:
            h = retention_layer(h, tabs, B, S, norm_mix[i], ret_w_in[j], ret_gn[j], ret_w_out[j])
        elif m == 1:
            h = nsa_layer(h, tabs, B, S, norm_mix[i], nsa_w_in[j], nsa_q_norm[j], nsa_k_norm[j],
                          nsa_cmp_pos[j], nsa_cmp_w1[j], nsa_cmp_w2[j], nsa_w_out[j])
        else:
            h = s5_layer(h, B, S, norm_mix[i], s5_w_in[j], s5_a_re[j], s5_a_im[j], s5_b_re[j], s5_b_im[j],
                         s5_c_re[j], s5_c_im[j], s5_d[j], s5_log_dt[j], s5_w_glu[j])
        tm_ffn = _row_tile(T, 1024)
        ple = (norm_ple[i], p_all, i, ple_w_gate[i].astype(BF16), ple_w_proj[i].astype(BF16))
        if i % 2 == 0:
            h = swiglu_ffn_ple(h, norm_ffn[i], ffn_w_gate_up[i // 2].astype(BF16),
                               ffn_w_down[i // 2].astype(BF16), *ple, tm=tm_ffn, tf=256)
        else:
            h = moe_layer(h, norm_ffn[i], moe_router[i // 2], moe_gu, moe_dn, i // 2, ple,
                          tm_route=tm_ffn, tm_expert=tm_ffn, tm_combine=_row_tile(T, 512))
    return h.reshape(B, S, D)
```

```python
import functools
import math

import numpy as np
import jax
import jax.numpy as jnp
from jax import lax
from jax.experimental import pallas as pl
from jax.experimental.pallas import tpu as pltpu

F32 = jnp.float32
BF16 = jnp.bfloat16

NORM_EPS = 1e-6
ROPE_THETA = 10000.0

RET_HEADS = 4
RET_CHUNK = 128
RET_BLOCK = 512

NSA_HEADS = 8
NSA_GROUPS = 2
NSA_DH = 128
NSA_CMP_LEN = 32
NSA_CMP_STRIDE = 16
NSA_SEL_LEN = 64
NSA_TOP_N = 16
NSA_WINDOW = 512
NSA_Q_TILE = 128
NSA_KV_TILE = 512

S5_GROUP = 16
S5_STATE = 64
S5_SET = 8
S5_TIME_CHUNK = 64

N_EXPERTS = 8
MOE_NF = 7
LANES = 128
MASK_NEG = -1e30
LOG2_E = 1.4426950408889634

VMEM_LIMIT = 56 * 1024 * 1024


def _params(*sem):
    return pltpu.CompilerParams(dimension_semantics=sem, vmem_limit_bytes=VMEM_LIMIT)


def _dot(a, b):
    return jnp.dot(a, b, preferred_element_type=F32)


def _dot_nt(a, b):
    return lax.dot_general(a, b, (((1,), (1,)), ((), ())), preferred_element_type=F32)


def _rms(x, g):
    return x * lax.rsqrt(jnp.mean(x * x, axis=-1, keepdims=True) + NORM_EPS) * g


def _gelu_tanh(x):
    return 0.5 * x * (1.0 + jnp.tanh(math.sqrt(2.0 / math.pi) * (x + 0.044715 * (x * x * x))))


def _rope_table_kernel(pos_ref, inv_ref, sign_ref, cos_ref, sin_ref):
    ang = pos_ref[...].astype(F32) * inv_ref[...]
    cos_ref[...] = jnp.cos(ang)
    sin_ref[...] = jnp.sin(ang) * sign_ref[...]


def rope_tables(pos_col, inv, sign, tm):
    T = pos_col.shape[0]
    return pl.pallas_call(
        _rope_table_kernel,
        grid=(T // tm,),
        in_specs=[pl.BlockSpec((tm, 1), lambda i: (i, 0)),
                  pl.BlockSpec((1, LANES), lambda i: (0, 0)),
                  pl.BlockSpec((1, LANES), lambda i: (0, 0))],
        out_specs=[pl.BlockSpec((tm, LANES), lambda i: (i, 0))] * 2,
        out_shape=[jax.ShapeDtypeStruct((T, LANES), F32)] * 2,
        compiler_params=_params("parallel"),
        name="rope_tables",
    )(pos_col, inv, sign)


def _norm_matmul_kernel(x_ref, g_ref, w_ref, o_ref, xn_ref):
    @pl.when(pl.program_id(1) == 0)
    def _():
        xn_ref[...] = _rms(x_ref[...], g_ref[...]).astype(BF16)

    o_ref[...] = _dot(xn_ref[...], w_ref[...]).astype(o_ref.dtype)


def norm_matmul(x, g, w, *, tm, tn, out_dtype, time_major_batch=None):
    T, D = x.shape
    N = w.shape[1]
    nj = N // tn
    if time_major_batch is None:
        out_shape = jax.ShapeDtypeStruct((T, N), out_dtype)
        out_spec = pl.BlockSpec((tm, tn), lambda i, j: (i, j))
    else:
        B, S = time_major_batch
        ns = S // tm
        out_shape = jax.ShapeDtypeStruct((S, B * N), out_dtype)
        out_spec = pl.BlockSpec((tm, tn), lambda i, j: (i % ns, (i // ns) * nj + j))
    return pl.pallas_call(
        _norm_matmul_kernel,
        grid=(T // tm, nj),
        in_specs=[pl.BlockSpec((tm, D), lambda i, j: (i, 0)),
                  pl.BlockSpec((1, D), lambda i, j: (0, 0)),
                  pl.BlockSpec((D, tn), lambda i, j: (0, j))],
        out_specs=out_spec,
        out_shape=out_shape,
        scratch_shapes=[pltpu.VMEM((tm, D), BF16)],
        compiler_params=_params("parallel", "arbitrary"),
        name="norm_matmul",
    )(x, g.reshape(1, D), w)


def _matmul_res_kernel(x_ref, w_ref, r_ref, o_ref):
    o_ref[...] = r_ref[...] + _dot(x_ref[...], w_ref[...])


def matmul_res(x, w, res, *, tm):
    T, K = x.shape
    N = w.shape[1]
    return pl.pallas_call(
        _matmul_res_kernel,
        grid=(T // tm,),
        in_specs=[pl.BlockSpec((tm, K), lambda i: (i, 0)),
                  pl.BlockSpec((K, N), lambda i: (0, 0)),
                  pl.BlockSpec((tm, N), lambda i: (i, 0))],
        out_specs=pl.BlockSpec((tm, N), lambda i: (i, 0)),
        out_shape=jax.ShapeDtypeStruct((T, N), F32),
        compiler_params=_params("parallel"),
        name="matmul_res",
    )(x, w, res)


def _retention_kernel(q_ref, k_ref, v_ref, g_ref, cos_ref, sin_ref, dmask_ref, qdec_ref, kdec_ref,
                      cdec_ref, gn_ref, o_ref, r_scr):
    H, dk, dv = r_scr.shape
    C = dmask_ref.shape[1]
    half = dk // 2

    @pl.when(pl.program_id(1) == 0)
    def _():
        r_scr[...] = jnp.zeros_like(r_scr)

    for c in range(q_ref.shape[0] // C):
        rows = slice(c * C, (c + 1) * C)
        cos = cos_ref[rows, :]
        sin = sin_ref[rows, :]

        def rope(x):
            x1, x2 = x[:, :half], x[:, half:]
            return jnp.concatenate([x1 * cos - x2 * sin, x2 * cos + x1 * sin], axis=-1)

        for h in range(H):
            q = rope(q_ref[rows, h * dk:(h + 1) * dk].astype(F32))
            k = rope(k_ref[rows, h * dk:(h + 1) * dk].astype(F32))
            v = v_ref[rows, h * dv:(h + 1) * dv]
            qb = q.astype(BF16)
            scores = _dot_nt(qb, k.astype(BF16)) * dmask_ref[h]
            intra = _dot(scores.astype(BF16), v)
            r_old = r_scr[h]
            cross = _dot(qb, r_old.astype(BF16)) * qdec_ref[h]
            kd_t = (k * kdec_ref[h]).T.astype(BF16)
            r_scr[h] = r_old * cdec_ref[h] + _dot(kd_t, v)

            o = intra + cross
            mu = jnp.mean(o, axis=-1, keepdims=True)
            oc = o - mu
            var = jnp.mean(oc * oc, axis=-1, keepdims=True)
            on = oc * lax.rsqrt(var + NORM_EPS) * gn_ref[:, h * dv:(h + 1) * dv]
            g = g_ref[rows, h * dv:(h + 1) * dv].astype(F32)
            o_ref[rows, h * dv:(h + 1) * dv] = (on * (g * jax.nn.sigmoid(g))).astype(o_ref.dtype)


def retention_core(proj, cos, sin, gn, B, S):
    H, C = RET_HEADS, RET_CHUNK
    T = proj.shape[0]
    dk = proj.shape[1] // (6 * H)
    dv = 2 * dk
    tb = min(RET_BLOCK, S)
    nb = S // tb
    scale = dk ** -0.5
    log_gamma = jnp.log1p(-jnp.exp2(-5.0 - jnp.arange(H, dtype=F32)))
    idx = jnp.arange(C, dtype=F32)
    diff = idx[:, None] - idx[None, :]
    dmask = jnp.where(diff >= 0, jnp.exp(jnp.maximum(diff, 0.0) * log_gamma[:, None, None]), 0.0) * scale
    qdec = jnp.exp((idx + 1.0) * log_gamma[:, None])[..., None]
    kdec = jnp.exp((C - 1.0 - idx) * log_gamma[:, None])[..., None] * scale
    cdec = jnp.exp(C * log_gamma).reshape(H, 1, 1)
    whole = lambda shape: pl.BlockSpec(shape, lambda b, n: (0,) * len(shape))
    return pl.pallas_call(
        _retention_kernel,
        grid=(B, nb),
        in_specs=[pl.BlockSpec((tb, H * dk), lambda b, n: (b * nb + n, 0)),
                  pl.BlockSpec((tb, H * dk), lambda b, n: (b * nb + n, 1)),
                  pl.BlockSpec((tb, H * dv), lambda b, n: (b * nb + n, 1)),
                  pl.BlockSpec((tb, H * dv), lambda b, n: (b * nb + n, 2)),
                  pl.BlockSpec((tb, dk // 2), lambda b, n: (b * nb + n, 0)),
                  pl.BlockSpec((tb, dk // 2), lambda b, n: (b * nb + n, 0)),
                  whole((H, C, C)), whole((H, C, 1)), whole((H, C, 1)), whole((H, 1, 1)),
                  whole((1, H * dv))],
        out_specs=pl.BlockSpec((tb, H * dv), lambda b, n: (b * nb + n, 0)),
        out_shape=jax.ShapeDtypeStruct((T, H * dv), BF16),
        scratch_shapes=[pltpu.VMEM((H, dk, dv), F32)],
        compiler_params=_params("parallel", "arbitrary"),
        name="retention",
    )(proj, proj, proj, proj, cos, sin, dmask, qdec, kdec, cdec, gn.reshape(1, H * dv))


def _nsa_compress_kernel(kc_ref, vc_ref, pe_ref, w1_ref, w2_ref, kn_ref, cos_ref, sin_ref,
                         kco_ref, vco_ref, xs_scr):
    nh = NSA_CMP_LEN // NSA_CMP_STRIDE
    n_rows = xs_scr.shape[0] // NSA_CMP_STRIDE
    for br, (src, dst) in enumerate(((kc_ref, kco_ref), (vc_ref, vco_ref))):
        xs_scr[...] = src[...].astype(F32)
        acc = [jnp.zeros((n_rows, w1_ref.shape[-1]), F32) for _ in range(nh)]
        for l in range(NSA_CMP_STRIDE):
            piece = xs_scr[pl.ds(l, n_rows, stride=NSA_CMP_STRIDE), :]
            for a in range(nh):
                ll = a * NSA_CMP_STRIDE + l
                acc[a] = acc[a] + _dot((piece + pe_ref[br, ll:ll + 1, :]).astype(BF16), w1_ref[br, ll])
        hid = acc[0] + pltpu.roll(acc[1], n_rows - 1, axis=0)
        z = _dot(_gelu_tanh(hid).astype(BF16), w2_ref[br])
        if br == 0:
            z = _rms(z, kn_ref[0:1, :])
            cos = cos_ref[pl.ds(0, n_rows, stride=NSA_CMP_STRIDE), :]
            sin = sin_ref[pl.ds(0, n_rows, stride=NSA_CMP_STRIDE), :]
            z = z * cos + pltpu.roll(z, NSA_DH // 2, axis=1) * sin
        dst[0, 0] = z.astype(dst.dtype)


def nsa_compress(proj, pe, w1, w2, kn, cosn, sinn, B, S):
    G, dh = NSA_GROUPS, NSA_DH
    nr = S // NSA_CMP_STRIDE
    kc0 = NSA_HEADS
    vc0 = NSA_HEADS + G
    return pl.pallas_call(
        _nsa_compress_kernel,
        grid=(B, G),
        in_specs=[pl.BlockSpec((S, dh), lambda b, g: (b, kc0 + g)),
                  pl.BlockSpec((S, dh), lambda b, g: (b, vc0 + g)),
                  pl.BlockSpec(pe.shape, lambda b, g: (0, 0, 0)),
                  pl.BlockSpec(w1.shape, lambda b, g: (0, 0, 0, 0)),
                  pl.BlockSpec(w2.shape, lambda b, g: (0, 0, 0)),
                  pl.BlockSpec(kn.shape, lambda b, g: (0, 0)),
                  pl.BlockSpec((S, dh), lambda b, g: (b, 0)),
                  pl.BlockSpec((S, dh), lambda b, g: (b, 0))],
        out_specs=[pl.BlockSpec((1, 1, nr, dh), lambda b, g: (b, g, 0, 0))] * 2,
        out_shape=[jax.ShapeDtypeStruct((B, G, nr, dh), BF16)] * 2,
        scratch_shapes=[pltpu.VMEM((S, dh), F32)],
        compiler_params=_params("parallel", "parallel"),
        name="nsa_compress",
    )(proj, proj, pe, w1, w2, kn, cosn, sinn)


def _softmax2_parts(s, bias, hg):
    tq, n = bias.shape
    s3 = s.reshape(hg, tq, n) + bias[None]
    p = jnp.exp2(s3 - jnp.max(s3, axis=-1, keepdims=True))
    return p.reshape(hg * tq, n), jnp.sum(p, axis=-1, keepdims=True).reshape(hg * tq, 1)


def _nsa_attn_kernel(q_ref, gate_ref, cosq_ref, sinq_ref, qn_ref, kc_ref, vc_ref,
                     ks_ref, vs_ref, kw_ref, vw_ref, kn_ref, cosk_ref, sink_ref, mmat_ref,
                     o_ref, ks_scr, kw_scr):
    grp = pl.program_id(1)
    qi = pl.program_id(2)
    tq = q_ref.shape[0]
    dh = NSA_DH
    hg = q_ref.shape[1] // dh
    S = ks_ref.shape[0]
    tk = min(NSA_KV_TILE, S)
    q0 = qi * tq

    @pl.when(qi == 0)
    def _():
        cosk = cosk_ref[...]
        sink = sink_ref[...]
        for src, dst, r in ((ks_ref, ks_scr, 1), (kw_ref, kw_scr, 2)):
            x = _rms(src[...].astype(F32), kn_ref[r:r + 1, :])
            dst[...] = (x * cosk + pltpu.roll(x, dh // 2, axis=1) * sink).astype(BF16)

    cosq = cosq_ref[...]
    sinq = sinq_ref[...]
    qs = []
    for h in range(hg):
        x = _rms(q_ref[:, h * dh:(h + 1) * dh].astype(F32), qn_ref[...])
        x = (x * cosq + pltpu.roll(x, dh // 2, axis=1) * sinq) * (dh ** -0.5 * LOG2_E)
        qs.append(x.astype(BF16))
    qall = jnp.concatenate(qs, axis=0)

    def tok(n):
        return q0 + lax.broadcasted_iota(jnp.int32, (tq, n), 0)

    def key(n):
        return lax.broadcasted_iota(jnp.int32, (tq, n), 1)

    ncp = kc_ref.shape[2]
    valid_c = key(ncp) * NSA_CMP_STRIDE + (NSA_CMP_LEN - 1) <= tok(ncp)
    p_c, l_c = _softmax2_parts(_dot_nt(qall, kc_ref[0, 0]), jnp.where(valid_c, 0.0, MASK_NEG), hg)
    t_row = q0 + jnp.bitwise_and(lax.broadcasted_iota(jnp.int32, (hg * tq, 1), 0), tq - 1)
    p_c = jnp.where(t_row >= NSA_CMP_LEN - 1, p_c / l_c, 0.0)
    o_c = _dot(p_c.astype(BF16), vc_ref[0, 0])
    p_sum = p_c[0:tq]
    for h in range(1, hg):
        p_sum = p_sum + p_c[h * tq:(h + 1) * tq]
    imp = jnp.dot(p_sum, mmat_ref[...], preferred_element_type=F32, precision=lax.Precision.HIGHEST)

    n_sel = S // NSA_SEL_LEN
    sel_rows = ((n_sel + 7) // 8) * 8
    v_imp = imp.T[0:sel_rows, :]
    jb = lax.broadcasted_iota(jnp.int32, (sel_rows, tq), 0)
    cur = (q0 + lax.broadcasted_iota(jnp.int32, (sel_rows, tq), 1)) // NSA_SEL_LEN
    forced = (jb == 0) | (jb == cur) | (jb == cur - 1)
    v_imp = jnp.where(forced, jnp.inf, jnp.where(jb > cur, -jnp.inf, v_imp))
    if sel_rows > n_sel:
        v_imp = jnp.where(jb >= n_sel, -jnp.inf, v_imp)
    rank = jnp.zeros((sel_rows, tq), jnp.int32)
    for i in range(n_sel):
        r = v_imp[i:i + 1, :]
        beats = (r > v_imp) | ((r == v_imp) & (jb > i))
        rank = rank + beats.astype(jnp.int32)
    sel_t = ((rank < min(NSA_TOP_N, n_sel)) & (jb < n_sel)).astype(F32)
    if sel_rows < LANES:
        sel_t = jnp.concatenate([sel_t, jnp.zeros((LANES - sel_rows, tq), F32)], axis=0)
    sel = sel_t.T.astype(BF16)

    row_t = tok(tk)
    lane_k = key(tk)
    e_row = lax.broadcasted_iota(jnp.int32, (LANES, tk), 0)
    e_blk = lax.broadcasted_iota(jnp.int32, (LANES, tk), 1) // NSA_SEL_LEN

    def sel_step(j, carry):
        m, l, acc = carry
        base = pl.multiple_of(j * tk, tk)
        s = _dot_nt(qall, ks_scr[pl.ds(base, tk), :])
        expand = (e_blk + j * (tk // NSA_SEL_LEN) == e_row).astype(BF16)
        chosen = _dot(sel, expand)
        bias = jnp.where((chosen > 0.5) & (lane_k + base <= row_t), 0.0, MASK_NEG)
        s3 = s.reshape(hg, tq, tk) + bias[None]
        m_new = jnp.maximum(m, jnp.max(s3, axis=-1, keepdims=True))
        alpha = jnp.exp2(m - m_new)
        p = jnp.exp2(s3 - m_new)
        l = alpha * l + jnp.sum(p, axis=-1, keepdims=True)
        pv = _dot(p.reshape(hg * tq, tk).astype(BF16), vs_ref[pl.ds(base, tk), :])
        acc = alpha.reshape(hg * tq, 1) * acc + pv
        return m_new, l, acc

    n_kv = (q0 + tq + tk - 1) // tk
    init = (jnp.full((hg, tq, 1), MASK_NEG, F32), jnp.zeros((hg, tq, 1), F32), jnp.zeros((hg * tq, dh), F32))
    _, l_s, acc_s = lax.fori_loop(0, n_kv, sel_step, init)
    o_s = acc_s / l_s.reshape(hg * tq, 1)

    span = min(tq + NSA_WINDOW, S)
    start = pl.multiple_of(jnp.maximum(q0 + tq - span, 0), tq)
    dist = tok(span) - (start + key(span))
    bias_w = jnp.where((dist >= 0) & (dist < NSA_WINDOW), 0.0, MASK_NEG)
    p_w, l_w = _softmax2_parts(_dot_nt(qall, kw_scr[pl.ds(start, span), :]), bias_w, hg)
    o_w = _dot(p_w.astype(BF16), vw_ref[pl.ds(start, span), :]) / l_w

    gsig = jax.nn.sigmoid(gate_ref[...].astype(F32))
    n_br = 3
    for h in range(hg):
        def gate_col(br):
            lo = h * n_br + br
            hi = lo + hg * n_br
            return jnp.where(grp == 0, gsig[:, lo:lo + 1], gsig[:, hi:hi + 1])
        rows = slice(h * tq, (h + 1) * tq)
        o = gate_col(0) * o_c[rows] + gate_col(1) * o_s[rows] + gate_col(2) * o_w[rows]
        o_ref[:, h * dh:(h + 1) * dh] = o.astype(o_ref.dtype)


def nsa_attention(proj, kcc, vcc, qn, kn, cosn, sinn, B, S):
    H, G, dh = NSA_HEADS, NSA_GROUPS, NSA_DH
    hg = H // G
    T = proj.shape[0]
    tq = min(NSA_Q_TILE, S)
    nq = S // tq
    nr = kcc.shape[2]
    n_c = (S - NSA_CMP_LEN) // NSA_CMP_STRIDE + 1
    n_sel = S // NSA_SEL_LEN
    cs = (np.arange(n_c) * NSA_CMP_STRIDE)[:, None]
    js = (np.arange(n_sel) * NSA_SEL_LEN)[None, :]
    overlap = np.clip(np.minimum(cs + NSA_CMP_LEN, js + NSA_SEL_LEN) - np.maximum(cs, js), 0, None) / NSA_CMP_LEN
    mmat = np.zeros((nr, LANES), np.float32)
    mmat[:n_c, :n_sel] = overlap
    col = lambda base: (lambda b, g, i: (b, base + g))
    return pl.pallas_call(
        _nsa_attn_kernel,
        grid=(B, G, nq),
        in_specs=[pl.BlockSpec((tq, hg * dh), lambda b, g, i: (b * nq + i, g)),
                  pl.BlockSpec((tq, LANES), lambda b, g, i: (b * nq + i, H + 6 * G)),
                  pl.BlockSpec((tq, dh), lambda b, g, i: (b * nq + i, 0)),
                  pl.BlockSpec((tq, dh), lambda b, g, i: (b * nq + i, 0)),
                  pl.BlockSpec((1, dh), lambda b, g, i: (0, 0)),
                  pl.BlockSpec((1, 1, nr, dh), lambda b, g, i: (b, g, 0, 0)),
                  pl.BlockSpec((1, 1, nr, dh), lambda b, g, i: (b, g, 0, 0)),
                  pl.BlockSpec((S, dh), col(H + 2 * G)),
                  pl.BlockSpec((S, dh), col(H + 3 * G)),
                  pl.BlockSpec((S, dh), col(H + 4 * G)),
                  pl.BlockSpec((S, dh), col(H + 5 * G)),
                  pl.BlockSpec(kn.shape, lambda b, g, i: (0, 0)),
                  pl.BlockSpec((S, dh), lambda b, g, i: (b, 0)),
                  pl.BlockSpec((S, dh), lambda b, g, i: (b, 0)),
                  pl.BlockSpec(mmat.shape, lambda b, g, i: (0, 0))],
        out_specs=pl.BlockSpec((tq, hg * dh), lambda b, g, i: (b * nq + i, g)),
        out_shape=jax.ShapeDtypeStruct((T, H * dh), BF16),
        scratch_shapes=[pltpu.VMEM((S, dh), BF16), pltpu.VMEM((S, dh), BF16)],
        compiler_params=_params("parallel", "parallel", "arbitrary"),
        name="nsa_attention",
    )(proj, proj, cosn, sinn, qn.reshape(1, dh), kcc, vcc, proj, proj, proj, proj, kn, cosn, sinn,
      jnp.asarray(mmat))


def _s5_scan_kernel(u_ref, bbd_ref, are_ref, aim_ref, cbd_ref, d_ref, y_ref, bu_scr, xs_scr, st_scr):
    nb = st_scr.shape[0] // 2
    w = are_ref.shape[-1]
    tc = u_ref.shape[0] // nb

    @pl.when(pl.program_id(1) == 0)
    def _():
        st_scr[...] = jnp.zeros_like(st_scr)

    u = u_ref[...]
    bu_scr[...] = _dot(u, bbd_ref[0])
    a_re = jnp.broadcast_to(are_ref[0], (nb, w))
    a_im = jnp.broadcast_to(aim_ref[0], (nb, w))

    def step(t, carry):
        xr, xi = carry
        r0 = pl.multiple_of(t * nb, nb)
        nxr = a_re * xr - a_im * xi + bu_scr[pl.ds(r0, nb), 0:w]
        nxi = a_re * xi + a_im * xr + bu_scr[pl.ds(r0, nb), w:2 * w]
        xs_scr[pl.ds(r0, nb), 0:w] = nxr
        xs_scr[pl.ds(r0, nb), w:2 * w] = nxi
        return nxr, nxi

    xr, xi = lax.fori_loop(0, tc, step, (st_scr[0:nb, :], st_scr[nb:2 * nb, :]), unroll=4)
    st_scr[0:nb, :] = xr
    st_scr[nb:2 * nb, :] = xi
    y = _dot(xs_scr[...].astype(BF16), cbd_ref[0]) + d_ref[...] * u.astype(F32)
    y_ref[...] = _gelu_tanh(y).astype(y_ref.dtype)


def s5_scan(u_tm, a_re, a_im, b_re, b_im, c_re, c_im, d_skip, log_dt, B, S):
    D = u_tm.shape[1]
    Cg, P, NS = S5_GROUP, S5_STATE, S5_SET
    G = D // Cg
    K = G // NS
    w = NS * P
    tc = min(S5_TIME_CHUNK, S)
    dt = jnp.exp(log_dt.astype(F32))[:, None]
    mag = jnp.exp(dt * a_re)
    abar_re, abar_im = mag * jnp.cos(dt * a_im), mag * jnp.sin(dt * a_im)
    den = a_re * a_re + a_im * a_im
    nr_, ni_ = abar_re - 1.0, abar_im
    f_re = (nr_ * a_re + ni_ * a_im) / den
    f_im = (ni_ * a_re - nr_ * a_im) / den
    bbar_re = f_re[..., None] * b_re - f_im[..., None] * b_im
    bbar_im = f_re[..., None] * b_im + f_im[..., None] * b_re
    eye = jnp.eye(NS, dtype=F32)

    def in_blockdiag(bb):
        t = bb.reshape(K, NS, P, Cg).transpose(0, 1, 3, 2)
        return jnp.einsum('kgcp,gh->kgchp', t, eye).reshape(K, NS * Cg, NS * P)

    def out_blockdiag(cc):
        t = cc.reshape(K, NS, Cg, P).transpose(0, 1, 3, 2)
        return jnp.einsum('kgpc,gh->kgphc', t, eye).reshape(K, NS * P, NS * Cg)

    bbd = jnp.concatenate([in_blockdiag(bbar_re), in_blockdiag(bbar_im)], axis=-1).astype(BF16)
    cbd = jnp.concatenate([out_blockdiag(c_re), -out_blockdiag(c_im)], axis=1).astype(BF16)
    are = abar_re.reshape(K, 1, w)
    aim = abar_im.reshape(K, 1, w)
    rows = tc * B
    return pl.pallas_call(
        _s5_scan_kernel,
        grid=(K, S // tc),
        in_specs=[pl.BlockSpec((rows, LANES), lambda k, c: (c, k)),
                  pl.BlockSpec((1, LANES, 2 * w), lambda k, c: (k, 0, 0)),
                  pl.BlockSpec((1, 1, w), lambda k, c: (k, 0, 0)),
                  pl.BlockSpec((1, 1, w), lambda k, c: (k, 0, 0)),
                  pl.BlockSpec((1, 2 * w, LANES), lambda k, c: (k, 0, 0)),
                  pl.BlockSpec((1, LANES), lambda k, c: (0, k))],
        out_specs=pl.BlockSpec((rows, LANES), lambda k, c: (c, k)),
        out_shape=jax.ShapeDtypeStruct((S * B, D), BF16),
        scratch_shapes=[pltpu.VMEM((rows, 2 * w), F32), pltpu.VMEM((rows, 2 * w), F32),
                        pltpu.VMEM((2 * B, w), F32)],
        compiler_params=_params("parallel", "arbitrary"),
        name="s5_scan",
    )(u_tm, bbd, are, aim, cbd, d_skip.reshape(1, D))


def _glu_res_kernel(y_ref, wa_ref, wb_ref, r_ref, o_ref):
    y = y_ref[...]
    a = _dot(y, wa_ref[...])
    b = _dot(y, wb_ref[...])
    o_ref[...] = r_ref[...] + a * jax.nn.sigmoid(b)


def glu_res(y_tm, w, res, B, S, *, tm):
    D = w.shape[0]
    N = w.shape[1] // 2
    ns = S // tm
    return pl.pallas_call(
        _glu_res_kernel,
        grid=(B * ns,),
        in_specs=[pl.BlockSpec((tm, D), lambda i: (i % ns, i // ns)),
                  pl.BlockSpec((D, N), lambda i: (0, 0)),
                  pl.BlockSpec((D, N), lambda i: (0, 1)),
                  pl.BlockSpec((tm, N), lambda i: (i, 0))],
        out_specs=pl.BlockSpec((tm, N), lambda i: (i, 0)),
        out_shape=jax.ShapeDtypeStruct((B * S, N), F32),
        compiler_params=_params("parallel"),
        name="glu_res",
    )(y_tm, w, w, res)


def _ple(h, g_ref, p_ref, wgate_ref, wproj_ref):
    gate = jax.nn.sigmoid(_dot(_rms(h, g_ref[...]).astype(BF16), wgate_ref[...]))
    return h + gate * _dot(p_ref[...].astype(BF16), wproj_ref[...])


def _ffn_kernel(x_ref, g_ref, wg_ref, wu_ref, wd_ref, gp_ref, p_ref, wgate_ref, wproj_ref, o_ref,
                xn_scr, acc_scr):
    f = pl.program_id(1)

    @pl.when(f == 0)
    def _():
        xn_scr[...] = _rms(x_ref[...], g_ref[...]).astype(BF16)
        acc_scr[...] = jnp.zeros_like(acc_scr)

    xn = xn_scr[...]
    gg = _dot(xn, wg_ref[...])
    uu = _dot(xn, wu_ref[...])
    acc_scr[...] += _dot((gg * jax.nn.sigmoid(gg) * uu).astype(BF16), wd_ref[...])

    @pl.when(f == pl.num_programs(1) - 1)
    def _():
        o_ref[...] = _ple(x_ref[...] + acc_scr[...], gp_ref, p_ref, wgate_ref, wproj_ref)


def swiglu_ffn_ple(x, g, w_gate_up, w_down, g_ple, p_all, layer, w_pgate, w_pproj, *, tm, tf):
    T, D = x.shape
    F = w_down.shape[0]
    PD = p_all.shape[2]
    nf = F // tf
    return pl.pallas_call(
        _ffn_kernel,
        grid=(T // tm, nf),
        in_specs=[pl.BlockSpec((tm, D), lambda i, f: (i, 0)),
                  pl.BlockSpec((1, D), lambda i, f: (0, 0)),
                  pl.BlockSpec((D, tf), lambda i, f: (0, f)),
                  pl.BlockSpec((D, tf), lambda i, f: (0, nf + f)),
                  pl.BlockSpec((tf, D), lambda i, f: (f, 0)),
                  pl.BlockSpec((1, D), lambda i, f: (0, 0)),
                  pl.BlockSpec((None, tm, PD), lambda i, f: (layer, i, 0)),
                  pl.BlockSpec((D, D), lambda i, f: (0, 0)),
                  pl.BlockSpec((PD, D), lambda i, f: (0, 0))],
        out_specs=pl.BlockSpec((tm, D), lambda i, f: (i, 0)),
        out_shape=jax.ShapeDtypeStruct((T, D), F32),
        scratch_shapes=[pltpu.VMEM((tm, D), BF16), pltpu.VMEM((tm, D), F32)],
        compiler_params=_params("parallel", "arbitrary"),
        name="swiglu_ffn_ple",
    )(x, g.reshape(1, D), w_gate_up, w_gate_up, w_down, g_ple.reshape(1, D), p_all, w_pgate, w_pproj)


META_E0, META_E1, META_W0, META_W1, META_R0, META_R1 = range(6)


def _router_kernel(x_ref, g_ref, w_ref, meta_ref, cnt_ref, carry_scr, tri_scr):
    @pl.when(pl.program_id(0) == 0)
    def _():
        carry_scr[...] = jnp.zeros_like(carry_scr)
        n = tri_scr.shape[0]
        tri = lax.broadcasted_iota(jnp.int32, (n, n), 0) > lax.broadcasted_iota(jnp.int32, (n, n), 1)
        tri_scr[...] = tri.astype(BF16)

    xn = _rms(x_ref[...], g_ref[...])
    x_hi = xn.astype(BF16)
    x_lo = (xn - x_hi.astype(F32)).astype(BF16)
    w = w_ref[...]
    w_hi = w.astype(BF16)
    w_lo = (w - w_hi.astype(F32)).astype(BF16)
    logits = _dot(x_hi, w_hi) + (_dot(x_hi, w_lo) + _dot(x_lo, w_hi))
    lane = lax.broadcasted_iota(jnp.int32, logits.shape, 1).astype(F32)
    logits = jnp.where(lane < N_EXPERTS, logits, -jnp.inf)

    def top1(v):
        m = jnp.max(v, axis=-1, keepdims=True)
        idx = jnp.min(jnp.where(v == m, lane, float(LANES)), axis=-1, keepdims=True)
        return m, idx, lane == idx

    m1, e0, hot1 = top1(logits)
    m2, e1, hot2 = top1(jnp.where(hot1, -jnp.inf, logits))
    ex = jnp.exp(m2 - m1)
    w0 = 1.0 / (1.0 + ex)
    w1 = ex * w0

    both = jnp.where(hot1 | hot2, 1.0, 0.0)
    before = _dot(tri_scr[...], both.astype(BF16)) + carry_scr[0:1, :]
    r0 = jnp.sum(jnp.where(hot1, before, 0.0), axis=-1, keepdims=True)
    r1 = jnp.sum(jnp.where(hot2, before, 0.0), axis=-1, keepdims=True)
    carry_scr[...] = carry_scr[...] + jnp.sum(both, axis=0, keepdims=True)
    cnt_ref[...] = carry_scr[...]

    meta = jnp.zeros_like(logits)
    for k, v in ((META_E0, e0), (META_E1, e1), (META_W0, w0), (META_W1, w1), (META_R0, r0), (META_R1, r1)):
        meta = jnp.where(lane == float(k), v, meta)
    meta_ref[...] = meta


def router_topk(x, g, w_router, *, tm):
    T, D = x.shape
    w = jnp.zeros((D, LANES), F32).at[:, :N_EXPERTS].set(w_router)
    return pl.pallas_call(
        _router_kernel,
        grid=(T // tm,),
        in_specs=[pl.BlockSpec((tm, D), lambda i: (i, 0)),
                  pl.BlockSpec((1, D), lambda i: (0, 0)),
                  pl.BlockSpec((D, LANES), lambda i: (0, 0))],
        out_specs=[pl.BlockSpec((tm, LANES), lambda i: (i, 0)),
                   pl.BlockSpec((8, LANES), lambda i: (0, 0))],
        out_shape=[jax.ShapeDtypeStruct((T, LANES), F32), jax.ShapeDtypeStruct((8, LANES), F32)],
        scratch_shapes=[pltpu.VMEM((8, LANES), F32), pltpu.VMEM((tm, tm), BF16)],
        compiler_params=_params("arbitrary"),
        name="router",
    )(x, g.reshape(1, D), w)


def _start_row_gathers(idx_ref, src_hbm, dst_ref, sem, lo, hi):
    for r in range(lo, hi):
        pltpu.make_async_copy(src_hbm.at[pl.ds(idx_ref[0, 0, r], 1), :], dst_ref.at[pl.ds(r, 1), :], sem).start()


def _wait_row_gathers(src_hbm, dst_ref, sem):
    pltpu.make_async_copy(src_hbm.at[pl.ds(0, dst_ref.shape[0]), :], dst_ref, sem).wait()


def _moe_expert_kernel(te_ref, nu_ref, tok_ref, tok_next_ref, x_hbm, g_ref, wg_ref, wu_ref, wd_ref, y_ref,
                       xg_scr, xn_scr, acc_scr, sems):
    i = pl.program_id(0)
    f = pl.program_id(1)
    n_tiles = pl.num_programs(0)
    nf = pl.num_programs(1)
    n_used = nu_ref[0]
    active = i < n_used
    tm = xn_scr.shape[0]
    cur = xg_scr.at[i % 2]
    nxt = xg_scr.at[(i + 1) % 2]
    sem_cur = sems.at[i % 2]
    sem_nxt = sems.at[(i + 1) % 2]

    @pl.when((i == 0) & (f == 0))
    def _():
        def issue(r, carry):
            pltpu.make_async_copy(x_hbm.at[pl.ds(tok_ref[0, 0, r], 1), :], cur.at[pl.ds(r, 1), :], sem_cur).start()
            return carry
        lax.fori_loop(0, tm, issue, 0, unroll=8)

    @pl.when((f == 0) & (i <= n_used))
    def _():
        _wait_row_gathers(x_hbm, cur, sem_cur)

    @pl.when(active & (f == 0))
    def _():
        xn_scr[...] = _rms(cur[...], g_ref[...]).astype(BF16)
        acc_scr[...] = jnp.zeros_like(acc_scr)

    def step(lo, hi):
        xn = xn_scr[...]
        gg = _dot(xn, wg_ref[0].astype(BF16))
        uu = _dot(xn, wu_ref[0].astype(BF16))
        acc_scr[...] += _dot((gg * jax.nn.sigmoid(gg) * uu).astype(BF16), wd_ref[0].astype(BF16))
        _start_row_gathers(tok_next_ref, x_hbm, nxt, sem_nxt, lo, hi)

    per = -(-tm // MOE_NF)
    for fs in range(MOE_NF):
        @pl.when(active & (f == fs))
        def _(fs=fs):
            step(min(fs * per, tm), min((fs + 1) * per, tm))

    @pl.when(f == nf - 1)
    def _():
        y_ref[...] = jnp.where(active, acc_scr[...], 0.0)

    @pl.when(active & (i == n_tiles - 1) & (f == nf - 1))
    def _():
        _wait_row_gathers(x_hbm, nxt, sem_nxt)


def moe_expert_ffn(x, g, tok_of_slot, tile_expert, n_used, w_gate_up, w_down, layer, *, tm):
    T, D = x.shape
    _, E, F, _ = w_down.shape
    nf = MOE_NF
    tf = F // nf
    n_tiles = tok_of_slot.shape[0]

    def fblk(i, f, nu):
        return jnp.where(i < nu[0], f, nf - 1)

    tok_spec = lambda off: pl.BlockSpec((1, 1, tm), lambda i, f, te, nu: (jnp.minimum(i + off, n_tiles - 1), 0, 0),
                                        memory_space=pltpu.SMEM)
    grid_spec = pltpu.PrefetchScalarGridSpec(
        num_scalar_prefetch=2,
        grid=(n_tiles, nf),
        in_specs=[tok_spec(0), tok_spec(1),
                  pl.BlockSpec(memory_space=pl.ANY),
                  pl.BlockSpec((1, D), lambda i, f, te, nu: (0, 0)),
                  pl.BlockSpec((None, 1, D, tf), lambda i, f, te, nu: (layer, te[i], 0, fblk(i, f, nu))),
                  pl.BlockSpec((None, 1, D, tf), lambda i, f, te, nu: (layer, te[i], 0, nf + fblk(i, f, nu))),
                  pl.BlockSpec((None, 1, tf, D), lambda i, f, te, nu: (layer, te[i], fblk(i, f, nu), 0))],
        out_specs=pl.BlockSpec((tm, D), lambda i, f, te, nu: (i, 0)),
        scratch_shapes=[pltpu.VMEM((2, tm, D), F32), pltpu.VMEM((tm, D), BF16), pltpu.VMEM((tm, D), F32),
                        pltpu.SemaphoreType.DMA((2,))],
    )
    return pl.pallas_call(
        _moe_expert_kernel,
        grid_spec=grid_spec,
        out_shape=jax.ShapeDtypeStruct((n_tiles * tm, D), F32),
        compiler_params=_params("arbitrary", "arbitrary"),
        name="moe_experts",
    )(tile_expert, n_used, tok_of_slot, tok_of_slot, x, g.reshape(1, D), w_gate_up, w_gate_up, w_down)


def _moe_combine_kernel(s0_ref, s1_ref, s0n_ref, s1n_ref, x_ref, meta_ref, y_hbm, gp_ref, p_ref, wgate_ref,
                        wproj_ref, o_ref, y_scr, sems):
    i = pl.program_id(0)
    n = y_scr.shape[2]
    cur, nxt = i % 2, (i + 1) % 2

    @pl.when(i == 0)
    def _():
        _start_row_gathers(s0_ref, y_hbm, y_scr.at[0, 0], sems.at[0, 0], 0, n)
        _start_row_gathers(s1_ref, y_hbm, y_scr.at[0, 1], sems.at[0, 1], 0, n)

    _wait_row_gathers(y_hbm, y_scr.at[cur, 0], sems.at[cur, 0])
    _wait_row_gathers(y_hbm, y_scr.at[cur, 1], sems.at[cur, 1])
    _start_row_gathers(s0n_ref, y_hbm, y_scr.at[nxt, 0], sems.at[nxt, 0], 0, n)
    _start_row_gathers(s1n_ref, y_hbm, y_scr.at[nxt, 1], sems.at[nxt, 1], 0, n)
    meta = meta_ref[...]
    w0 = meta[:, META_W0:META_W0 + 1]
    w1 = meta[:, META_W1:META_W1 + 1]
    h = x_ref[...] + w0 * y_scr[cur, 0] + w1 * y_scr[cur, 1]
    o_ref[...] = _ple(h, gp_ref, p_ref, wgate_ref, wproj_ref)

    @pl.when(i == pl.num_programs(0) - 1)
    def _():
        _wait_row_gathers(y_hbm, y_scr.at[nxt, 0], sems.at[nxt, 0])
        _wait_row_gathers(y_hbm, y_scr.at[nxt, 1], sems.at[nxt, 1])


def moe_combine_ple(x, meta, slot0, slot1, y, g_ple, p_all, layer, w_pgate, w_pproj, *, tm):
    T, D = x.shape
    PD = p_all.shape[2]
    nt = T // tm
    idx = lambda off: pl.BlockSpec((1, 1, tm), lambda i: (jnp.minimum(i + off, nt - 1), 0, 0),
                                   memory_space=pltpu.SMEM)
    return pl.pallas_call(
        _moe_combine_kernel,
        grid=(nt,),
        in_specs=[idx(0), idx(0), idx(1), idx(1),
                  pl.BlockSpec((tm, D), lambda i: (i, 0)),
                  pl.BlockSpec((tm, LANES), lambda i: (i, 0)),
                  pl.BlockSpec(memory_space=pl.ANY),
                  pl.BlockSpec((1, D), lambda i: (0, 0)),
                  pl.BlockSpec((None, tm, PD), lambda i: (layer, i, 0)),
                  pl.BlockSpec((D, D), lambda i: (0, 0)),
                  pl.BlockSpec((PD, D), lambda i: (0, 0))],
        out_specs=pl.BlockSpec((tm, D), lambda i: (i, 0)),
        out_shape=jax.ShapeDtypeStruct((T, D), F32),
        scratch_shapes=[pltpu.VMEM((2, 2, tm, D), F32), pltpu.SemaphoreType.DMA((2, 2))],
        compiler_params=_params("arbitrary"),
        name="moe_combine_ple",
    )(slot0, slot1, slot0, slot1, x, meta, y, g_ple.reshape(1, D), p_all, w_pgate, w_pproj)


def moe_layer(x, g, w_router, w_gate_up, w_down, layer, ple, *, tm_route, tm_expert, tm_combine):
    T, D = x.shape
    E = w_down.shape[1]
    n_tiles = (2 * T) // tm_expert + E
    meta, cnt = router_topk(x, g, w_router, tm=tm_route)
    e0 = meta[:, META_E0].astype(jnp.int32)
    e1 = meta[:, META_E1].astype(jnp.int32)
    r0 = meta[:, META_R0].astype(jnp.int32)
    r1 = meta[:, META_R1].astype(jnp.int32)
    counts = cnt[0, :E].astype(jnp.int32)
    tiles_per = (counts + tm_expert - 1) // tm_expert
    tile_end = jnp.cumsum(tiles_per)
    group_start = (tile_end - tiles_per) * tm_expert
    slot0 = group_start[e0] + r0
    slot1 = group_start[e1] + r1
    n_used = tile_end[-1:]
    tile_ids = jnp.arange(n_tiles, dtype=jnp.int32)
    tile_expert = jnp.searchsorted(tile_end, jnp.minimum(tile_ids, n_used[0] - 1), side="right").astype(jnp.int32)
    tile_expert = jnp.minimum(tile_expert, E - 1)
    tok = jnp.arange(T, dtype=jnp.int32)
    tok_of_slot = jnp.zeros((n_tiles * tm_expert,), jnp.int32).at[slot0].set(tok).at[slot1].set(tok)
    slot0 = slot0.reshape(T // tm_combine, 1, tm_combine)
    slot1 = slot1.reshape(T // tm_combine, 1, tm_combine)
    y = moe_expert_ffn(x, g, tok_of_slot.reshape(n_tiles, 1, tm_expert), tile_expert, n_used.astype(jnp.int32),
                       w_gate_up, w_down, layer, tm=tm_expert)
    return moe_combine_ple(x, meta, slot0, slot1, y, *ple, tm=tm_combine)


def _row_tile(S, want):
    return min(want, S)


def retention_layer(h, tabs, B, S, g_norm, w_in, gn, w_out):
    tm = _row_tile(S, 1024)
    proj = norm_matmul(h, g_norm, w_in.astype(BF16), tm=tm, tn=min(2048, w_in.shape[1]), out_dtype=BF16)
    o = retention_core(proj, tabs["ret_cos"], tabs["ret_sin"], gn, B, S)
    return matmul_res(o, w_out.astype(BF16), h, tm=_row_tile(S, 512))


def nsa_layer(h, tabs, B, S, g_norm, w_in, q_norm, k_norm, cmp_pos, cmp_w1, cmp_w2, w_out):
    D, n_in = w_in.shape
    n_pad = -(-n_in // (7 * LANES)) * (7 * LANES)
    w_in_p = jnp.zeros((D, n_pad), BF16).at[:, :n_in].set(w_in.astype(BF16))
    proj = norm_matmul(h, g_norm, w_in_p, tm=_row_tile(S, 1024), tn=7 * LANES, out_dtype=BF16)
    kn = jnp.zeros((8, NSA_DH), F32).at[:3].set(k_norm)
    w1 = cmp_w1.astype(BF16).reshape(2, NSA_CMP_LEN, NSA_DH, cmp_w1.shape[-1])
    kcc, vcc = nsa_compress(proj, cmp_pos, w1, cmp_w2.astype(BF16), kn, tabs["nsa_cos"], tabs["nsa_sin"], B, S)
    o = nsa_attention(proj, kcc, vcc, q_norm, kn, tabs["nsa_cos"], tabs["nsa_sin"], B, S)
    return matmul_res(o, w_out.astype(BF16), h, tm=_row_tile(S, 512))


def s5_layer(h, B, S, g_norm, w_in, a_re, a_im, b_re, b_im, c_re, c_im, d_skip, log_dt, w_glu):
    D = h.shape[1]
    tm = _row_tile(S, 1024)
    u = norm_matmul(h, g_norm, w_in.astype(BF16), tm=tm, tn=D, out_dtype=BF16, time_major_batch=(B, S))
    y = s5_scan(u.reshape(S * B, D), a_re, a_im, b_re, b_im, c_re, c_im, d_skip, log_dt, B, S)
    return glu_res(y.reshape(S, B * D), w_glu.astype(BF16), h, B, S, tm=_row_tile(S, 512))


def kernel(x, p, positions, norm_mix, norm_ffn, norm_ple, ret_w_in, ret_gn, ret_w_out, nsa_w_in, nsa_q_norm, nsa_k_norm, nsa_cmp_pos, nsa_cmp_w1, nsa_cmp_w2, nsa_w_out, s5_w_in, s5_a_re, s5_a_im, s5_b_re, s5_b_im, s5_c_re, s5_c_im, s5_d, s5_log_dt, s5_w_glu, ffn_w_gate_up, ffn_w_down, moe_router, moe_w_gate_up, moe_w_down, ple_w_proj, ple_w_gate):
    B, S, D = x.shape
    depth = p.shape[0]
    T = B * S
    h = x.reshape(T, D)
    pos_col = positions.reshape(T, 1)
    p_all = p.reshape(depth, T, p.shape[-1])
    moe_gu, moe_dn = moe_w_gate_up, moe_w_down

    ones = jnp.ones((1, LANES), F32)
    dk = D // RET_HEADS
    inv_ret = (ROPE_THETA ** (-jnp.arange(0, dk, 2, dtype=F32) / dk)).reshape(1, LANES)
    inv_half = ROPE_THETA ** (-jnp.arange(0, NSA_DH, 2, dtype=F32) / NSA_DH)
    inv_nsa = jnp.concatenate([inv_half, inv_half]).reshape(1, LANES)
    sign_nsa = jnp.concatenate([-jnp.ones((NSA_DH // 2,), F32), jnp.ones((NSA_DH // 2,), F32)]).reshape(1, LANES)
    tabs = {}
    tabs["ret_cos"], tabs["ret_sin"] = rope_tables(pos_col, inv_ret, ones, _row_tile(T, 1024))
    if depth > 1:
        tabs["nsa_cos"], tabs["nsa_sin"] = rope_tables(pos_col, inv_nsa, sign_nsa, _row_tile(T, 1024))

    for i in range(depth):
        m, j = i % 3, i // 3
        if m == 0:
            h = retention_layer(h, tabs, B, S, norm_mix[i], ret_w_in[j], ret_gn[j], ret_w_out[j])
        elif m == 1:
            h = nsa_layer(h, tabs, B, S, norm_mix[i], nsa_w_in[j], nsa_q_norm[j], nsa_k_norm[j],
                          nsa_cmp_pos[j], nsa_cmp_w1[j], nsa_cmp_w2[j], nsa_w_out[j])
        else:
            h = s5_layer(h, B, S, norm_mix[i], s5_w_in[j], s5_a_re[j], s5_a_im[j], s5_b_re[j], s5_b_im[j],
                         s5_c_re[j], s5_c_im[j], s5_d[j], s5_log_dt[j], s5_w_glu[j])
        tm_ffn = _row_tile(T, 1024)
        ple = (norm_ple[i], p_all, i, ple_w_gate[i].astype(BF16), ple_w_proj[i].astype(BF16))
        if i % 2 == 0:
            h = swiglu_ffn_ple(h, norm_ffn[i], ffn_w_gate_up[i // 2].astype(BF16),
                               ffn_w_down[i // 2].astype(BF16), *ple, tm=tm_ffn, tf=256)
        else:
            h = moe_layer(h, norm_ffn[i], moe_router[i // 2], moe_gu, moe_dn, i // 2, ple,
                          tm_route=tm_ffn, tm_expert=tm_ffn, tm_combine=_row_tile(T, 512))
    return h.reshape(B, S, D)
```

```python
import functools
import math

import numpy as np
import jax
import jax.numpy as jnp
from jax import lax
from jax.experimental import pallas as pl
from jax.experimental.pallas import tpu as pltpu

F32 = jnp.float32
BF16 = jnp.bfloat16

NORM_EPS = 1e-6
ROPE_THETA = 10000.0

RET_HEADS = 4
RET_CHUNK = 128
RET_BLOCK = 512

NSA_HEADS = 8
NSA_GROUPS = 2
NSA_DH = 128
NSA_CMP_LEN = 32
NSA_CMP_STRIDE = 16
NSA_SEL_LEN = 64
NSA_TOP_N = 16
NSA_WINDOW = 512
NSA_Q_TILE = 256
NSA_KV_TILE = 512

S5_GROUP = 16
S5_STATE = 64
S5_SET = 8
S5_TIME_CHUNK = 64

N_EXPERTS = 8
MOE_NF = 7
LANES = 128
MASK_NEG = -1e30
LOG2_E = 1.4426950408889634

VMEM_LIMIT = 56 * 1024 * 1024


def _params(*sem):
    return pltpu.CompilerParams(dimension_semantics=sem, vmem_limit_bytes=VMEM_LIMIT)


def _dot(a, b):
    return jnp.dot(a, b, preferred_element_type=F32)


def _dot_nt(a, b):
    return lax.dot_general(a, b, (((1,), (1,)), ((), ())), preferred_element_type=F32)


def _rms(x, g):
    return x * lax.rsqrt(jnp.mean(x * x, axis=-1, keepdims=True) + NORM_EPS) * g


def _gelu_tanh(x):
    return 0.5 * x * (1.0 + jnp.tanh(math.sqrt(2.0 / math.pi) * (x + 0.044715 * (x * x * x))))


def _rope_table_kernel(pos_ref, inv_ref, sign_ref, cos_ref, sin_ref):
    ang = pos_ref[...].astype(F32) * inv_ref[...]
    cos_ref[...] = jnp.cos(ang)
    sin_ref[...] = jnp.sin(ang) * sign_ref[...]


def rope_tables(pos_col, inv, sign, tm):
    T = pos_col.shape[0]
    return pl.pallas_call(
        _rope_table_kernel,
        grid=(T // tm,),
        in_specs=[pl.BlockSpec((tm, 1), lambda i: (i, 0)),
                  pl.BlockSpec((1, LANES), lambda i: (0, 0)),
                  pl.BlockSpec((1, LANES), lambda i: (0, 0))],
        out_specs=[pl.BlockSpec((tm, LANES), lambda i: (i, 0))] * 2,
        out_shape=[jax.ShapeDtypeStruct((T, LANES), F32)] * 2,
        compiler_params=_params("parallel"),
        name="rope_tables",
    )(pos_col, inv, sign)


def _norm_matmul_kernel(x_ref, g_ref, w_ref, o_ref, xn_ref):
    @pl.when(pl.program_id(1) == 0)
    def _():
        xn_ref[...] = _rms(x_ref[...], g_ref[...]).astype(BF16)

    o_ref[...] = _dot(xn_ref[...], w_ref[...]).astype(o_ref.dtype)


def norm_matmul(x, g, w, *, tm, tn, out_dtype, time_major_batch=None):
    T, D = x.shape
    N = w.shape[1]
    nj = N // tn
    if time_major_batch is None:
        out_shape = jax.ShapeDtypeStruct((T, N), out_dtype)
        out_spec = pl.BlockSpec((tm, tn), lambda i, j: (i, j))
    else:
        B, S = time_major_batch
        ns = S // tm
        out_shape = jax.ShapeDtypeStruct((S, B * N), out_dtype)
        out_spec = pl.BlockSpec((tm, tn), lambda i, j: (i % ns, (i // ns) * nj + j))
    return pl.pallas_call(
        _norm_matmul_kernel,
        grid=(T // tm, nj),
        in_specs=[pl.BlockSpec((tm, D), lambda i, j: (i, 0)),
                  pl.BlockSpec((1, D), lambda i, j: (0, 0)),
                  pl.BlockSpec((D, tn), lambda i, j: (0, j))],
        out_specs=out_spec,
        out_shape=out_shape,
        scratch_shapes=[pltpu.VMEM((tm, D), BF16)],
        compiler_params=_params("parallel", "arbitrary"),
        name="norm_matmul",
    )(x, g.reshape(1, D), w)


def _matmul_res_kernel(x_ref, w_ref, r_ref, o_ref):
    o_ref[...] = r_ref[...] + _dot(x_ref[...], w_ref[...])


def matmul_res(x, w, res, *, tm):
    T, K = x.shape
    N = w.shape[1]
    return pl.pallas_call(
        _matmul_res_kernel,
        grid=(T // tm,),
        in_specs=[pl.BlockSpec((tm, K), lambda i: (i, 0)),
                  pl.BlockSpec((K, N), lambda i: (0, 0)),
                  pl.BlockSpec((tm, N), lambda i: (i, 0))],
        out_specs=pl.BlockSpec((tm, N), lambda i: (i, 0)),
        out_shape=jax.ShapeDtypeStruct((T, N), F32),
        compiler_params=_params("parallel"),
        name="matmul_res",
    )(x, w, res)


def _retention_kernel(q_ref, k_ref, v_ref, g_ref, cos_ref, sin_ref, dmask_ref, qdec_ref, kdec_ref,
                      cdec_ref, gn_ref, o_ref, r_scr):
    H, dk, dv = r_scr.shape
    C = dmask_ref.shape[1]
    half = dk // 2

    @pl.when(pl.program_id(1) == 0)
    def _():
        r_scr[...] = jnp.zeros_like(r_scr)

    for c in range(q_ref.shape[0] // C):
        rows = slice(c * C, (c + 1) * C)
        cos = cos_ref[rows, :]
        sin = sin_ref[rows, :]

        def rope(x):
            x1, x2 = x[:, :half], x[:, half:]
            return jnp.concatenate([x1 * cos - x2 * sin, x2 * cos + x1 * sin], axis=-1)

        for h in range(H):
            q = rope(q_ref[rows, h * dk:(h + 1) * dk].astype(F32))
            k = rope(k_ref[rows, h * dk:(h + 1) * dk].astype(F32))
            v = v_ref[rows, h * dv:(h + 1) * dv]
            qb = q.astype(BF16)
            scores = _dot_nt(qb, k.astype(BF16)) * dmask_ref[h]
            intra = _dot(scores.astype(BF16), v)
            r_old = r_scr[h]
            cross = _dot(qb, r_old.astype(BF16)) * qdec_ref[h]
            kd_t = (k * kdec_ref[h]).T.astype(BF16)
            r_scr[h] = r_old * cdec_ref[h] + _dot(kd_t, v)

            o = intra + cross
            mu = jnp.mean(o, axis=-1, keepdims=True)
            oc = o - mu
            var = jnp.mean(oc * oc, axis=-1, keepdims=True)
            on = oc * lax.rsqrt(var + NORM_EPS) * gn_ref[:, h * dv:(h + 1) * dv]
            g = g_ref[rows, h * dv:(h + 1) * dv].astype(F32)
            o_ref[rows, h * dv:(h + 1) * dv] = (on * (g * jax.nn.sigmoid(g))).astype(o_ref.dtype)


def retention_core(proj, cos, sin, gn, B, S):
    H, C = RET_HEADS, RET_CHUNK
    T = proj.shape[0]
    dk = proj.shape[1] // (6 * H)
    dv = 2 * dk
    tb = min(RET_BLOCK, S)
    nb = S // tb
    scale = dk ** -0.5
    log_gamma = jnp.log1p(-jnp.exp2(-5.0 - jnp.arange(H, dtype=F32)))
    idx = jnp.arange(C, dtype=F32)
    diff = idx[:, None] - idx[None, :]
    dmask = jnp.where(diff >= 0, jnp.exp(jnp.maximum(diff, 0.0) * log_gamma[:, None, None]), 0.0) * scale
    qdec = jnp.exp((idx + 1.0) * log_gamma[:, None])[..., None]
    kdec = jnp.exp((C - 1.0 - idx) * log_gamma[:, None])[..., None] * scale
    cdec = jnp.exp(C * log_gamma).reshape(H, 1, 1)
    whole = lambda shape: pl.BlockSpec(shape, lambda b, n: (0,) * len(shape))
    return pl.pallas_call(
        _retention_kernel,
        grid=(B, nb),
        in_specs=[pl.BlockSpec((tb, H * dk), lambda b, n: (b * nb + n, 0)),
                  pl.BlockSpec((tb, H * dk), lambda b, n: (b * nb + n, 1)),
                  pl.BlockSpec((tb, H * dv), lambda b, n: (b * nb + n, 1)),
                  pl.BlockSpec((tb, H * dv), lambda b, n: (b * nb + n, 2)),
                  pl.BlockSpec((tb, dk // 2), lambda b, n: (b * nb + n, 0)),
                  pl.BlockSpec((tb, dk // 2), lambda b, n: (b * nb + n, 0)),
                  whole((H, C, C)), whole((H, C, 1)), whole((H, C, 1)), whole((H, 1, 1)),
                  whole((1, H * dv))],
        out_specs=pl.BlockSpec((tb, H * dv), lambda b, n: (b * nb + n, 0)),
        out_shape=jax.ShapeDtypeStruct((T, H * dv), BF16),
        scratch_shapes=[pltpu.VMEM((H, dk, dv), F32)],
        compiler_params=_params("parallel", "arbitrary"),
        name="retention",
    )(proj, proj, proj, proj, cos, sin, dmask, qdec, kdec, cdec, gn.reshape(1, H * dv))


def _nsa_compress_kernel(kc_ref, vc_ref, pe_ref, w1_ref, w2_ref, kn_ref, cos_ref, sin_ref,
                         kco_ref, vco_ref, xs_scr):
    nh = NSA_CMP_LEN // NSA_CMP_STRIDE
    n_rows = xs_scr.shape[0] // NSA_CMP_STRIDE
    for br, (src, dst) in enumerate(((kc_ref, kco_ref), (vc_ref, vco_ref))):
        xs_scr[...] = src[...].astype(F32)
        acc = [jnp.zeros((n_rows, w1_ref.shape[-1]), F32) for _ in range(nh)]
        for l in range(NSA_CMP_STRIDE):
            piece = xs_scr[pl.ds(l, n_rows, stride=NSA_CMP_STRIDE), :]
            for a in range(nh):
                ll = a * NSA_CMP_STRIDE + l
                acc[a] = acc[a] + _dot((piece + pe_ref[br, ll:ll + 1, :]).astype(BF16), w1_ref[br, ll])
        hid = acc[0] + pltpu.roll(acc[1], n_rows - 1, axis=0)
        z = _dot(_gelu_tanh(hid).astype(BF16), w2_ref[br])
        if br == 0:
            z = _rms(z, kn_ref[0:1, :])
            cos = cos_ref[pl.ds(0, n_rows, stride=NSA_CMP_STRIDE), :]
            sin = sin_ref[pl.ds(0, n_rows, stride=NSA_CMP_STRIDE), :]
            z = z * cos + pltpu.roll(z, NSA_DH // 2, axis=1) * sin
        dst[0, 0] = z.astype(dst.dtype)


def nsa_compress(proj, pe, w1, w2, kn, cosn, sinn, B, S):
    G, dh = NSA_GROUPS, NSA_DH
    nr = S // NSA_CMP_STRIDE
    kc0 = NSA_HEADS
    vc0 = NSA_HEADS + G
    return pl.pallas_call(
        _nsa_compress_kernel,
        grid=(B, G),
        in_specs=[pl.BlockSpec((S, dh), lambda b, g: (b, kc0 + g)),
                  pl.BlockSpec((S, dh), lambda b, g: (b, vc0 + g)),
                  pl.BlockSpec(pe.shape, lambda b, g: (0, 0, 0)),
                  pl.BlockSpec(w1.shape, lambda b, g: (0, 0, 0, 0)),
                  pl.BlockSpec(w2.shape, lambda b, g: (0, 0, 0)),
                  pl.BlockSpec(kn.shape, lambda b, g: (0, 0)),
                  pl.BlockSpec((S, dh), lambda b, g: (b, 0)),
                  pl.BlockSpec((S, dh), lambda b, g: (b, 0))],
        out_specs=[pl.BlockSpec((1, 1, nr, dh), lambda b, g: (b, g, 0, 0))] * 2,
        out_shape=[jax.ShapeDtypeStruct((B, G, nr, dh), BF16)] * 2,
        scratch_shapes=[pltpu.VMEM((S, dh), F32)],
        compiler_params=_params("parallel", "parallel"),
        name="nsa_compress",
    )(proj, proj, pe, w1, w2, kn, cosn, sinn)


def _softmax2_parts(s, bias, hg):
    tq, n = bias.shape
    s3 = s.reshape(hg, tq, n) + bias[None]
    p = jnp.exp2(s3 - jnp.max(s3, axis=-1, keepdims=True))
    return p.reshape(hg * tq, n), jnp.sum(p, axis=-1, keepdims=True).reshape(hg * tq, 1)


def _nsa_attn_kernel(q_ref, gate_ref, cosq_ref, sinq_ref, qn_ref, kc_ref, vc_ref,
                     ks_ref, vs_ref, kw_ref, vw_ref, kn_ref, cosk_ref, sink_ref, mmat_ref,
                     o_ref, ks_scr, kw_scr):
    grp = pl.program_id(1)
    qi = pl.program_id(2)
    tq = q_ref.shape[0]
    dh = NSA_DH
    hg = q_ref.shape[1] // dh
    S = ks_ref.shape[0]
    tk = min(NSA_KV_TILE, S)
    q0 = qi * tq

    @pl.when(qi == 0)
    def _():
        cosk = cosk_ref[...]
        sink = sink_ref[...]
        for src, dst, r in ((ks_ref, ks_scr, 1), (kw_ref, kw_scr, 2)):
            x = _rms(src[...].astype(F32), kn_ref[r:r + 1, :])
            dst[...] = (x * cosk + pltpu.roll(x, dh // 2, axis=1) * sink).astype(BF16)

    cosq = cosq_ref[...]
    sinq = sinq_ref[...]
    qs = []
    for h in range(hg):
        x = _rms(q_ref[:, h * dh:(h + 1) * dh].astype(F32), qn_ref[...])
        x = (x * cosq + pltpu.roll(x, dh // 2, axis=1) * sinq) * (dh ** -0.5 * LOG2_E)
        qs.append(x.astype(BF16))
    qall = jnp.concatenate(qs, axis=0)

    def tok(n):
        return q0 + lax.broadcasted_iota(jnp.int32, (tq, n), 0)

    def key(n):
        return lax.broadcasted_iota(jnp.int32, (tq, n), 1)

    ncp = kc_ref.shape[2]
    valid_c = key(ncp) * NSA_CMP_STRIDE + (NSA_CMP_LEN - 1) <= tok(ncp)
    p_c, l_c = _softmax2_parts(_dot_nt(qall, kc_ref[0, 0]), jnp.where(valid_c, 0.0, MASK_NEG), hg)
    t_row = q0 + jnp.bitwise_and(lax.broadcasted_iota(jnp.int32, (hg * tq, 1), 0), tq - 1)
    p_c = jnp.where(t_row >= NSA_CMP_LEN - 1, p_c / l_c, 0.0)
    o_c = _dot(p_c.astype(BF16), vc_ref[0, 0])
    p_sum = p_c[0:tq]
    for h in range(1, hg):
        p_sum = p_sum + p_c[h * tq:(h + 1) * tq]
    imp = jnp.dot(p_sum, mmat_ref[...], preferred_element_type=F32, precision=lax.Precision.HIGHEST)

    n_sel = S // NSA_SEL_LEN
    sel_rows = ((n_sel + 7) // 8) * 8
    v_imp = imp.T[0:sel_rows, :]
    jb = lax.broadcasted_iota(jnp.int32, (sel_rows, tq), 0)
    cur = (q0 + lax.broadcasted_iota(jnp.int32, (sel_rows, tq), 1)) // NSA_SEL_LEN
    forced = (jb == 0) | (jb == cur) | (jb == cur - 1)
    v_imp = jnp.where(forced, jnp.inf, jnp.where(jb > cur, -jnp.inf, v_imp))
    if sel_rows > n_sel:
        v_imp = jnp.where(jb >= n_sel, -jnp.inf, v_imp)
    rank = jnp.zeros((sel_rows, tq), jnp.int32)
    for i in range(n_sel):
        r = v_imp[i:i + 1, :]
        beats = (r > v_imp) | ((r == v_imp) & (jb > i))
        rank = rank + beats.astype(jnp.int32)
    sel_t = ((rank < min(NSA_TOP_N, n_sel)) & (jb < n_sel)).astype(F32)
    if sel_rows < LANES:
        sel_t = jnp.concatenate([sel_t, jnp.zeros((LANES - sel_rows, tq), F32)], axis=0)
    sel = sel_t.T.astype(BF16)

    row_t = tok(tk)
    lane_k = key(tk)
    e_row = lax.broadcasted_iota(jnp.int32, (LANES, tk), 0)
    e_blk = lax.broadcasted_iota(jnp.int32, (LANES, tk), 1) // NSA_SEL_LEN

    def sel_step(j, carry):
        m, l, acc = carry
        base = pl.multiple_of(j * tk, tk)
        s = _dot_nt(qall, ks_scr[pl.ds(base, tk), :])
        expand = (e_blk + j * (tk // NSA_SEL_LEN) == e_row).astype(BF16)
        chosen = _dot(sel, expand)
        bias = jnp.where((chosen > 0.5) & (lane_k + base <= row_t), 0.0, MASK_NEG)
        s3 = s.reshape(hg, tq, tk) + bias[None]
        m_new = jnp.maximum(m, jnp.max(s3, axis=-1, keepdims=True))
        alpha = jnp.exp2(m - m_new)
        p = jnp.exp2(s3 - m_new)
        l = alpha * l + jnp.sum(p, axis=-1, keepdims=True)
        pv = _dot(p.reshape(hg * tq, tk).astype(BF16), vs_ref[pl.ds(base, tk), :])
        acc = alpha.reshape(hg * tq, 1) * acc + pv
        return m_new, l, acc

    n_kv = (q0 + tq + tk - 1) // tk
    init = (jnp.full((hg, tq, 1), MASK_NEG, F32), jnp.zeros((hg, tq, 1), F32), jnp.zeros((hg * tq, dh), F32))
    _, l_s, acc_s = lax.fori_loop(0, n_kv, sel_step, init)
    o_s = acc_s / l_s.reshape(hg * tq, 1)

    span = min(tq + NSA_WINDOW, S)
    start = pl.multiple_of(jnp.maximum(q0 + tq - span, 0), tq)
    dist = tok(span) - (start + key(span))
    bias_w = jnp.where((dist >= 0) & (dist < NSA_WINDOW), 0.0, MASK_NEG)
    p_w, l_w = _softmax2_parts(_dot_nt(qall, kw_scr[pl.ds(start, span), :]), bias_w, hg)
    o_w = _dot(p_w.astype(BF16), vw_ref[pl.ds(start, span), :]) / l_w

    gsig = jax.nn.sigmoid(gate_ref[...].astype(F32))
    n_br = 3
    for h in range(hg):
        def gate_col(br):
            lo = h * n_br + br
            hi = lo + hg * n_br
            return jnp.where(grp == 0, gsig[:, lo:lo + 1], gsig[:, hi:hi + 1])
        rows = slice(h * tq, (h + 1) * tq)
        o = gate_col(0) * o_c[rows] + gate_col(1) * o_s[rows] + gate_col(2) * o_w[rows]
        o_ref[:, h * dh:(h + 1) * dh] = o.astype(o_ref.dtype)


def nsa_attention(proj, kcc, vcc, qn, kn, cosn, sinn, B, S):
    H, G, dh = NSA_HEADS, NSA_GROUPS, NSA_DH
    hg = H // G
    T = proj.shape[0]
    tq = min(NSA_Q_TILE, S)
    nq = S // tq
    nr = kcc.shape[2]
    n_c = (S - NSA_CMP_LEN) // NSA_CMP_STRIDE + 1
    n_sel = S // NSA_SEL_LEN
    cs = (np.arange(n_c) * NSA_CMP_STRIDE)[:, None]
    js = (np.arange(n_sel) * NSA_SEL_LEN)[None, :]
    overlap = np.clip(np.minimum(cs + NSA_CMP_LEN, js + NSA_SEL_LEN) - np.maximum(cs, js), 0, None) / NSA_CMP_LEN
    mmat = np.zeros((nr, LANES), np.float32)
    mmat[:n_c, :n_sel] = overlap
    col = lambda base: (lambda b, g, i: (b, base + g))
    return pl.pallas_call(
        _nsa_attn_kernel,
        grid=(B, G, nq),
        in_specs=[pl.BlockSpec((tq, hg * dh), lambda b, g, i: (b * nq + i, g)),
                  pl.BlockSpec((tq, LANES), lambda b, g, i: (b * nq + i, H + 6 * G)),
                  pl.BlockSpec((tq, dh), lambda b, g, i: (b * nq + i, 0)),
                  pl.BlockSpec((tq, dh), lambda b, g, i: (b * nq + i, 0)),
                  pl.BlockSpec((1, dh), lambda b, g, i: (0, 0)),
                  pl.BlockSpec((1, 1, nr, dh), lambda b, g, i: (b, g, 0, 0)),
                  pl.BlockSpec((1, 1, nr, dh), lambda b, g, i: (b, g, 0, 0)),
                  pl.BlockSpec((S, dh), col(H + 2 * G)),
                  pl.BlockSpec((S, dh), col(H + 3 * G)),
                  pl.BlockSpec((S, dh), col(H + 4 * G)),
                  pl.BlockSpec((S, dh), col(H + 5 * G)),
                  pl.BlockSpec(kn.shape, lambda b, g, i: (0, 0)),
                  pl.BlockSpec((S, dh), lambda b, g, i: (b, 0)),
                  pl.BlockSpec((S, dh), lambda b, g, i: (b, 0)),
                  pl.BlockSpec(mmat.shape, lambda b, g, i: (0, 0))],
        out_specs=pl.BlockSpec((tq, hg * dh), lambda b, g, i: (b * nq + i, g)),
        out_shape=jax.ShapeDtypeStruct((T, H * dh), BF16),
        scratch_shapes=[pltpu.VMEM((S, dh), BF16), pltpu.VMEM((S, dh), BF16)],
        compiler_params=_params("parallel", "parallel", "arbitrary"),
        name="nsa_attention",
    )(proj, proj, cosn, sinn, qn.reshape(1, dh), kcc, vcc, proj, proj, proj, proj, kn, cosn, sinn,
      jnp.asarray(mmat))


def _s5_scan_kernel(u_ref, bbd_ref, are_ref, aim_ref, cbd_ref, d_ref, y_ref, bu_scr, xs_scr, st_scr):
    nb = st_scr.shape[0] // 2
    w = are_ref.shape[-1]
    tc = u_ref.shape[0] // nb

    @pl.when(pl.program_id(1) == 0)
    def _():
        st_scr[...] = jnp.zeros_like(st_scr)

    u = u_ref[...]
    bu_scr[...] = _dot(u, bbd_ref[0])
    a_re = jnp.broadcast_to(are_ref[0], (nb, w))
    a_im = jnp.broadcast_to(aim_ref[0], (nb, w))

    def step(t, carry):
        xr, xi = carry
        r0 = pl.multiple_of(t * nb, nb)
        nxr = a_re * xr - a_im * xi + bu_scr[pl.ds(r0, nb), 0:w]
        nxi = a_re * xi + a_im * xr + bu_scr[pl.ds(r0, nb), w:2 * w]
        xs_scr[pl.ds(r0, nb), 0:w] = nxr
        xs_scr[pl.ds(r0, nb), w:2 * w] = nxi
        return nxr, nxi

    xr, xi = lax.fori_loop(0, tc, step, (st_scr[0:nb, :], st_scr[nb:2 * nb, :]), unroll=4)
    st_scr[0:nb, :] = xr
    st_scr[nb:2 * nb, :] = xi
    y = _dot(xs_scr[...].astype(BF16), cbd_ref[0]) + d_ref[...] * u.astype(F32)
    y_ref[...] = _gelu_tanh(y).astype(y_ref.dtype)


def s5_scan(u_tm, a_re, a_im, b_re, b_im, c_re, c_im, d_skip, log_dt, B, S):
    D = u_tm.shape[1]
    Cg, P, NS = S5_GROUP, S5_STATE, S5_SET
    G = D // Cg
    K = G // NS
    w = NS * P
    tc = min(S5_TIME_CHUNK, S)
    dt = jnp.exp(log_dt.astype(F32))[:, None]
    mag = jnp.exp(dt * a_re)
    abar_re, abar_im = mag * jnp.cos(dt * a_im), mag * jnp.sin(dt * a_im)
    den = a_re * a_re + a_im * a_im
    nr_, ni_ = abar_re - 1.0, abar_im
    f_re = (nr_ * a_re + ni_ * a_im) / den
    f_im = (ni_ * a_re - nr_ * a_im) / den
    bbar_re = f_re[..., None] * b_re - f_im[..., None] * b_im
    bbar_im = f_re[..., None] * b_im + f_im[..., None] * b_re
    eye = jnp.eye(NS, dtype=F32)

    def in_blockdiag(bb):
        t = bb.reshape(K, NS, P, Cg).transpose(0, 1, 3, 2)
        return jnp.einsum('kgcp,gh->kgchp', t, eye).reshape(K, NS * Cg, NS * P)

    def out_blockdiag(cc):
        t = cc.reshape(K, NS, Cg, P).transpose(0, 1, 3, 2)
        return jnp.einsum('kgpc,gh->kgphc', t, eye).reshape(K, NS * P, NS * Cg)

    bbd = jnp.concatenate([in_blockdiag(bbar_re), in_blockdiag(bbar_im)], axis=-1).astype(BF16)
    cbd = jnp.concatenate([out_blockdiag(c_re), -out_blockdiag(c_im)], axis=1).astype(BF16)
    are = abar_re.reshape(K, 1, w)
    aim = abar_im.reshape(K, 1, w)
    rows = tc * B
    return pl.pallas_call(
        _s5_scan_kernel,
        grid=(K, S // tc),
        in_specs=[pl.BlockSpec((rows, LANES), lambda k, c: (c, k)),
                  pl.BlockSpec((1, LANES, 2 * w), lambda k, c: (k, 0, 0)),
                  pl.BlockSpec((1, 1, w), lambda k, c: (k, 0, 0)),
                  pl.BlockSpec((1, 1, w), lambda k, c: (k, 0, 0)),
                  pl.BlockSpec((1, 2 * w, LANES), lambda k, c: (k, 0, 0)),
                  pl.BlockSpec((1, LANES), lambda k, c: (0, k))],
        out_specs=pl.BlockSpec((rows, LANES), lambda k, c: (c, k)),
        out_shape=jax.ShapeDtypeStruct((S * B, D), BF16),
        scratch_shapes=[pltpu.VMEM((rows, 2 * w), F32), pltpu.VMEM((rows, 2 * w), F32),
                        pltpu.VMEM((2 * B, w), F32)],
        compiler_params=_params("parallel", "arbitrary"),
        name="s5_scan",
    )(u_tm, bbd, are, aim, cbd, d_skip.reshape(1, D))


def _glu_res_kernel(y_ref, wa_ref, wb_ref, r_ref, o_ref):
    y = y_ref[...]
    a = _dot(y, wa_ref[...])
    b = _dot(y, wb_ref[...])
    o_ref[...] = r_ref[...] + a * jax.nn.sigmoid(b)


def glu_res(y_tm, w, res, B, S, *, tm):
    D = w.shape[0]
    N = w.shape[1] // 2
    ns = S // tm
    return pl.pallas_call(
        _glu_res_kernel,
        grid=(B * ns,),
        in_specs=[pl.BlockSpec((tm, D), lambda i: (i % ns, i // ns)),
                  pl.BlockSpec((D, N), lambda i: (0, 0)),
                  pl.BlockSpec((D, N), lambda i: (0, 1)),
                  pl.BlockSpec((tm, N), lambda i: (i, 0))],
        out_specs=pl.BlockSpec((tm, N), lambda i: (i, 0)),
        out_shape=jax.ShapeDtypeStruct((B * S, N), F32),
        compiler_params=_params("parallel"),
        name="glu_res",
    )(y_tm, w, w, res)


def _ple(h, g_ref, p_ref, wgate_ref, wproj_ref):
    gate = jax.nn.sigmoid(_dot(_rms(h, g_ref[...]).astype(BF16), wgate_ref[...]))
    return h + gate * _dot(p_ref[...].astype(BF16), wproj_ref[...])


def _ffn_kernel(x_ref, g_ref, wg_ref, wu_ref, wd_ref, gp_ref, p_ref, wgate_ref, wproj_ref, o_ref,
                xn_scr, acc_scr):
    f = pl.program_id(1)

    @pl.when(f == 0)
    def _():
        xn_scr[...] = _rms(x_ref[...], g_ref[...]).astype(BF16)
        acc_scr[...] = jnp.zeros_like(acc_scr)

    xn = xn_scr[...]
    gg = _dot(xn, wg_ref[...])
    uu = _dot(xn, wu_ref[...])
    acc_scr[...] += _dot((gg * jax.nn.sigmoid(gg) * uu).astype(BF16), wd_ref[...])

    @pl.when(f == pl.num_programs(1) - 1)
    def _():
        o_ref[...] = _ple(x_ref[...] + acc_scr[...], gp_ref, p_ref, wgate_ref, wproj_ref)


def swiglu_ffn_ple(x, g, w_gate_up, w_down, g_ple, p_all, layer, w_pgate, w_pproj, *, tm, tf):
    T, D = x.shape
    F = w_down.shape[0]
    PD = p_all.shape[2]
    nf = F // tf
    return pl.pallas_call(
        _ffn_kernel,
        grid=(T // tm, nf),
        in_specs=[pl.BlockSpec((tm, D), lambda i, f: (i, 0)),
                  pl.BlockSpec((1, D), lambda i, f: (0, 0)),
                  pl.BlockSpec((D, tf), lambda i, f: (0, f)),
                  pl.BlockSpec((D, tf), lambda i, f: (0, nf + f)),
                  pl.BlockSpec((tf, D), lambda i, f: (f, 0)),
                  pl.BlockSpec((1, D), lambda i, f: (0, 0)),
                  pl.BlockSpec((None, tm, PD), lambda i, f: (layer, i, 0)),
                  pl.BlockSpec((D, D), lambda i, f: (0, 0)),
                  pl.BlockSpec((PD, D), lambda i, f: (0, 0))],
        out_specs=pl.BlockSpec((tm, D), lambda i, f: (i, 0)),
        out_shape=jax.ShapeDtypeStruct((T, D), F32),
        scratch_shapes=[pltpu.VMEM((tm, D), BF16), pltpu.VMEM((tm, D), F32)],
        compiler_params=_params("parallel", "arbitrary"),
        name="swiglu_ffn_ple",
    )(x, g.reshape(1, D), w_gate_up, w_gate_up, w_down, g_ple.reshape(1, D), p_all, w_pgate, w_pproj)


META_E0, META_E1, META_W0, META_W1, META_R0, META_R1 = range(6)


def _router_kernel(x_ref, g_ref, w_ref, meta_ref, cnt_ref, carry_scr, tri_scr):
    @pl.when(pl.program_id(0) == 0)
    def _():
        carry_scr[...] = jnp.zeros_like(carry_scr)
        n = tri_scr.shape[0]
        tri = lax.broadcasted_iota(jnp.int32, (n, n), 0) > lax.broadcasted_iota(jnp.int32, (n, n), 1)
        tri_scr[...] = tri.astype(BF16)

    xn = _rms(x_ref[...], g_ref[...])
    x_hi = xn.astype(BF16)
    x_lo = (xn - x_hi.astype(F32)).astype(BF16)
    w = w_ref[...]
    w_hi = w.astype(BF16)
    w_lo = (w - w_hi.astype(F32)).astype(BF16)
    logits = _dot(x_hi, w_hi) + (_dot(x_hi, w_lo) + _dot(x_lo, w_hi))
    lane = lax.broadcasted_iota(jnp.int32, logits.shape, 1).astype(F32)
    logits = jnp.where(lane < N_EXPERTS, logits, -jnp.inf)

    def top1(v):
        m = jnp.max(v, axis=-1, keepdims=True)
        idx = jnp.min(jnp.where(v == m, lane, float(LANES)), axis=-1, keepdims=True)
        return m, idx, lane == idx

    m1, e0, hot1 = top1(logits)
    m2, e1, hot2 = top1(jnp.where(hot1, -jnp.inf, logits))
    ex = jnp.exp(m2 - m1)
    w0 = 1.0 / (1.0 + ex)
    w1 = ex * w0

    both = jnp.where(hot1 | hot2, 1.0, 0.0)
    before = _dot(tri_scr[...], both.astype(BF16)) + carry_scr[0:1, :]
    r0 = jnp.sum(jnp.where(hot1, before, 0.0), axis=-1, keepdims=True)
    r1 = jnp.sum(jnp.where(hot2, before, 0.0), axis=-1, keepdims=True)
    carry_scr[...] = carry_scr[...] + jnp.sum(both, axis=0, keepdims=True)
    cnt_ref[...] = carry_scr[...]

    meta = jnp.zeros_like(logits)
    for k, v in ((META_E0, e0), (META_E1, e1), (META_W0, w0), (META_W1, w1), (META_R0, r0), (META_R1, r1)):
        meta = jnp.where(lane == float(k), v, meta)
    meta_ref[...] = meta


def router_topk(x, g, w_router, *, tm):
    T, D = x.shape
    w = jnp.zeros((D, LANES), F32).at[:, :N_EXPERTS].set(w_router)
    return pl.pallas_call(
        _router_kernel,
        grid=(T // tm,),
        in_specs=[pl.BlockSpec((tm, D), lambda i: (i, 0)),
                  pl.BlockSpec((1, D), lambda i: (0, 0)),
                  pl.BlockSpec((D, LANES), lambda i: (0, 0))],
        out_specs=[pl.BlockSpec((tm, LANES), lambda i: (i, 0)),
                   pl.BlockSpec((8, LANES), lambda i: (0, 0))],
        out_shape=[jax.ShapeDtypeStruct((T, LANES), F32), jax.ShapeDtypeStruct((8, LANES), F32)],
        scratch_shapes=[pltpu.VMEM((8, LANES), F32), pltpu.VMEM((tm, tm), BF16)],
        compiler_params=_params("arbitrary"),
        name="router",
    )(x, g.reshape(1, D), w)


def _start_row_gathers(idx_ref, src_hbm, dst_ref, sem, lo, hi, idx_off=0):
    for r in range(lo, hi):
        pltpu.make_async_copy(src_hbm.at[pl.ds(idx_ref[0, 0, idx_off + r], 1), :], dst_ref.at[pl.ds(r, 1), :],
                              sem).start()


def _wait_row_gathers(src_hbm, dst_ref, sem):
    pltpu.make_async_copy(src_hbm.at[pl.ds(0, dst_ref.shape[0]), :], dst_ref, sem).wait()


def _moe_expert_kernel(te_ref, nu_ref, tok_ref, tok_next_ref, x_hbm, g_ref, wg_ref, wu_ref, wd_ref, y_ref,
                       xg_scr, xn_scr, acc_scr, sems):
    i = pl.program_id(0)
    f = pl.program_id(1)
    n_tiles = pl.num_programs(0)
    nf = pl.num_programs(1)
    n_used = nu_ref[0]
    active = i < n_used
    tm = xn_scr.shape[0]
    cur = xg_scr.at[i % 2]
    nxt = xg_scr.at[(i + 1) % 2]
    sem_cur = sems.at[i % 2]
    sem_nxt = sems.at[(i + 1) % 2]

    @pl.when((i == 0) & (f == 0))
    def _():
        def issue(r, carry):
            pltpu.make_async_copy(x_hbm.at[pl.ds(tok_ref[0, 0, r], 1), :], cur.at[pl.ds(r, 1), :], sem_cur).start()
            return carry
        lax.fori_loop(0, tm, issue, 0, unroll=8)

    @pl.when((f == 0) & (i <= n_used))
    def _():
        _wait_row_gathers(x_hbm, cur, sem_cur)

    @pl.when(active & (f == 0))
    def _():
        xn_scr[...] = _rms(cur[...], g_ref[...]).astype(BF16)
        acc_scr[...] = jnp.zeros_like(acc_scr)

    def step(lo, hi):
        xn = xn_scr[...]
        gg = _dot(xn, wg_ref[0].astype(BF16))
        uu = _dot(xn, wu_ref[0].astype(BF16))
        acc_scr[...] += _dot((gg * jax.nn.sigmoid(gg) * uu).astype(BF16), wd_ref[0].astype(BF16))
        _start_row_gathers(tok_next_ref, x_hbm, nxt, sem_nxt, lo, hi)

    per = -(-tm // MOE_NF)
    for fs in range(MOE_NF):
        @pl.when(active & (f == fs))
        def _(fs=fs):
            step(min(fs * per, tm), min((fs + 1) * per, tm))

    @pl.when(f == nf - 1)
    def _():
        y_ref[...] = jnp.where(active, acc_scr[...], 0.0)

    @pl.when(active & (i == n_tiles - 1) & (f == nf - 1))
    def _():
        _wait_row_gathers(x_hbm, nxt, sem_nxt)


def moe_expert_ffn(x, g, tok_of_slot, tile_expert, n_used, w_gate_up, w_down, layer, *, tm):
    T, D = x.shape
    _, E, F, _ = w_down.shape
    nf = MOE_NF
    tf = F // nf
    n_tiles = tok_of_slot.shape[0]

    def fblk(i, f, nu):
        return jnp.where(i < nu[0], f, nf - 1)

    tok_spec = lambda off: pl.BlockSpec((1, 1, tm), lambda i, f, te, nu: (jnp.minimum(i + off, n_tiles - 1), 0, 0),
                                        memory_space=pltpu.SMEM)
    grid_spec = pltpu.PrefetchScalarGridSpec(
        num_scalar_prefetch=2,
        grid=(n_tiles, nf),
        in_specs=[tok_spec(0), tok_spec(1),
                  pl.BlockSpec(memory_space=pl.ANY),
                  pl.BlockSpec((1, D), lambda i, f, te, nu: (0, 0)),
                  pl.BlockSpec((None, 1, D, tf), lambda i, f, te, nu: (layer, te[i], 0, fblk(i, f, nu))),
                  pl.BlockSpec((None, 1, D, tf), lambda i, f, te, nu: (layer, te[i], 0, nf + fblk(i, f, nu))),
                  pl.BlockSpec((None, 1, tf, D), lambda i, f, te, nu: (layer, te[i], fblk(i, f, nu), 0))],
        out_specs=pl.BlockSpec((tm, D), lambda i, f, te, nu: (i, 0)),
        scratch_shapes=[pltpu.VMEM((2, tm, D), F32), pltpu.VMEM((tm, D), BF16), pltpu.VMEM((tm, D), F32),
                        pltpu.SemaphoreType.DMA((2,))],
    )
    return pl.pallas_call(
        _moe_expert_kernel,
        grid_spec=grid_spec,
        out_shape=jax.ShapeDtypeStruct((n_tiles * tm, D), F32),
        compiler_params=_params("arbitrary", "arbitrary"),
        name="moe_experts",
    )(tile_expert, n_used, tok_of_slot, tok_of_slot, x, g.reshape(1, D), w_gate_up, w_gate_up, w_down)


def _moe_combine_kernel(s0_ref, s1_ref, s0n_ref, s1n_ref, x_ref, meta_ref, y_hbm, gp_ref, p_ref, wgate_ref,
                        wproj_ref, o_ref, ya0, ya1, yb0, yb1, sems):
    i = pl.program_id(0)
    n = ya0.shape[0]

    def gather(idx0, idx1, buf0, buf1, k, off):
        _start_row_gathers(idx0, y_hbm, buf0, sems.at[k, 0], 0, n, off)
        _start_row_gathers(idx1, y_hbm, buf1, sems.at[k, 1], 0, n, off)

    def wait(buf0, buf1, k):
        _wait_row_gathers(y_hbm, buf0, sems.at[k, 0])
        _wait_row_gathers(y_hbm, buf1, sems.at[k, 1])

    def combine(buf0, buf1, rows):
        meta = meta_ref[rows, :]
        w0 = meta[:, META_W0:META_W0 + 1]
        w1 = meta[:, META_W1:META_W1 + 1]
        h = x_ref[rows, :] + w0 * buf0[...] + w1 * buf1[...]
        gate = jax.nn.sigmoid(_dot(_rms(h, gp_ref[...]).astype(BF16), wgate_ref[...]))
        o_ref[rows, :] = h + gate * _dot(p_ref[rows, :].astype(BF16), wproj_ref[...])

    @pl.when(i == 0)
    def _():
        gather(s0_ref, s1_ref, ya0, ya1, 0, 0)

    wait(ya0, ya1, 0)
    gather(s0_ref, s1_ref, yb0, yb1, 1, n)
    combine(ya0, ya1, slice(0, n))
    wait(yb0, yb1, 1)
    gather(s0n_ref, s1n_ref, ya0, ya1, 0, 0)
    combine(yb0, yb1, slice(n, 2 * n))

    @pl.when(i == pl.num_programs(0) - 1)
    def _():
        wait(ya0, ya1, 0)


def moe_combine_ple(x, meta, slot0, slot1, y, g_ple, p_all, layer, w_pgate, w_pproj, *, tm):
    T, D = x.shape
    PD = p_all.shape[2]
    nt = T // tm
    idx = lambda off: pl.BlockSpec((1, 1, tm), lambda i: (jnp.minimum(i + off, nt - 1), 0, 0),
                                   memory_space=pltpu.SMEM)
    return pl.pallas_call(
        _moe_combine_kernel,
        grid=(nt,),
        in_specs=[idx(0), idx(0), idx(1), idx(1),
                  pl.BlockSpec((tm, D), lambda i: (i, 0)),
                  pl.BlockSpec((tm, LANES), lambda i: (i, 0)),
                  pl.BlockSpec(memory_space=pl.ANY),
                  pl.BlockSpec((1, D), lambda i: (0, 0)),
                  pl.BlockSpec((None, tm, PD), lambda i: (layer, i, 0)),
                  pl.BlockSpec((D, D), lambda i: (0, 0)),
                  pl.BlockSpec((PD, D), lambda i: (0, 0))],
        out_specs=pl.BlockSpec((tm, D), lambda i: (i, 0)),
        out_shape=jax.ShapeDtypeStruct((T, D), F32),
        scratch_shapes=[pltpu.VMEM((tm // 2, D), F32)] * 4 + [pltpu.SemaphoreType.DMA((2, 2))],
        compiler_params=_params("arbitrary"),
        name="moe_combine_ple",
    )(slot0, slot1, slot0, slot1, x, meta, y, g_ple.reshape(1, D), p_all, w_pgate, w_pproj)


def moe_layer(x, g, w_router, w_gate_up, w_down, layer, ple, *, tm_route, tm_expert, tm_combine):
    T, D = x.shape
    E = w_down.shape[1]
    n_tiles = (2 * T) // tm_expert + E
    meta, cnt = router_topk(x, g, w_router, tm=tm_route)
    e0 = meta[:, META_E0].astype(jnp.int32)
    e1 = meta[:, META_E1].astype(jnp.int32)
    r0 = meta[:, META_R0].astype(jnp.int32)
    r1 = meta[:, META_R1].astype(jnp.int32)
    counts = cnt[0, :E].astype(jnp.int32)
    tiles_per = (counts + tm_expert - 1) // tm_expert
    tile_end = jnp.cumsum(tiles_per)
    group_start = (tile_end - tiles_per) * tm_expert
    slot0 = group_start[e0] + r0
    slot1 = group_start[e1] + r1
    n_used = tile_end[-1:]
    tile_ids = jnp.arange(n_tiles, dtype=jnp.int32)
    tile_expert = jnp.searchsorted(tile_end, jnp.minimum(tile_ids, n_used[0] - 1), side="right").astype(jnp.int32)
    tile_expert = jnp.minimum(tile_expert, E - 1)
    tok = jnp.arange(T, dtype=jnp.int32)
    tok_of_slot = jnp.zeros((n_tiles * tm_expert,), jnp.int32).at[slot0].set(tok).at[slot1].set(tok)
    slot0 = slot0.reshape(T // tm_combine, 1, tm_combine)
    slot1 = slot1.reshape(T // tm_combine, 1, tm_combine)
    y = moe_expert_ffn(x, g, tok_of_slot.reshape(n_tiles, 1, tm_expert), tile_expert, n_used.astype(jnp.int32),
                       w_gate_up, w_down, layer, tm=tm_expert)
    return moe_combine_ple(x, meta, slot0, slot1, y, *ple, tm=tm_combine)


def _row_tile(S, want):
    return min(want, S)


def retention_layer(h, tabs, B, S, g_norm, w_in, gn, w_out):
    tm = _row_tile(S, 1024)
    proj = norm_matmul(h, g_norm, w_in.astype(BF16), tm=tm, tn=min(2048, w_in.shape[1]), out_dtype=BF16)
    o = retention_core(proj, tabs["ret_cos"], tabs["ret_sin"], gn, B, S)
    return matmul_res(o, w_out.astype(BF16), h, tm=_row_tile(S, 512))


def nsa_layer(h, tabs, B, S, g_norm, w_in, q_norm, k_norm, cmp_pos, cmp_w1, cmp_w2, w_out):
    D, n_in = w_in.shape
    n_pad = -(-n_in // (7 * LANES)) * (7 * LANES)
    w_in_p = jnp.zeros((D, n_pad), BF16).at[:, :n_in].set(w_in.astype(BF16))
    proj = norm_matmul(h, g_norm, w_in_p, tm=_row_tile(S, 1024), tn=7 * LANES, out_dtype=BF16)
    kn = jnp.zeros((8, NSA_DH), F32).at[:3].set(k_norm)
    w1 = cmp_w1.astype(BF16).reshape(2, NSA_CMP_LEN, NSA_DH, cmp_w1.shape[-1])
    kcc, vcc = nsa_compress(proj, cmp_pos, w1, cmp_w2.astype(BF16), kn, tabs["nsa_cos"], tabs["nsa_sin"], B, S)
    o = nsa_attention(proj, kcc, vcc, q_norm, kn, tabs["nsa_cos"], tabs["nsa_sin"], B, S)
    return matmul_res(o, w_out.astype(BF16), h, tm=_row_tile(S, 512))


def s5_layer(h, B, S, g_norm, w_in, a_re, a_im, b_re, b_im, c_re, c_im, d_skip, log_dt, w_glu):
    D = h.shape[1]
    tm = _row_tile(S, 1024)
    u = norm_matmul(h, g_norm, w_in.astype(BF16), tm=tm, tn=D, out_dtype=BF16, time_major_batch=(B, S))
    y = s5_scan(u.reshape(S * B, D), a_re, a_im, b_re, b_im, c_re, c_im, d_skip, log_dt, B, S)
    return glu_res(y.reshape(S, B * D), w_glu.astype(BF16), h, B, S, tm=_row_tile(S, 512))


def kernel(x, p, positions, norm_mix, norm_ffn, norm_ple, ret_w_in, ret_gn, ret_w_out, nsa_w_in, nsa_q_norm, nsa_k_norm, nsa_cmp_pos, nsa_cmp_w1, nsa_cmp_w2, nsa_w_out, s5_w_in, s5_a_re, s5_a_im, s5_b_re, s5_b_im, s5_c_re, s5_c_im, s5_d, s5_log_dt, s5_w_glu, ffn_w_gate_up, ffn_w_down, moe_router, moe_w_gate_up, moe_w_down, ple_w_proj, ple_w_gate):
    B, S, D = x.shape
    depth = p.shape[0]
    T = B * S
    h = x.reshape(T, D)
    pos_col = positions.reshape(T, 1)
    p_all = p.reshape(depth, T, p.shape[-1])
    moe_gu, moe_dn = moe_w_gate_up, moe_w_down

    ones = jnp.ones((1, LANES), F32)
    dk = D // RET_HEADS
    inv_ret = (ROPE_THETA ** (-jnp.arange(0, dk, 2, dtype=F32) / dk)).reshape(1, LANES)
    inv_half = ROPE_THETA ** (-jnp.arange(0, NSA_DH, 2, dtype=F32) / NSA_DH)
    inv_nsa = jnp.concatenate([inv_half, inv_half]).reshape(1, LANES)
    sign_nsa = jnp.concatenate([-jnp.ones((NSA_DH // 2,), F32), jnp.ones((NSA_DH // 2,), F32)]).reshape(1, LANES)
    tabs = {}
    tabs["ret_cos"], tabs["ret_sin"] = rope_tables(pos_col, inv_ret, ones, _row_tile(T, 1024))
    if depth > 1:
        tabs["nsa_cos"], tabs["nsa_sin"] = rope_tables(pos_col, inv_nsa, sign_nsa, _row_tile(T, 1024))

    for i in range(depth):
        m, j = i % 3, i // 3
        if m == 0:
            h = retention_layer(h, tabs, B, S, norm_mix[i], ret_w_in[j], ret_gn[j], ret_w_out[j])
        elif m == 1:
            h = nsa_layer(h, tabs, B, S, norm_mix[i], nsa_w_in[j], nsa_q_norm[j], nsa_k_norm[j],
                          nsa_cmp_pos[j], nsa_cmp_w1[j], nsa_cmp_w2[j], nsa_w_out[j])
        else:
            h = s5_layer(h, B, S, norm_mix[i], s5_w_in[j], s5_a_re[j], s5_a_im[j], s5_b_re[j], s5_b_im[j],
                         s5_c_re[j], s5_c_im[j], s5_d[j], s5_log_dt[j], s5_w_glu[j])
        tm_ffn = _row_tile(T, 1024)
        ple = (norm_ple[i], p_all, i, ple_w_gate[i].astype(BF16), ple_w_proj[i].astype(BF16))
        if i % 2 == 0:
            h = swiglu_ffn_ple(h, norm_ffn[i], ffn_w_gate_up[i // 2].astype(BF16),
                               ffn_w_down[i // 2].astype(BF16), *ple, tm=tm_ffn, tf=256)
        else:
            h = moe_layer(h, norm_ffn[i], moe_router[i // 2], moe_gu, moe_dn, i // 2, ple,
                          tm_route=tm_ffn, tm_expert=tm_ffn, tm_combine=tm_ffn)
    return h.reshape(B, S, D)
```

```python
import functools
import math

import numpy as np
import jax
import jax.numpy as jnp
from jax import lax
from jax.experimental import pallas as pl
from jax.experimental.pallas import tpu as pltpu

F32 = jnp.float32
BF16 = jnp.bfloat16

NORM_EPS = 1e-6
ROPE_THETA = 10000.0

RET_HEADS = 4
RET_CHUNK = 128
RET_BLOCK = 512

NSA_HEADS = 8
NSA_GROUPS = 2
NSA_DH = 128
NSA_CMP_LEN = 32
NSA_CMP_STRIDE = 16
NSA_SEL_LEN = 64
NSA_TOP_N = 16
NSA_WINDOW = 512
NSA_Q_TILE = 256
NSA_KV_TILE = 512

S5_GROUP = 16
S5_STATE = 64
S5_SET = 8
S5_TIME_CHUNK = 64

N_EXPERTS = 8
MOE_NF = 7
LANES = 128
MASK_NEG = -1e30
LOG2_E = 1.4426950408889634

VMEM_LIMIT = 56 * 1024 * 1024


def _params(*sem):
    return pltpu.CompilerParams(dimension_semantics=sem, vmem_limit_bytes=VMEM_LIMIT)


def _dot(a, b):
    return jnp.dot(a, b, preferred_element_type=F32)


def _dot_nt(a, b):
    return lax.dot_general(a, b, (((1,), (1,)), ((), ())), preferred_element_type=F32)


def _rms(x, g):
    return x * lax.rsqrt(jnp.mean(x * x, axis=-1, keepdims=True) + NORM_EPS) * g


def _gelu_tanh(x):
    return 0.5 * x * (1.0 + jnp.tanh(math.sqrt(2.0 / math.pi) * (x + 0.044715 * (x * x * x))))


def _rope_table_kernel(pos_ref, inv_ref, sign_ref, cos_ref, sin_ref):
    ang = pos_ref[...].astype(F32) * inv_ref[...]
    cos_ref[...] = jnp.cos(ang)
    sin_ref[...] = jnp.sin(ang) * sign_ref[...]


def rope_tables(pos_col, inv, sign, tm):
    T = pos_col.shape[0]
    return pl.pallas_call(
        _rope_table_kernel,
        grid=(T // tm,),
        in_specs=[pl.BlockSpec((tm, 1), lambda i: (i, 0)),
                  pl.BlockSpec((1, LANES), lambda i: (0, 0)),
                  pl.BlockSpec((1, LANES), lambda i: (0, 0))],
        out_specs=[pl.BlockSpec((tm, LANES), lambda i: (i, 0))] * 2,
        out_shape=[jax.ShapeDtypeStruct((T, LANES), F32)] * 2,
        compiler_params=_params("parallel"),
        name="rope_tables",
    )(pos_col, inv, sign)


def _norm_matmul_kernel(x_ref, g_ref, w_ref, o_ref, xn_ref):
    @pl.when(pl.program_id(1) == 0)
    def _():
        xn_ref[...] = _rms(x_ref[...], g_ref[...]).astype(BF16)

    o_ref[...] = _dot(xn_ref[...], w_ref[...]).astype(o_ref.dtype)


def norm_matmul(x, g, w, *, tm, tn, out_dtype, time_major_batch=None):
    T, D = x.shape
    N = w.shape[1]
    nj = N // tn
    if time_major_batch is None:
        out_shape = jax.ShapeDtypeStruct((T, N), out_dtype)
        out_spec = pl.BlockSpec((tm, tn), lambda i, j: (i, j))
    else:
        B, S = time_major_batch
        ns = S // tm
        out_shape = jax.ShapeDtypeStruct((S, B * N), out_dtype)
        out_spec = pl.BlockSpec((tm, tn), lambda i, j: (i % ns, (i // ns) * nj + j))
    return pl.pallas_call(
        _norm_matmul_kernel,
        grid=(T // tm, nj),
        in_specs=[pl.BlockSpec((tm, D), lambda i, j: (i, 0)),
                  pl.BlockSpec((1, D), lambda i, j: (0, 0)),
                  pl.BlockSpec((D, tn), lambda i, j: (0, j))],
        out_specs=out_spec,
        out_shape=out_shape,
        scratch_shapes=[pltpu.VMEM((tm, D), BF16)],
        compiler_params=_params("parallel", "arbitrary"),
        name="norm_matmul",
    )(x, g.reshape(1, D), w)


def _matmul_res_kernel(x_ref, w_ref, r_ref, o_ref):
    o_ref[...] = r_ref[...] + _dot(x_ref[...], w_ref[...])


def matmul_res(x, w, res, *, tm):
    T, K = x.shape
    N = w.shape[1]
    return pl.pallas_call(
        _matmul_res_kernel,
        grid=(T // tm,),
        in_specs=[pl.BlockSpec((tm, K), lambda i: (i, 0)),
                  pl.BlockSpec((K, N), lambda i: (0, 0)),
                  pl.BlockSpec((tm, N), lambda i: (i, 0))],
        out_specs=pl.BlockSpec((tm, N), lambda i: (i, 0)),
        out_shape=jax.ShapeDtypeStruct((T, N), F32),
        compiler_params=_params("parallel"),
        name="matmul_res",
    )(x, w, res)


def _retention_kernel(q_ref, k_ref, v_ref, g_ref, cos_ref, sin_ref, dmask_ref, qdec_ref, kdec_ref,
                      cdec_ref, gn_ref, o_ref, r_scr):
    H, dk, dv = r_scr.shape
    C = dmask_ref.shape[1]
    half = dk // 2

    @pl.when(pl.program_id(1) == 0)
    def _():
        r_scr[...] = jnp.zeros_like(r_scr)

    for c in range(q_ref.shape[0] // C):
        rows = slice(c * C, (c + 1) * C)
        cos = cos_ref[rows, :]
        sin = sin_ref[rows, :]

        def rope(x):
            x1, x2 = x[:, :half], x[:, half:]
            return jnp.concatenate([x1 * cos - x2 * sin, x2 * cos + x1 * sin], axis=-1)

        for h in range(H):
            q = rope(q_ref[rows, h * dk:(h + 1) * dk].astype(F32))
            k = rope(k_ref[rows, h * dk:(h + 1) * dk].astype(F32))
            v = v_ref[rows, h * dv:(h + 1) * dv]
            qb = q.astype(BF16)
            scores = _dot_nt(qb, k.astype(BF16)) * dmask_ref[h]
            intra = _dot(scores.astype(BF16), v)
            r_old = r_scr[h]
            cross = _dot(qb, r_old.astype(BF16)) * qdec_ref[h]
            kd_t = (k * kdec_ref[h]).T.astype(BF16)
            r_scr[h] = r_old * cdec_ref[h] + _dot(kd_t, v)

            o = intra + cross
            mu = jnp.mean(o, axis=-1, keepdims=True)
            oc = o - mu
            var = jnp.mean(oc * oc, axis=-1, keepdims=True)
            on = oc * lax.rsqrt(var + NORM_EPS) * gn_ref[:, h * dv:(h + 1) * dv]
            g = g_ref[rows, h * dv:(h + 1) * dv].astype(F32)
            o_ref[rows, h * dv:(h + 1) * dv] = (on * (g * jax.nn.sigmoid(g))).astype(o_ref.dtype)


def retention_core(proj, cos, sin, gn, B, S):
    H, C = RET_HEADS, RET_CHUNK
    T = proj.shape[0]
    dk = proj.shape[1] // (6 * H)
    dv = 2 * dk
    tb = min(RET_BLOCK, S)
    nb = S // tb
    scale = dk ** -0.5
    log_gamma = jnp.log1p(-jnp.exp2(-5.0 - jnp.arange(H, dtype=F32)))
    idx = jnp.arange(C, dtype=F32)
    diff = idx[:, None] - idx[None, :]
    dmask = jnp.where(diff >= 0, jnp.exp(jnp.maximum(diff, 0.0) * log_gamma[:, None, None]), 0.0) * scale
    qdec = jnp.exp((idx + 1.0) * log_gamma[:, None])[..., None]
    kdec = jnp.exp((C - 1.0 - idx) * log_gamma[:, None])[..., None] * scale
    cdec = jnp.exp(C * log_gamma).reshape(H, 1, 1)
    whole = lambda shape: pl.BlockSpec(shape, lambda b, n: (0,) * len(shape))
    return pl.pallas_call(
        _retention_kernel,
        grid=(B, nb),
        in_specs=[pl.BlockSpec((tb, H * dk), lambda b, n: (b * nb + n, 0)),
                  pl.BlockSpec((tb, H * dk), lambda b, n: (b * nb + n, 1)),
                  pl.BlockSpec((tb, H * dv), lambda b, n: (b * nb + n, 1)),
                  pl.BlockSpec((tb, H * dv), lambda b, n: (b * nb + n, 2)),
                  pl.BlockSpec((tb, dk // 2), lambda b, n: (b * nb + n, 0)),
                  pl.BlockSpec((tb, dk // 2), lambda b, n: (b * nb + n, 0)),
                  whole((H, C, C)), whole((H, C, 1)), whole((H, C, 1)), whole((H, 1, 1)),
                  whole((1, H * dv))],
        out_specs=pl.BlockSpec((tb, H * dv), lambda b, n: (b * nb + n, 0)),
        out_shape=jax.ShapeDtypeStruct((T, H * dv), BF16),
        scratch_shapes=[pltpu.VMEM((H, dk, dv), F32)],
        compiler_params=_params("parallel", "arbitrary"),
        name="retention",
    )(proj, proj, proj, proj, cos, sin, dmask, qdec, kdec, cdec, gn.reshape(1, H * dv))


def _nsa_compress_kernel(kc_ref, vc_ref, pe_ref, w1_ref, w2_ref, kn_ref, cos_ref, sin_ref,
                         kco_ref, vco_ref, xs_scr):
    nh = NSA_CMP_LEN // NSA_CMP_STRIDE
    n_rows = xs_scr.shape[0] // NSA_CMP_STRIDE
    for br, (src, dst) in enumerate(((kc_ref, kco_ref), (vc_ref, vco_ref))):
        xs_scr[...] = src[...].astype(F32)
        acc = [jnp.zeros((n_rows, w1_ref.shape[-1]), F32) for _ in range(nh)]
        for l in range(NSA_CMP_STRIDE):
            piece = xs_scr[pl.ds(l, n_rows, stride=NSA_CMP_STRIDE), :]
            for a in range(nh):
                ll = a * NSA_CMP_STRIDE + l
                acc[a] = acc[a] + _dot((piece + pe_ref[br, ll:ll + 1, :]).astype(BF16), w1_ref[br, ll])
        hid = acc[0] + pltpu.roll(acc[1], n_rows - 1, axis=0)
        z = _dot(_gelu_tanh(hid).astype(BF16), w2_ref[br])
        if br == 0:
            z = _rms(z, kn_ref[0:1, :])
            cos = cos_ref[pl.ds(0, n_rows, stride=NSA_CMP_STRIDE), :]
            sin = sin_ref[pl.ds(0, n_rows, stride=NSA_CMP_STRIDE), :]
            z = z * cos + pltpu.roll(z, NSA_DH // 2, axis=1) * sin
        dst[0, 0] = z.astype(dst.dtype)


def nsa_compress(proj, pe, w1, w2, kn, cosn, sinn, B, S):
    G, dh = NSA_GROUPS, NSA_DH
    nr = S // NSA_CMP_STRIDE
    kc0 = NSA_HEADS
    vc0 = NSA_HEADS + G
    return pl.pallas_call(
        _nsa_compress_kernel,
        grid=(B, G),
        in_specs=[pl.BlockSpec((S, dh), lambda b, g: (b, kc0 + g)),
                  pl.BlockSpec((S, dh), lambda b, g: (b, vc0 + g)),
                  pl.BlockSpec(pe.shape, lambda b, g: (0, 0, 0)),
                  pl.BlockSpec(w1.shape, lambda b, g: (0, 0, 0, 0)),
                  pl.BlockSpec(w2.shape, lambda b, g: (0, 0, 0)),
                  pl.BlockSpec(kn.shape, lambda b, g: (0, 0)),
                  pl.BlockSpec((S, dh), lambda b, g: (b, 0)),
                  pl.BlockSpec((S, dh), lambda b, g: (b, 0))],
        out_specs=[pl.BlockSpec((1, 1, nr, dh), lambda b, g: (b, g, 0, 0))] * 2,
        out_shape=[jax.ShapeDtypeStruct((B, G, nr, dh), BF16)] * 2,
        scratch_shapes=[pltpu.VMEM((S, dh), F32)],
        compiler_params=_params("parallel", "parallel"),
        name="nsa_compress",
    )(proj, proj, pe, w1, w2, kn, cosn, sinn)


def _softmax2_parts(s, bias, hg):
    tq, n = bias.shape
    s3 = s.reshape(hg, tq, n) + bias[None]
    p = jnp.exp2(s3 - jnp.max(s3, axis=-1, keepdims=True))
    return p.reshape(hg * tq, n), jnp.sum(p, axis=-1, keepdims=True).reshape(hg * tq, 1)


def _nsa_attn_kernel(q_ref, gate_ref, cosq_ref, sinq_ref, qn_ref, kc_ref, vc_ref,
                     ks_ref, vs_ref, kw_ref, vw_ref, kn_ref, cosk_ref, sink_ref, mmat_ref,
                     o_ref, ks_scr, kw_scr):
    grp = pl.program_id(1)
    qi = pl.program_id(2)
    tq = q_ref.shape[0]
    dh = NSA_DH
    hg = q_ref.shape[1] // dh
    S = ks_ref.shape[0]
    tk = min(NSA_KV_TILE, S)
    q0 = qi * tq

    @pl.when(qi == 0)
    def _():
        cosk = cosk_ref[...]
        sink = sink_ref[...]
        for src, dst, r in ((ks_ref, ks_scr, 1), (kw_ref, kw_scr, 2)):
            x = _rms(src[...].astype(F32), kn_ref[r:r + 1, :])
            dst[...] = (x * cosk + pltpu.roll(x, dh // 2, axis=1) * sink).astype(BF16)

    cosq = cosq_ref[...]
    sinq = sinq_ref[...]
    qs = []
    for h in range(hg):
        x = _rms(q_ref[:, h * dh:(h + 1) * dh].astype(F32), qn_ref[...])
        x = (x * cosq + pltpu.roll(x, dh // 2, axis=1) * sinq) * (dh ** -0.5 * LOG2_E)
        qs.append(x.astype(BF16))
    qall = jnp.concatenate(qs, axis=0)

    def tok(n):
        return q0 + lax.broadcasted_iota(jnp.int32, (tq, n), 0)

    def key(n):
        return lax.broadcasted_iota(jnp.int32, (tq, n), 1)

    ncp = kc_ref.shape[2]
    valid_c = key(ncp) * NSA_CMP_STRIDE + (NSA_CMP_LEN - 1) <= tok(ncp)
    p_c, l_c = _softmax2_parts(_dot_nt(qall, kc_ref[0, 0]), jnp.where(valid_c, 0.0, MASK_NEG), hg)
    t_row = q0 + jnp.bitwise_and(lax.broadcasted_iota(jnp.int32, (hg * tq, 1), 0), tq - 1)
    p_c = jnp.where(t_row >= NSA_CMP_LEN - 1, p_c / l_c, 0.0)
    o_c = _dot(p_c.astype(BF16), vc_ref[0, 0])
    p_sum = p_c[0:tq]
    for h in range(1, hg):
        p_sum = p_sum + p_c[h * tq:(h + 1) * tq]
    imp = jnp.dot(p_sum, mmat_ref[...], preferred_element_type=F32, precision=lax.Precision.HIGHEST)

    n_sel = S // NSA_SEL_LEN
    sel_rows = ((n_sel + 7) // 8) * 8
    v_imp = imp.T[0:sel_rows, :]
    jb = lax.broadcasted_iota(jnp.int32, (sel_rows, tq), 0)
    cur = (q0 + lax.broadcasted_iota(jnp.int32, (sel_rows, tq), 1)) // NSA_SEL_LEN
    forced = (jb == 0) | (jb == cur) | (jb == cur - 1)
    v_imp = jnp.where(forced, jnp.inf, jnp.where(jb > cur, -jnp.inf, v_imp))
    if sel_rows > n_sel:
        v_imp = jnp.where(jb >= n_sel, -jnp.inf, v_imp)
    rank = jnp.zeros((sel_rows, tq), jnp.int32)
    for i in range(n_sel):
        r = v_imp[i:i + 1, :]
        beats = (r > v_imp) | ((r == v_imp) & (jb > i))
        rank = rank + beats.astype(jnp.int32)
    sel_t = ((rank < min(NSA_TOP_N, n_sel)) & (jb < n_sel)).astype(F32)
    if sel_rows < LANES:
        sel_t = jnp.concatenate([sel_t, jnp.zeros((LANES - sel_rows, tq), F32)], axis=0)
    sel = sel_t.T.astype(BF16)

    row_t = tok(tk)
    lane_k = key(tk)
    e_row = lax.broadcasted_iota(jnp.int32, (LANES, tk), 0)
    e_blk = lax.broadcasted_iota(jnp.int32, (LANES, tk), 1) // NSA_SEL_LEN

    def sel_step(j, carry):
        m, l, acc = carry
        base = pl.multiple_of(j * tk, tk)
        s = _dot_nt(qall, ks_scr[pl.ds(base, tk), :])
        expand = (e_blk + j * (tk // NSA_SEL_LEN) == e_row).astype(BF16)
        chosen = _dot(sel, expand)
        bias = jnp.where((chosen > 0.5) & (lane_k + base <= row_t), 0.0, MASK_NEG)
        s3 = s.reshape(hg, tq, tk) + bias[None]
        m_new = jnp.maximum(m, jnp.max(s3, axis=-1, keepdims=True))
        alpha = jnp.exp2(m - m_new)
        p = jnp.exp2(s3 - m_new)
        l = alpha * l + jnp.sum(p, axis=-1, keepdims=True)
        pv = _dot(p.reshape(hg * tq, tk).astype(BF16), vs_ref[pl.ds(base, tk), :])
        acc = alpha.reshape(hg * tq, 1) * acc + pv
        return m_new, l, acc

    n_kv = (q0 + tq + tk - 1) // tk
    init = (jnp.full((hg, tq, 1), MASK_NEG, F32), jnp.zeros((hg, tq, 1), F32), jnp.zeros((hg * tq, dh), F32))
    _, l_s, acc_s = lax.fori_loop(0, n_kv, sel_step, init)
    o_s = acc_s / l_s.reshape(hg * tq, 1)

    span = min(tq + NSA_WINDOW, S)
    start = pl.multiple_of(jnp.maximum(q0 + tq - span, 0), tq)
    dist = tok(span) - (start + key(span))
    bias_w = jnp.where((dist >= 0) & (dist < NSA_WINDOW), 0.0, MASK_NEG)
    p_w, l_w = _softmax2_parts(_dot_nt(qall, kw_scr[pl.ds(start, span), :]), bias_w, hg)
    o_w = _dot(p_w.astype(BF16), vw_ref[pl.ds(start, span), :]) / l_w

    gsig = jax.nn.sigmoid(gate_ref[...].astype(F32))
    n_br = 3
    for h in range(hg):
        def gate_col(br):
            lo = h * n_br + br
            hi = lo + hg * n_br
            return jnp.where(grp == 0, gsig[:, lo:lo + 1], gsig[:, hi:hi + 1])
        rows = slice(h * tq, (h + 1) * tq)
        o = gate_col(0) * o_c[rows] + gate_col(1) * o_s[rows] + gate_col(2) * o_w[rows]
        o_ref[:, h * dh:(h + 1) * dh] = o.astype(o_ref.dtype)


def nsa_attention(proj, kcc, vcc, qn, kn, cosn, sinn, B, S):
    H, G, dh = NSA_HEADS, NSA_GROUPS, NSA_DH
    hg = H // G
    T = proj.shape[0]
    tq = min(NSA_Q_TILE, S)
    nq = S // tq
    nr = kcc.shape[2]
    n_c = (S - NSA_CMP_LEN) // NSA_CMP_STRIDE + 1
    n_sel = S // NSA_SEL_LEN
    cs = (np.arange(n_c) * NSA_CMP_STRIDE)[:, None]
    js = (np.arange(n_sel) * NSA_SEL_LEN)[None, :]
    overlap = np.clip(np.minimum(cs + NSA_CMP_LEN, js + NSA_SEL_LEN) - np.maximum(cs, js), 0, None) / NSA_CMP_LEN
    mmat = np.zeros((nr, LANES), np.float32)
    mmat[:n_c, :n_sel] = overlap
    col = lambda base: (lambda b, g, i: (b, base + g))
    return pl.pallas_call(
        _nsa_attn_kernel,
        grid=(B, G, nq),
        in_specs=[pl.BlockSpec((tq, hg * dh), lambda b, g, i: (b * nq + i, g)),
                  pl.BlockSpec((tq, LANES), lambda b, g, i: (b * nq + i, H + 6 * G)),
                  pl.BlockSpec((tq, dh), lambda b, g, i: (b * nq + i, 0)),
                  pl.BlockSpec((tq, dh), lambda b, g, i: (b * nq + i, 0)),
                  pl.BlockSpec((1, dh), lambda b, g, i: (0, 0)),
                  pl.BlockSpec((1, 1, nr, dh), lambda b, g, i: (b, g, 0, 0)),
                  pl.BlockSpec((1, 1, nr, dh), lambda b, g, i: (b, g, 0, 0)),
                  pl.BlockSpec((S, dh), col(H + 2 * G)),
                  pl.BlockSpec((S, dh), col(H + 3 * G)),
                  pl.BlockSpec((S, dh), col(H + 4 * G)),
                  pl.BlockSpec((S, dh), col(H + 5 * G)),
                  pl.BlockSpec(kn.shape, lambda b, g, i: (0, 0)),
                  pl.BlockSpec((S, dh), lambda b, g, i: (b, 0)),
                  pl.BlockSpec((S, dh), lambda b, g, i: (b, 0)),
                  pl.BlockSpec(mmat.shape, lambda b, g, i: (0, 0))],
        out_specs=pl.BlockSpec((tq, hg * dh), lambda b, g, i: (b * nq + i, g)),
        out_shape=jax.ShapeDtypeStruct((T, H * dh), BF16),
        scratch_shapes=[pltpu.VMEM((S, dh), BF16), pltpu.VMEM((S, dh), BF16)],
        compiler_params=_params("parallel", "parallel", "arbitrary"),
        name="nsa_attention",
    )(proj, proj, cosn, sinn, qn.reshape(1, dh), kcc, vcc, proj, proj, proj, proj, kn, cosn, sinn,
      jnp.asarray(mmat))


def _s5_scan_kernel(u_ref, bbd_ref, are_ref, aim_ref, cbd_ref, d_ref, y_ref, bu_scr, xs_scr, st_scr):
    nb = st_scr.shape[0] // 2
    w = are_ref.shape[-1]
    tc = u_ref.shape[0] // nb

    @pl.when(pl.program_id(1) == 0)
    def _():
        st_scr[...] = jnp.zeros_like(st_scr)

    u = u_ref[...]
    bu_scr[...] = _dot(u, bbd_ref[0])
    a_re = jnp.broadcast_to(are_ref[0], (nb, w))
    a_im = jnp.broadcast_to(aim_ref[0], (nb, w))

    def step(t, carry):
        xr, xi = carry
        r0 = pl.multiple_of(t * nb, nb)
        nxr = a_re * xr - a_im * xi + bu_scr[pl.ds(r0, nb), 0:w]
        nxi = a_re * xi + a_im * xr + bu_scr[pl.ds(r0, nb), w:2 * w]
        xs_scr[pl.ds(r0, nb), 0:w] = nxr
        xs_scr[pl.ds(r0, nb), w:2 * w] = nxi
        return nxr, nxi

    xr, xi = lax.fori_loop(0, tc, step, (st_scr[0:nb, :], st_scr[nb:2 * nb, :]), unroll=4)
    st_scr[0:nb, :] = xr
    st_scr[nb:2 * nb, :] = xi
    y = _dot(xs_scr[...].astype(BF16), cbd_ref[0]) + d_ref[...] * u.astype(F32)
    y_ref[...] = _gelu_tanh(y).astype(y_ref.dtype)


def s5_scan(u_tm, a_re, a_im, b_re, b_im, c_re, c_im, d_skip, log_dt, B, S):
    D = u_tm.shape[1]
    Cg, P, NS = S5_GROUP, S5_STATE, S5_SET
    G = D // Cg
    K = G // NS
    w = NS * P
    tc = min(S5_TIME_CHUNK, S)
    dt = jnp.exp(log_dt.astype(F32))[:, None]
    mag = jnp.exp(dt * a_re)
    abar_re, abar_im = mag * jnp.cos(dt * a_im), mag * jnp.sin(dt * a_im)
    den = a_re * a_re + a_im * a_im
    nr_, ni_ = abar_re - 1.0, abar_im
    f_re = (nr_ * a_re + ni_ * a_im) / den
    f_im = (ni_ * a_re - nr_ * a_im) / den
    bbar_re = f_re[..., None] * b_re - f_im[..., None] * b_im
    bbar_im = f_re[..., None] * b_im + f_im[..., None] * b_re
    eye = jnp.eye(NS, dtype=F32)

    def in_blockdiag(bb):
        t = bb.reshape(K, NS, P, Cg).transpose(0, 1, 3, 2)
        return jnp.einsum('kgcp,gh->kgchp', t, eye).reshape(K, NS * Cg, NS * P)

    def out_blockdiag(cc):
        t = cc.reshape(K, NS, Cg, P).transpose(0, 1, 3, 2)
        return jnp.einsum('kgpc,gh->kgphc', t, eye).reshape(K, NS * P, NS * Cg)

    bbd = jnp.concatenate([in_blockdiag(bbar_re), in_blockdiag(bbar_im)], axis=-1).astype(BF16)
    cbd = jnp.concatenate([out_blockdiag(c_re), -out_blockdiag(c_im)], axis=1).astype(BF16)
    are = abar_re.reshape(K, 1, w)
    aim = abar_im.reshape(K, 1, w)
    rows = tc * B
    return pl.pallas_call(
        _s5_scan_kernel,
        grid=(K, S // tc),
        in_specs=[pl.BlockSpec((rows, LANES), lambda k, c: (c, k)),
                  pl.BlockSpec((1, LANES, 2 * w), lambda k, c: (k, 0, 0)),
                  pl.BlockSpec((1, 1, w), lambda k, c: (k, 0, 0)),
                  pl.BlockSpec((1, 1, w), lambda k, c: (k, 0, 0)),
                  pl.BlockSpec((1, 2 * w, LANES), lambda k, c: (k, 0, 0)),
                  pl.BlockSpec((1, LANES), lambda k, c: (0, k))],
        out_specs=pl.BlockSpec((rows, LANES), lambda k, c: (c, k)),
        out_shape=jax.ShapeDtypeStruct((S * B, D), BF16),
        scratch_shapes=[pltpu.VMEM((rows, 2 * w), F32), pltpu.VMEM((rows, 2 * w), F32),
                        pltpu.VMEM((2 * B, w), F32)],
        compiler_params=_params("parallel", "arbitrary"),
        name="s5_scan",
    )(u_tm, bbd, are, aim, cbd, d_skip.reshape(1, D))


def _glu_res_kernel(y_ref, wa_ref, wb_ref, r_ref, o_ref):
    y = y_ref[...]
    a = _dot(y, wa_ref[...])
    b = _dot(y, wb_ref[...])
    o_ref[...] = r_ref[...] + a * jax.nn.sigmoid(b)


def glu_res(y_tm, w, res, B, S, *, tm):
    D = w.shape[0]
    N = w.shape[1] // 2
    ns = S // tm
    return pl.pallas_call(
        _glu_res_kernel,
        grid=(B * ns,),
        in_specs=[pl.BlockSpec((tm, D), lambda i: (i % ns, i // ns)),
                  pl.BlockSpec((D, N), lambda i: (0, 0)),
                  pl.BlockSpec((D, N), lambda i: (0, 1)),
                  pl.BlockSpec((tm, N), lambda i: (i, 0))],
        out_specs=pl.BlockSpec((tm, N), lambda i: (i, 0)),
        out_shape=jax.ShapeDtypeStruct((B * S, N), F32),
        compiler_params=_params("parallel"),
        name="glu_res",
    )(y_tm, w, w, res)


def _ple(h, g_ref, p_ref, wgate_ref, wproj_ref):
    gate = jax.nn.sigmoid(_dot(_rms(h, g_ref[...]).astype(BF16), wgate_ref[...]))
    return h + gate * _dot(p_ref[...].astype(BF16), wproj_ref[...])


def _ffn_kernel(x_ref, g_ref, wg_ref, wu_ref, wd_ref, gp_ref, p_ref, wgate_ref, wproj_ref, o_ref,
                xn_scr, acc_scr):
    f = pl.program_id(1)

    @pl.when(f == 0)
    def _():
        xn_scr[...] = _rms(x_ref[...], g_ref[...]).astype(BF16)
        acc_scr[...] = jnp.zeros_like(acc_scr)

    xn = xn_scr[...]
    gg = _dot(xn, wg_ref[...])
    uu = _dot(xn, wu_ref[...])
    acc_scr[...] += _dot((gg * jax.nn.sigmoid(gg) * uu).astype(BF16), wd_ref[...])

    @pl.when(f == pl.num_programs(1) - 1)
    def _():
        o_ref[...] = _ple(x_ref[...] + acc_scr[...], gp_ref, p_ref, wgate_ref, wproj_ref)


def swiglu_ffn_ple(x, g, w_gate_up, w_down, g_ple, p_all, layer, w_pgate, w_pproj, *, tm, tf):
    T, D = x.shape
    F = w_down.shape[0]
    PD = p_all.shape[2]
    nf = F // tf
    return pl.pallas_call(
        _ffn_kernel,
        grid=(T // tm, nf),
        in_specs=[pl.BlockSpec((tm, D), lambda i, f: (i, 0)),
                  pl.BlockSpec((1, D), lambda i, f: (0, 0)),
                  pl.BlockSpec((D, tf), lambda i, f: (0, f)),
                  pl.BlockSpec((D, tf), lambda i, f: (0, nf + f)),
                  pl.BlockSpec((tf, D), lambda i, f: (f, 0)),
                  pl.BlockSpec((1, D), lambda i, f: (0, 0)),
                  pl.BlockSpec((None, tm, PD), lambda i, f: (layer, i, 0)),
                  pl.BlockSpec((D, D), lambda i, f: (0, 0)),
                  pl.BlockSpec((PD, D), lambda i, f: (0, 0))],
        out_specs=pl.BlockSpec((tm, D), lambda i, f: (i, 0)),
        out_shape=jax.ShapeDtypeStruct((T, D), F32),
        scratch_shapes=[pltpu.VMEM((tm, D), BF16), pltpu.VMEM((tm, D), F32)],
        compiler_params=_params("parallel", "arbitrary"),
        name="swiglu_ffn_ple",
    )(x, g.reshape(1, D), w_gate_up, w_gate_up, w_down, g_ple.reshape(1, D), p_all, w_pgate, w_pproj)


META_E0, META_E1, META_W0, META_W1, META_R0, META_R1 = range(6)


def _router_kernel(x_ref, g_ref, w_ref, meta_ref, cnt_ref, carry_scr, tri_scr):
    @pl.when(pl.program_id(0) == 0)
    def _():
        carry_scr[...] = jnp.zeros_like(carry_scr)
        n = tri_scr.shape[0]
        tri = lax.broadcasted_iota(jnp.int32, (n, n), 0) > lax.broadcasted_iota(jnp.int32, (n, n), 1)
        tri_scr[...] = tri.astype(BF16)

    xn = _rms(x_ref[...], g_ref[...])
    x_hi = xn.astype(BF16)
    x_lo = (xn - x_hi.astype(F32)).astype(BF16)
    w = w_ref[...]
    w_hi = w.astype(BF16)
    w_lo = (w - w_hi.astype(F32)).astype(BF16)
    logits = _dot(x_hi, w_hi) + (_dot(x_hi, w_lo) + _dot(x_lo, w_hi))
    lane = lax.broadcasted_iota(jnp.int32, logits.shape, 1).astype(F32)
    logits = jnp.where(lane < N_EXPERTS, logits, -jnp.inf)

    def top1(v):
        m = jnp.max(v, axis=-1, keepdims=True)
        idx = jnp.min(jnp.where(v == m, lane, float(LANES)), axis=-1, keepdims=True)
        return m, idx, lane == idx

    m1, e0, hot1 = top1(logits)
    m2, e1, hot2 = top1(jnp.where(hot1, -jnp.inf, logits))
    ex = jnp.exp(m2 - m1)
    w0 = 1.0 / (1.0 + ex)
    w1 = ex * w0

    both = jnp.where(hot1 | hot2, 1.0, 0.0)
    before = _dot(tri_scr[...], both.astype(BF16)) + carry_scr[0:1, :]
    r0 = jnp.sum(jnp.where(hot1, before, 0.0), axis=-1, keepdims=True)
    r1 = jnp.sum(jnp.where(hot2, before, 0.0), axis=-1, keepdims=True)
    carry_scr[...] = carry_scr[...] + jnp.sum(both, axis=0, keepdims=True)
    cnt_ref[...] = carry_scr[...]

    meta = jnp.zeros_like(logits)
    for k, v in ((META_E0, e0), (META_E1, e1), (META_W0, w0), (META_W1, w1), (META_R0, r0), (META_R1, r1)):
        meta = jnp.where(lane == float(k), v, meta)
    meta_ref[...] = meta


def router_topk(x, g, w_router, *, tm):
    T, D = x.shape
    w = jnp.zeros((D, LANES), F32).at[:, :N_EXPERTS].set(w_router)
    return pl.pallas_call(
        _router_kernel,
        grid=(T // tm,),
        in_specs=[pl.BlockSpec((tm, D), lambda i: (i, 0)),
                  pl.BlockSpec((1, D), lambda i: (0, 0)),
                  pl.BlockSpec((D, LANES), lambda i: (0, 0))],
        out_specs=[pl.BlockSpec((tm, LANES), lambda i: (i, 0)),
                   pl.BlockSpec((8, LANES), lambda i: (0, 0))],
        out_shape=[jax.ShapeDtypeStruct((T, LANES), F32), jax.ShapeDtypeStruct((8, LANES), F32)],
        scratch_shapes=[pltpu.VMEM((8, LANES), F32), pltpu.VMEM((tm, tm), BF16)],
        compiler_params=_params("arbitrary"),
        name="router",
    )(x, g.reshape(1, D), w)


def _start_row_gathers(idx_ref, src_hbm, dst_ref, sem, lo, hi, idx_off=0, priority=0):
    for r in range(lo, hi):
        pltpu.make_async_copy(src_hbm.at[pl.ds(idx_ref[0, 0, idx_off + r], 1), :], dst_ref.at[pl.ds(r, 1), :],
                              sem).start(priority=priority)


def _wait_row_gathers(src_hbm, dst_ref, sem):
    pltpu.make_async_copy(src_hbm.at[pl.ds(0, dst_ref.shape[0]), :], dst_ref, sem).wait()


def _moe_expert_kernel(te_ref, nu_ref, tok_ref, tok_next_ref, x_hbm, g_ref, wg_ref, wu_ref, wd_ref, y_ref,
                       xg_scr, xn_scr, acc_scr, sems):
    i = pl.program_id(0)
    f = pl.program_id(1)
    n_tiles = pl.num_programs(0)
    nf = pl.num_programs(1)
    n_used = nu_ref[0]
    active = i < n_used
    tm = xn_scr.shape[0]
    cur = xg_scr.at[i % 2]
    nxt = xg_scr.at[(i + 1) % 2]
    sem_cur = sems.at[i % 2]
    sem_nxt = sems.at[(i + 1) % 2]

    @pl.when((i == 0) & (f == 0))
    def _():
        def issue(r, carry):
            pltpu.make_async_copy(x_hbm.at[pl.ds(tok_ref[0, 0, r], 1), :], cur.at[pl.ds(r, 1), :], sem_cur).start()
            return carry
        lax.fori_loop(0, tm, issue, 0, unroll=8)

    @pl.when((f == 0) & (i <= n_used))
    def _():
        _wait_row_gathers(x_hbm, cur, sem_cur)

    @pl.when(active & (f == 0))
    def _():
        xn_scr[...] = _rms(cur[...], g_ref[...]).astype(BF16)
        acc_scr[...] = jnp.zeros_like(acc_scr)

    def step(lo, hi):
        xn = xn_scr[...]
        gg = _dot(xn, wg_ref[0].astype(BF16))
        uu = _dot(xn, wu_ref[0].astype(BF16))
        acc_scr[...] += _dot((gg * jax.nn.sigmoid(gg) * uu).astype(BF16), wd_ref[0].astype(BF16))
        _start_row_gathers(tok_next_ref, x_hbm, nxt, sem_nxt, lo, hi)

    per = -(-tm // MOE_NF)
    for fs in range(MOE_NF):
        @pl.when(active & (f == fs))
        def _(fs=fs):
            step(min(fs * per, tm), min((fs + 1) * per, tm))

    @pl.when(f == nf - 1)
    def _():
        y_ref[...] = jnp.where(active, acc_scr[...], 0.0)

    @pl.when(active & (i == n_tiles - 1) & (f == nf - 1))
    def _():
        _wait_row_gathers(x_hbm, nxt, sem_nxt)


def moe_expert_ffn(x, g, tok_of_slot, tile_expert, n_used, w_gate_up, w_down, layer, *, tm):
    T, D = x.shape
    _, E, F, _ = w_down.shape
    nf = MOE_NF
    tf = F // nf
    n_tiles = tok_of_slot.shape[0]

    def fblk(i, f, nu):
        return jnp.where(i < nu[0], f, nf - 1)

    tok_spec = lambda off: pl.BlockSpec((1, 1, tm), lambda i, f, te, nu: (jnp.minimum(i + off, n_tiles - 1), 0, 0),
                                        memory_space=pltpu.SMEM)
    grid_spec = pltpu.PrefetchScalarGridSpec(
        num_scalar_prefetch=2,
        grid=(n_tiles, nf),
        in_specs=[tok_spec(0), tok_spec(1),
                  pl.BlockSpec(memory_space=pl.ANY),
                  pl.BlockSpec((1, D), lambda i, f, te, nu: (0, 0)),
                  pl.BlockSpec((None, 1, D, tf), lambda i, f, te, nu: (layer, te[i], 0, fblk(i, f, nu))),
                  pl.BlockSpec((None, 1, D, tf), lambda i, f, te, nu: (layer, te[i], 0, nf + fblk(i, f, nu))),
                  pl.BlockSpec((None, 1, tf, D), lambda i, f, te, nu: (layer, te[i], fblk(i, f, nu), 0))],
        out_specs=pl.BlockSpec((tm, D), lambda i, f, te, nu: (i, 0)),
        scratch_shapes=[pltpu.VMEM((2, tm, D), F32), pltpu.VMEM((tm, D), BF16), pltpu.VMEM((tm, D), F32),
                        pltpu.SemaphoreType.DMA((2,))],
    )
    return pl.pallas_call(
        _moe_expert_kernel,
        grid_spec=grid_spec,
        out_shape=jax.ShapeDtypeStruct((n_tiles * tm, D), F32),
        compiler_params=_params("arbitrary", "arbitrary"),
        name="moe_experts",
    )(tile_expert, n_used, tok_of_slot, tok_of_slot, x, g.reshape(1, D), w_gate_up, w_gate_up, w_down)


def _moe_combine_kernel(s0_ref, s1_ref, s0n_ref, s1n_ref, x_ref, meta_ref, y_hbm, gp_ref, p_ref, wgate_ref,
                        wproj_ref, o_ref, ya0, ya1, yb0, yb1, sems):
    i = pl.program_id(0)
    n = ya0.shape[0]

    def gather(idx0, idx1, buf0, buf1, k, off):
        _start_row_gathers(idx0, y_hbm, buf0, sems.at[k, 0], 0, n, off, priority=0)
        _start_row_gathers(idx1, y_hbm, buf1, sems.at[k, 1], 0, n, off, priority=1)

    def wait(buf0, buf1, k):
        _wait_row_gathers(y_hbm, buf0, sems.at[k, 0])
        _wait_row_gathers(y_hbm, buf1, sems.at[k, 1])

    def combine(buf0, buf1, rows):
        meta = meta_ref[rows, :]
        w0 = meta[:, META_W0:META_W0 + 1]
        w1 = meta[:, META_W1:META_W1 + 1]
        h = x_ref[rows, :] + w0 * buf0[...] + w1 * buf1[...]
        gate = jax.nn.sigmoid(_dot(_rms(h, gp_ref[...]).astype(BF16), wgate_ref[...]))
        o_ref[rows, :] = h + gate * _dot(p_ref[rows, :].astype(BF16), wproj_ref[...])

    @pl.when(i == 0)
    def _():
        gather(s0_ref, s1_ref, ya0, ya1, 0, 0)

    wait(ya0, ya1, 0)
    gather(s0_ref, s1_ref, yb0, yb1, 1, n)
    combine(ya0, ya1, slice(0, n))
    wait(yb0, yb1, 1)
    gather(s0n_ref, s1n_ref, ya0, ya1, 0, 0)
    combine(yb0, yb1, slice(n, 2 * n))

    @pl.when(i == pl.num_programs(0) - 1)
    def _():
        wait(ya0, ya1, 0)


def moe_combine_ple(x, meta, slot0, slot1, y, g_ple, p_all, layer, w_pgate, w_pproj, *, tm):
    T, D = x.shape
    PD = p_all.shape[2]
    nt = T // tm
    idx = lambda off: pl.BlockSpec((1, 1, tm), lambda i: (jnp.minimum(i + off, nt - 1), 0, 0),
                                   memory_space=pltpu.SMEM)
    return pl.pallas_call(
        _moe_combine_kernel,
        grid=(nt,),
        in_specs=[idx(0), idx(0), idx(1), idx(1),
                  pl.BlockSpec((tm, D), lambda i: (i, 0)),
                  pl.BlockSpec((tm, LANES), lambda i: (i, 0)),
                  pl.BlockSpec(memory_space=pl.ANY),
                  pl.BlockSpec((1, D), lambda i: (0, 0)),
                  pl.BlockSpec((None, tm, PD), lambda i: (layer, i, 0)),
                  pl.BlockSpec((D, D), lambda i: (0, 0)),
                  pl.BlockSpec((PD, D), lambda i: (0, 0))],
        out_specs=pl.BlockSpec((tm, D), lambda i: (i, 0)),
        out_shape=jax.ShapeDtypeStruct((T, D), F32),
        scratch_shapes=[pltpu.VMEM((tm // 2, D), F32)] * 4 + [pltpu.SemaphoreType.DMA((2, 2))],
        compiler_params=_params("arbitrary"),
        name="moe_combine_ple",
    )(slot0, slot1, slot0, slot1, x, meta, y, g_ple.reshape(1, D), p_all, w_pgate, w_pproj)


def moe_layer(x, g, w_router, w_gate_up, w_down, layer, ple, *, tm_route, tm_expert, tm_combine):
    T, D = x.shape
    E = w_down.shape[1]
    n_tiles = (2 * T) // tm_expert + E
    meta, cnt = router_topk(x, g, w_router, tm=tm_route)
    e0 = meta[:, META_E0].astype(jnp.int32)
    e1 = meta[:, META_E1].astype(jnp.int32)
    r0 = meta[:, META_R0].astype(jnp.int32)
    r1 = meta[:, META_R1].astype(jnp.int32)
    counts = cnt[0, :E].astype(jnp.int32)
    tiles_per = (counts + tm_expert - 1) // tm_expert
    tile_end = jnp.cumsum(tiles_per)
    group_start = (tile_end - tiles_per) * tm_expert
    slot0 = group_start[e0] + r0
    slot1 = group_start[e1] + r1
    n_used = tile_end[-1:]
    tile_ids = jnp.arange(n_tiles, dtype=jnp.int32)
    tile_expert = jnp.searchsorted(tile_end, jnp.minimum(tile_ids, n_used[0] - 1), side="right").astype(jnp.int32)
    tile_expert = jnp.minimum(tile_expert, E - 1)
    tok = jnp.arange(T, dtype=jnp.int32)
    tok_of_slot = jnp.zeros((n_tiles * tm_expert,), jnp.int32).at[slot0].set(tok).at[slot1].set(tok)
    slot0 = slot0.reshape(T // tm_combine, 1, tm_combine)
    slot1 = slot1.reshape(T // tm_combine, 1, tm_combine)
    y = moe_expert_ffn(x, g, tok_of_slot.reshape(n_tiles, 1, tm_expert), tile_expert, n_used.astype(jnp.int32),
                       w_gate_up, w_down, layer, tm=tm_expert)
    return moe_combine_ple(x, meta, slot0, slot1, y, *ple, tm=tm_combine)


def _row_tile(S, want):
    return min(want, S)


def retention_layer(h, tabs, B, S, g_norm, w_in, gn, w_out):
    tm = _row_tile(S, 1024)
    proj = norm_matmul(h, g_norm, w_in.astype(BF16), tm=tm, tn=min(2048, w_in.shape[1]), out_dtype=BF16)
    o = retention_core(proj, tabs["ret_cos"], tabs["ret_sin"], gn, B, S)
    return matmul_res(o, w_out.astype(BF16), h, tm=_row_tile(S, 512))


def nsa_layer(h, tabs, B, S, g_norm, w_in, q_norm, k_norm, cmp_pos, cmp_w1, cmp_w2, w_out):
    D, n_in = w_in.shape
    n_pad = -(-n_in // (7 * LANES)) * (7 * LANES)
    w_in_p = jnp.zeros((D, n_pad), BF16).at[:, :n_in].set(w_in.astype(BF16))
    proj = norm_matmul(h, g_norm, w_in_p, tm=_row_tile(S, 1024), tn=7 * LANES, out_dtype=BF16)
    kn = jnp.zeros((8, NSA_DH), F32).at[:3].set(k_norm)
    w1 = cmp_w1.astype(BF16).reshape(2, NSA_CMP_LEN, NSA_DH, cmp_w1.shape[-1])
    kcc, vcc = nsa_compress(proj, cmp_pos, w1, cmp_w2.astype(BF16), kn, tabs["nsa_cos"], tabs["nsa_sin"], B, S)
    o = nsa_attention(proj, kcc, vcc, q_norm, kn, tabs["nsa_cos"], tabs["nsa_sin"], B, S)
    return matmul_res(o, w_out.astype(BF16), h, tm=_row_tile(S, 512))


def s5_layer(h, B, S, g_norm, w_in, a_re, a_im, b_re, b_im, c_re, c_im, d_skip, log_dt, w_glu):
    D = h.shape[1]
    tm = _row_tile(S, 1024)
    u = norm_matmul(h, g_norm, w_in.astype(BF16), tm=tm, tn=D, out_dtype=BF16, time_major_batch=(B, S))
    y = s5_scan(u.reshape(S * B, D), a_re, a_im, b_re, b_im, c_re, c_im, d_skip, log_dt, B, S)
    return glu_res(y.reshape(S, B * D), w_glu.astype(BF16), h, B, S, tm=_row_tile(S, 512))


def kernel(x, p, positions, norm_mix, norm_ffn, norm_ple, ret_w_in, ret_gn, ret_w_out, nsa_w_in, nsa_q_norm, nsa_k_norm, nsa_cmp_pos, nsa_cmp_w1, nsa_cmp_w2, nsa_w_out, s5_w_in, s5_a_re, s5_a_im, s5_b_re, s5_b_im, s5_c_re, s5_c_im, s5_d, s5_log_dt, s5_w_glu, ffn_w_gate_up, ffn_w_down, moe_router, moe_w_gate_up, moe_w_down, ple_w_proj, ple_w_gate):
    B, S, D = x.shape
    depth = p.shape[0]
    T = B * S
    h = x.reshape(T, D)
    pos_col = positions.reshape(T, 1)
    p_all = p.reshape(depth, T, p.shape[-1])
    moe_gu, moe_dn = moe_w_gate_up, moe_w_down

    ones = jnp.ones((1, LANES), F32)
    dk = D // RET_HEADS
    inv_ret = (ROPE_THETA ** (-jnp.arange(0, dk, 2, dtype=F32) / dk)).reshape(1, LANES)
    inv_half = ROPE_THETA ** (-jnp.arange(0, NSA_DH, 2, dtype=F32) / NSA_DH)
    inv_nsa = jnp.concatenate([inv_half, inv_half]).reshape(1, LANES)
    sign_nsa = jnp.concatenate([-jnp.ones((NSA_DH // 2,), F32), jnp.ones((NSA_DH // 2,), F32)]).reshape(1, LANES)
    tabs = {}
    tabs["ret_cos"], tabs["ret_sin"] = rope_tables(pos_col, inv_ret, ones, _row_tile(T, 1024))
    if depth > 1:
        tabs["nsa_cos"], tabs["nsa_sin"] = rope_tables(pos_col, inv_nsa, sign_nsa, _row_tile(T, 1024))

    for i in range(depth):
        m, j = i % 3, i // 3
        if m == 0:
            h = retention_layer(h, tabs, B, S, norm_mix[i], ret_w_in[j], ret_gn[j], ret_w_out[j])
        elif m == 1:
            h = nsa_layer(h, tabs, B, S, norm_mix[i], nsa_w_in[j], nsa_q_norm[j], nsa_k_norm[j],
                          nsa_cmp_pos[j], nsa_cmp_w1[j], nsa_cmp_w2[j], nsa_w_out[j])
        else:
            h = s5_layer(h, B, S, norm_mix[i], s5_w_in[j], s5_a_re[j], s5_a_im[j], s5_b_re[j], s5_b_im[j],
                         s5_c_re[j], s5_c_im[j], s5_d[j], s5_log_dt[j], s5_w_glu[j])
        tm_ffn = _row_tile(T, 1024)
        ple = (norm_ple[i], p_all, i, ple_w_gate[i].astype(BF16), ple_w_proj[i].astype(BF16))
        if i % 2 == 0:
            h = swiglu_ffn_ple(h, norm_ffn[i], ffn_w_gate_up[i // 2].astype(BF16),
                               ffn_w_down[i // 2].astype(BF16), *ple, tm=tm_ffn, tf=256)
        else:
            h = moe_layer(h, norm_ffn[i], moe_router[i // 2], moe_gu, moe_dn, i // 2, ple,
                          tm_route=tm_ffn, tm_expert=tm_ffn, tm_combine=tm_ffn)
    return h.reshape(B, S, D)
```

```python
import functools
import math

import numpy as np
import jax
import jax.numpy as jnp
from jax import lax
from jax.experimental import pallas as pl
from jax.experimental.pallas import tpu as pltpu

F32 = jnp.float32
BF16 = jnp.bfloat16

NORM_EPS = 1e-6
ROPE_THETA = 10000.0

RET_HEADS = 4
RET_CHUNK = 128
RET_BLOCK = 512

NSA_HEADS = 8
NSA_GROUPS = 2
NSA_DH = 128
NSA_CMP_LEN = 32
NSA_CMP_STRIDE = 16
NSA_SEL_LEN = 64
NSA_TOP_N = 16
NSA_WINDOW = 512
NSA_Q_TILE = 256
NSA_KV_TILE = 512

S5_GROUP = 16
S5_STATE = 64
S5_SET = 8
S5_TIME_CHUNK = 64

N_EXPERTS = 8
MOE_NF = 7
LANES = 128
MASK_NEG = -1e30
LOG2_E = 1.4426950408889634

VMEM_LIMIT = 56 * 1024 * 1024


def _params(*sem):
    return pltpu.CompilerParams(dimension_semantics=sem, vmem_limit_bytes=VMEM_LIMIT)


def _dot(a, b):
    return jnp.dot(a, b, preferred_element_type=F32)


def _dot_nt(a, b):
    return lax.dot_general(a, b, (((1,), (1,)), ((), ())), preferred_element_type=F32)


def _rms(x, g):
    return x * lax.rsqrt(jnp.mean(x * x, axis=-1, keepdims=True) + NORM_EPS) * g


def _gelu_tanh(x):
    return 0.5 * x * (1.0 + jnp.tanh(math.sqrt(2.0 / math.pi) * (x + 0.044715 * (x * x * x))))


def _rope_table_kernel(pos_ref, inv_ref, sign_ref, cos_ref, sin_ref):
    ang = pos_ref[...].astype(F32) * inv_ref[...]
    cos_ref[...] = jnp.cos(ang)
    sin_ref[...] = jnp.sin(ang) * sign_ref[...]


def rope_tables(pos_col, inv, sign, tm):
    T = pos_col.shape[0]
    return pl.pallas_call(
        _rope_table_kernel,
        grid=(T // tm,),
        in_specs=[pl.BlockSpec((tm, 1), lambda i: (i, 0)),
                  pl.BlockSpec((1, LANES), lambda i: (0, 0)),
                  pl.BlockSpec((1, LANES), lambda i: (0, 0))],
        out_specs=[pl.BlockSpec((tm, LANES), lambda i: (i, 0))] * 2,
        out_shape=[jax.ShapeDtypeStruct((T, LANES), F32)] * 2,
        compiler_params=_params("parallel"),
        name="rope_tables",
    )(pos_col, inv, sign)


def _norm_matmul_kernel(x_ref, g_ref, w_ref, o_ref, xn_ref):
    @pl.when(pl.program_id(1) == 0)
    def _():
        xn_ref[...] = _rms(x_ref[...], g_ref[...]).astype(BF16)

    o_ref[...] = _dot(xn_ref[...], w_ref[...]).astype(o_ref.dtype)


def norm_matmul(x, g, w, *, tm, tn, out_dtype, time_major_batch=None):
    T, D = x.shape
    N = w.shape[1]
    nj = N // tn
    if time_major_batch is None:
        out_shape = jax.ShapeDtypeStruct((T, N), out_dtype)
        out_spec = pl.BlockSpec((tm, tn), lambda i, j: (i, j))
    else:
        B, S = time_major_batch
        ns = S // tm
        out_shape = jax.ShapeDtypeStruct((S, B * N), out_dtype)
        out_spec = pl.BlockSpec((tm, tn), lambda i, j: (i % ns, (i // ns) * nj + j))
    return pl.pallas_call(
        _norm_matmul_kernel,
        grid=(T // tm, nj),
        in_specs=[pl.BlockSpec((tm, D), lambda i, j: (i, 0)),
                  pl.BlockSpec((1, D), lambda i, j: (0, 0)),
                  pl.BlockSpec((D, tn), lambda i, j: (0, j))],
        out_specs=out_spec,
        out_shape=out_shape,
        scratch_shapes=[pltpu.VMEM((tm, D), BF16)],
        compiler_params=_params("parallel", "arbitrary"),
        name="norm_matmul",
    )(x, g.reshape(1, D), w)


def _norm_matmul_rope_kernel(x_ref, g_ref, w_ref, cos_ref, sin_ref, o_ref, xn_ref, *, n_rope_tiles, dk):
    j = pl.program_id(1)

    @pl.when(j == 0)
    def _():
        xn_ref[...] = _rms(x_ref[...], g_ref[...]).astype(BF16)

    y = _dot(xn_ref[...], w_ref[...])

    @pl.when(j < n_rope_tiles)
    def _():
        cos = cos_ref[...]
        sin = sin_ref[...]
        half = dk // 2
        for h in range(y.shape[1] // dk):
            lo, mid, hi = h * dk, h * dk + half, (h + 1) * dk
            x1, x2 = y[:, lo:mid], y[:, mid:hi]
            o_ref[:, lo:mid] = (x1 * cos - x2 * sin).astype(o_ref.dtype)
            o_ref[:, mid:hi] = (x2 * cos + x1 * sin).astype(o_ref.dtype)

    @pl.when(j >= n_rope_tiles)
    def _():
        o_ref[...] = y.astype(o_ref.dtype)


def norm_matmul_rope(x, g, w, cos, sin, *, tm, tn, rope_cols, dk, out_dtype):
    T, D = x.shape
    N = w.shape[1]
    assert rope_cols % tn == 0 and tn % dk == 0
    return pl.pallas_call(
        functools.partial(_norm_matmul_rope_kernel, n_rope_tiles=rope_cols // tn, dk=dk),
        grid=(T // tm, N // tn),
        in_specs=[pl.BlockSpec((tm, D), lambda i, j: (i, 0)),
                  pl.BlockSpec((1, D), lambda i, j: (0, 0)),
                  pl.BlockSpec((D, tn), lambda i, j: (0, j)),
                  pl.BlockSpec((tm, dk // 2), lambda i, j: (i, 0)),
                  pl.BlockSpec((tm, dk // 2), lambda i, j: (i, 0))],
        out_specs=pl.BlockSpec((tm, tn), lambda i, j: (i, j)),
        out_shape=jax.ShapeDtypeStruct((T, N), out_dtype),
        scratch_shapes=[pltpu.VMEM((tm, D), BF16)],
        compiler_params=_params("parallel", "arbitrary"),
        name="norm_matmul_rope",
    )(x, g.reshape(1, D), w, cos, sin)


def _norm_matmul_headnorm_kernel(x_ref, g_ref, w_ref, cos_ref, sin_ref, gain_ref, o_ref, xn_ref, *, block_scale):
    j = pl.program_id(1)
    per_tile = o_ref.shape[1] // LANES

    @pl.when(j == 0)
    def _():
        xn_ref[...] = _rms(x_ref[...], g_ref[...]).astype(BF16)

    y = _dot(xn_ref[...], w_ref[...])
    for jt in range(len(block_scale) // per_tile):
        @pl.when(j == jt)
        def _(jt=jt):
            for b in range(per_tile):
                c = jt * per_tile + b
                yb = y[:, b * LANES:(b + 1) * LANES]
                if block_scale[c] is not None:
                    yb = _rms(yb, gain_ref[c:c + 1, :])
                    yb = (yb * cos_ref[...] + pltpu.roll(yb, LANES // 2, axis=1) * sin_ref[...]) * block_scale[c]
                o_ref[:, b * LANES:(b + 1) * LANES] = yb.astype(o_ref.dtype)


def norm_matmul_headnorm(x, g, w, cos, sin, gains, block_scale, *, tm, tn, out_dtype):
    T, D = x.shape
    N = w.shape[1]
    assert N // LANES == len(block_scale) and tn % LANES == 0
    return pl.pallas_call(
        functools.partial(_norm_matmul_headnorm_kernel, block_scale=tuple(block_scale)),
        grid=(T // tm, N // tn),
        in_specs=[pl.BlockSpec((tm, D), lambda i, j: (i, 0)),
                  pl.BlockSpec((1, D), lambda i, j: (0, 0)),
                  pl.BlockSpec((D, tn), lambda i, j: (0, j)),
                  pl.BlockSpec((tm, LANES), lambda i, j: (i, 0)),
                  pl.BlockSpec((tm, LANES), lambda i, j: (i, 0)),
                  pl.BlockSpec(gains.shape, lambda i, j: (0, 0))],
        out_specs=pl.BlockSpec((tm, tn), lambda i, j: (i, j)),
        out_shape=jax.ShapeDtypeStruct((T, N), out_dtype),
        scratch_shapes=[pltpu.VMEM((tm, D), BF16)],
        compiler_params=_params("parallel", "arbitrary"),
        name="norm_matmul_headnorm",
    )(x, g.reshape(1, D), w, cos, sin, gains)


def _matmul_res_kernel(x_ref, w_ref, r_ref, o_ref):
    o_ref[...] = r_ref[...] + _dot(x_ref[...], w_ref[...])


def matmul_res(x, w, res, *, tm):
    T, K = x.shape
    N = w.shape[1]
    return pl.pallas_call(
        _matmul_res_kernel,
        grid=(T // tm,),
        in_specs=[pl.BlockSpec((tm, K), lambda i: (i, 0)),
                  pl.BlockSpec((K, N), lambda i: (0, 0)),
                  pl.BlockSpec((tm, N), lambda i: (i, 0))],
        out_specs=pl.BlockSpec((tm, N), lambda i: (i, 0)),
        out_shape=jax.ShapeDtypeStruct((T, N), F32),
        compiler_params=_params("parallel"),
        name="matmul_res",
    )(x, w, res)


def _retention_kernel(q_ref, k_ref, v_ref, g_ref, dmask_ref, qdec_ref, kdec_ref, cdec_ref, gn_ref, o_ref, r_scr):
    H, dk, dv = r_scr.shape
    C = dmask_ref.shape[1]

    @pl.when(pl.program_id(1) == 0)
    def _():
        r_scr[...] = jnp.zeros_like(r_scr)

    for c in range(q_ref.shape[0] // C):
        rows = slice(c * C, (c + 1) * C)
        for h in range(H):
            qb = q_ref[rows, h * dk:(h + 1) * dk]
            kb = k_ref[rows, h * dk:(h + 1) * dk]
            v = v_ref[rows, h * dv:(h + 1) * dv]
            scores = _dot_nt(qb, kb) * dmask_ref[h]
            intra = _dot(scores.astype(BF16), v)
            r_old = r_scr[h]
            cross = _dot(qb, r_old.astype(BF16)) * qdec_ref[h]
            kd_t = (kb.astype(F32) * kdec_ref[h]).T.astype(BF16)
            r_scr[h] = r_old * cdec_ref[h] + _dot(kd_t, v)

            o = intra + cross
            mu = jnp.mean(o, axis=-1, keepdims=True)
            oc = o - mu
            var = jnp.mean(oc * oc, axis=-1, keepdims=True)
            on = oc * lax.rsqrt(var + NORM_EPS) * gn_ref[:, h * dv:(h + 1) * dv]
            g = g_ref[rows, h * dv:(h + 1) * dv].astype(F32)
            o_ref[rows, h * dv:(h + 1) * dv] = (on * (g * jax.nn.sigmoid(g))).astype(o_ref.dtype)


def retention_core(proj, gn, B, S):
    H, C = RET_HEADS, RET_CHUNK
    T = proj.shape[0]
    dk = proj.shape[1] // (6 * H)
    dv = 2 * dk
    tb = min(RET_BLOCK, S)
    nb = S // tb
    scale = dk ** -0.5
    log_gamma = jnp.log1p(-jnp.exp2(-5.0 - jnp.arange(H, dtype=F32)))
    idx = jnp.arange(C, dtype=F32)
    diff = idx[:, None] - idx[None, :]
    dmask = jnp.where(diff >= 0, jnp.exp(jnp.maximum(diff, 0.0) * log_gamma[:, None, None]), 0.0) * scale
    qdec = jnp.exp((idx + 1.0) * log_gamma[:, None])[..., None]
    kdec = jnp.exp((C - 1.0 - idx) * log_gamma[:, None])[..., None] * scale
    cdec = jnp.exp(C * log_gamma).reshape(H, 1, 1)
    whole = lambda shape: pl.BlockSpec(shape, lambda b, n: (0,) * len(shape))
    return pl.pallas_call(
        _retention_kernel,
        grid=(B, nb),
        in_specs=[pl.BlockSpec((tb, H * dk), lambda b, n: (b * nb + n, 0)),
                  pl.BlockSpec((tb, H * dk), lambda b, n: (b * nb + n, 1)),
                  pl.BlockSpec((tb, H * dv), lambda b, n: (b * nb + n, 1)),
                  pl.BlockSpec((tb, H * dv), lambda b, n: (b * nb + n, 2)),
                  whole((H, C, C)), whole((H, C, 1)), whole((H, C, 1)), whole((H, 1, 1)),
                  whole((1, H * dv))],
        out_specs=pl.BlockSpec((tb, H * dv), lambda b, n: (b * nb + n, 0)),
        out_shape=jax.ShapeDtypeStruct((T, H * dv), BF16),
        scratch_shapes=[pltpu.VMEM((H, dk, dv), F32)],
        compiler_params=_params("parallel", "arbitrary"),
        name="retention",
    )(proj, proj, proj, proj, dmask, qdec, kdec, cdec, gn.reshape(1, H * dv))


def _nsa_compress_kernel(kc_ref, vc_ref, pe_ref, w1_ref, w2_ref, kn_ref, cos_ref, sin_ref,
                         kco_ref, vco_ref, xs_scr):
    nh = NSA_CMP_LEN // NSA_CMP_STRIDE
    n_rows = xs_scr.shape[0] // NSA_CMP_STRIDE
    for br, (src, dst) in enumerate(((kc_ref, kco_ref), (vc_ref, vco_ref))):
        xs_scr[...] = src[...].astype(F32)
        acc = [jnp.zeros((n_rows, w1_ref.shape[-1]), F32) for _ in range(nh)]
        for l in range(NSA_CMP_STRIDE):
            piece = xs_scr[pl.ds(l, n_rows, stride=NSA_CMP_STRIDE), :]
            for a in range(nh):
                ll = a * NSA_CMP_STRIDE + l
                acc[a] = acc[a] + _dot((piece + pe_ref[br, ll:ll + 1, :]).astype(BF16), w1_ref[br, ll])
        hid = acc[0] + pltpu.roll(acc[1], n_rows - 1, axis=0)
        z = _dot(_gelu_tanh(hid).astype(BF16), w2_ref[br])
        if br == 0:
            z = _rms(z, kn_ref[0:1, :])
            cos = cos_ref[pl.ds(0, n_rows, stride=NSA_CMP_STRIDE), :]
            sin = sin_ref[pl.ds(0, n_rows, stride=NSA_CMP_STRIDE), :]
            z = z * cos + pltpu.roll(z, NSA_DH // 2, axis=1) * sin
        dst[0, 0] = z.astype(dst.dtype)


def nsa_compress(proj, pe, w1, w2, kn, cosn, sinn, B, S):
    G, dh = NSA_GROUPS, NSA_DH
    nr = S // NSA_CMP_STRIDE
    kc0 = NSA_HEADS
    vc0 = NSA_HEADS + G
    return pl.pallas_call(
        _nsa_compress_kernel,
        grid=(B, G),
        in_specs=[pl.BlockSpec((S, dh), lambda b, g: (b, kc0 + g)),
                  pl.BlockSpec((S, dh), lambda b, g: (b, vc0 + g)),
                  pl.BlockSpec(pe.shape, lambda b, g: (0, 0, 0)),
                  pl.BlockSpec(w1.shape, lambda b, g: (0, 0, 0, 0)),
                  pl.BlockSpec(w2.shape, lambda b, g: (0, 0, 0)),
                  pl.BlockSpec(kn.shape, lambda b, g: (0, 0)),
                  pl.BlockSpec((S, dh), lambda b, g: (b, 0)),
                  pl.BlockSpec((S, dh), lambda b, g: (b, 0))],
        out_specs=[pl.BlockSpec((1, 1, nr, dh), lambda b, g: (b, g, 0, 0))] * 2,
        out_shape=[jax.ShapeDtypeStruct((B, G, nr, dh), BF16)] * 2,
        scratch_shapes=[pltpu.VMEM((S, dh), F32)],
        compiler_params=_params("parallel", "parallel"),
        name="nsa_compress",
    )(proj, proj, pe, w1, w2, kn, cosn, sinn)


def _softmax2_parts(s, bias, hg):
    tq, n = bias.shape
    s3 = s.reshape(hg, tq, n) + bias[None]
    p = jnp.exp2(s3 - jnp.max(s3, axis=-1, keepdims=True))
    return p.reshape(hg * tq, n), jnp.sum(p, axis=-1, keepdims=True).reshape(hg * tq, 1)


def _nsa_attn_kernel(q_ref, gate_ref, kc_ref, vc_ref, ks_ref, vs_ref, kw_ref, vw_ref, mmat_ref, o_ref):
    grp = pl.program_id(1)
    qi = pl.program_id(2)
    tq = q_ref.shape[0]
    dh = NSA_DH
    hg = q_ref.shape[1] // dh
    S = ks_ref.shape[0]
    tk = min(NSA_KV_TILE, S)
    q0 = qi * tq

    qall = jnp.concatenate([q_ref[:, h * dh:(h + 1) * dh] for h in range(hg)], axis=0)

    def tok(n):
        return q0 + lax.broadcasted_iota(jnp.int32, (tq, n), 0)

    def key(n):
        return lax.broadcasted_iota(jnp.int32, (tq, n), 1)

    ncp = kc_ref.shape[2]
    valid_c = key(ncp) * NSA_CMP_STRIDE + (NSA_CMP_LEN - 1) <= tok(ncp)
    p_c, l_c = _softmax2_parts(_dot_nt(qall, kc_ref[0, 0]), jnp.where(valid_c, 0.0, MASK_NEG), hg)
    t_row = q0 + jnp.bitwise_and(lax.broadcasted_iota(jnp.int32, (hg * tq, 1), 0), tq - 1)
    p_c = jnp.where(t_row >= NSA_CMP_LEN - 1, p_c / l_c, 0.0)
    o_c = _dot(p_c.astype(BF16), vc_ref[0, 0])
    p_sum = p_c[0:tq]
    for h in range(1, hg):
        p_sum = p_sum + p_c[h * tq:(h + 1) * tq]
    imp = jnp.dot(p_sum, mmat_ref[...], preferred_element_type=F32, precision=lax.Precision.HIGHEST)

    n_sel = S // NSA_SEL_LEN
    sel_rows = ((n_sel + 7) // 8) * 8
    v_imp = imp.T[0:sel_rows, :]
    jb = lax.broadcasted_iota(jnp.int32, (sel_rows, tq), 0)
    cur = (q0 + lax.broadcasted_iota(jnp.int32, (sel_rows, tq), 1)) // NSA_SEL_LEN
    forced = (jb == 0) | (jb == cur) | (jb == cur - 1)
    v_imp = jnp.where(forced, jnp.inf, jnp.where(jb > cur, -jnp.inf, v_imp))
    if sel_rows > n_sel:
        v_imp = jnp.where(jb >= n_sel, -jnp.inf, v_imp)
    rank = jnp.zeros((sel_rows, tq), jnp.int32)
    for i in range(n_sel):
        r = v_imp[i:i + 1, :]
        beats = (r > v_imp) | ((r == v_imp) & (jb > i))
        rank = rank + beats.astype(jnp.int32)
    sel_t = ((rank < min(NSA_TOP_N, n_sel)) & (jb < n_sel)).astype(F32)
    if sel_rows < LANES:
        sel_t = jnp.concatenate([sel_t, jnp.zeros((LANES - sel_rows, tq), F32)], axis=0)
    sel = sel_t.T.astype(BF16)

    row_t = tok(tk)
    lane_k = key(tk)
    e_row = lax.broadcasted_iota(jnp.int32, (LANES, tk), 0)
    e_blk = lax.broadcasted_iota(jnp.int32, (LANES, tk), 1) // NSA_SEL_LEN

    def sel_step(j, carry):
        m, l, acc = carry
        base = pl.multiple_of(j * tk, tk)
        s = _dot_nt(qall, ks_ref[pl.ds(base, tk), :])
        expand = (e_blk + j * (tk // NSA_SEL_LEN) == e_row).astype(BF16)
        chosen = _dot(sel, expand)
        bias = jnp.where((chosen > 0.5) & (lane_k + base <= row_t), 0.0, MASK_NEG)
        s3 = s.reshape(hg, tq, tk) + bias[None]
        m_new = jnp.maximum(m, jnp.max(s3, axis=-1, keepdims=True))
        alpha = jnp.exp2(m - m_new)
        p = jnp.exp2(s3 - m_new)
        l = alpha * l + jnp.sum(p, axis=-1, keepdims=True)
        pv = _dot(p.reshape(hg * tq, tk).astype(BF16), vs_ref[pl.ds(base, tk), :])
        acc = alpha.reshape(hg * tq, 1) * acc + pv
        return m_new, l, acc

    n_kv = (q0 + tq + tk - 1) // tk
    init = (jnp.full((hg, tq, 1), MASK_NEG, F32), jnp.zeros((hg, tq, 1), F32), jnp.zeros((hg * tq, dh), F32))
    _, l_s, acc_s = lax.fori_loop(0, n_kv, sel_step, init)
    o_s = acc_s / l_s.reshape(hg * tq, 1)

    span = min(tq + NSA_WINDOW, S)
    start = pl.multiple_of(jnp.maximum(q0 + tq - span, 0), tq)
    dist = tok(span) - (start + key(span))
    bias_w = jnp.where((dist >= 0) & (dist < NSA_WINDOW), 0.0, MASK_NEG)
    p_w, l_w = _softmax2_parts(_dot_nt(qall, kw_ref[pl.ds(start, span), :]), bias_w, hg)
    o_w = _dot(p_w.astype(BF16), vw_ref[pl.ds(start, span), :]) / l_w

    gsig = jax.nn.sigmoid(gate_ref[...].astype(F32))
    n_br = 3
    for h in range(hg):
        def gate_col(br):
            lo = h * n_br + br
            hi = lo + hg * n_br
            return jnp.where(grp == 0, gsig[:, lo:lo + 1], gsig[:, hi:hi + 1])
        rows = slice(h * tq, (h + 1) * tq)
        o = gate_col(0) * o_c[rows] + gate_col(1) * o_s[rows] + gate_col(2) * o_w[rows]
        o_ref[:, h * dh:(h + 1) * dh] = o.astype(o_ref.dtype)


def nsa_attention(proj, kcc, vcc, B, S):
    H, G, dh = NSA_HEADS, NSA_GROUPS, NSA_DH
    hg = H // G
    T = proj.shape[0]
    tq = min(NSA_Q_TILE, S)
    nq = S // tq
    nr = kcc.shape[2]
    n_c = (S - NSA_CMP_LEN) // NSA_CMP_STRIDE + 1
    n_sel = S // NSA_SEL_LEN
    cs = (np.arange(n_c) * NSA_CMP_STRIDE)[:, None]
    js = (np.arange(n_sel) * NSA_SEL_LEN)[None, :]
    overlap = np.clip(np.minimum(cs + NSA_CMP_LEN, js + NSA_SEL_LEN) - np.maximum(cs, js), 0, None) / NSA_CMP_LEN
    mmat = np.zeros((nr, LANES), np.float32)
    mmat[:n_c, :n_sel] = overlap
    col = lambda base: (lambda b, g, i: (b, base + g))
    return pl.pallas_call(
        _nsa_attn_kernel,
        grid=(B, G, nq),
        in_specs=[pl.BlockSpec((tq, hg * dh), lambda b, g, i: (b * nq + i, g)),
                  pl.BlockSpec((tq, LANES), lambda b, g, i: (b * nq + i, H + 6 * G)),
                  pl.BlockSpec((1, 1, nr, dh), lambda b, g, i: (b, g, 0, 0)),
                  pl.BlockSpec((1, 1, nr, dh), lambda b, g, i: (b, g, 0, 0)),
                  pl.BlockSpec((S, dh), col(H + 2 * G)),
                  pl.BlockSpec((S, dh), col(H + 3 * G)),
                  pl.BlockSpec((S, dh), col(H + 4 * G)),
                  pl.BlockSpec((S, dh), col(H + 5 * G)),
                  pl.BlockSpec(mmat.shape, lambda b, g, i: (0, 0))],
        out_specs=pl.BlockSpec((tq, hg * dh), lambda b, g, i: (b * nq + i, g)),
        out_shape=jax.ShapeDtypeStruct((T, H * dh), BF16),
        compiler_params=_params("parallel", "parallel", "arbitrary"),
        name="nsa_attention",
    )(proj, proj, kcc, vcc, proj, proj, proj, proj, jnp.asarray(mmat))


def _s5_scan_kernel(u_ref, bbd_ref, are_ref, aim_ref, cbd_ref, d_ref, y_ref, bu_scr, xs_scr, st_scr):
    nb = st_scr.shape[0] // 2
    w = are_ref.shape[-1]
    tc = u_ref.shape[0] // nb

    @pl.when(pl.program_id(1) == 0)
    def _():
        st_scr[...] = jnp.zeros_like(st_scr)

    u = u_ref[...]
    bu_scr[...] = _dot(u, bbd_ref[0])
    a_re = jnp.broadcast_to(are_ref[0], (nb, w))
    a_im = jnp.broadcast_to(aim_ref[0], (nb, w))

    def step(t, carry):
        xr, xi = carry
        r0 = pl.multiple_of(t * nb, nb)
        nxr = a_re * xr - a_im * xi + bu_scr[pl.ds(r0, nb), 0:w]
        nxi = a_re * xi + a_im * xr + bu_scr[pl.ds(r0, nb), w:2 * w]
        xs_scr[pl.ds(r0, nb), 0:w] = nxr
        xs_scr[pl.ds(r0, nb), w:2 * w] = nxi
        return nxr, nxi

    xr, xi = lax.fori_loop(0, tc, step, (st_scr[0:nb, :], st_scr[nb:2 * nb, :]), unroll=4)
    st_scr[0:nb, :] = xr
    st_scr[nb:2 * nb, :] = xi
    y = _dot(xs_scr[...].astype(BF16), cbd_ref[0]) + d_ref[...] * u.astype(F32)
    y_ref[...] = _gelu_tanh(y).astype(y_ref.dtype)


def s5_scan(u_tm, a_re, a_im, b_re, b_im, c_re, c_im, d_skip, log_dt, B, S):
    D = u_tm.shape[1]
    Cg, P, NS = S5_GROUP, S5_STATE, S5_SET
    G = D // Cg
    K = G // NS
    w = NS * P
    tc = min(S5_TIME_CHUNK, S)
    dt = jnp.exp(log_dt.astype(F32))[:, None]
    mag = jnp.exp(dt * a_re)
    abar_re, abar_im = mag * jnp.cos(dt * a_im), mag * jnp.sin(dt * a_im)
    den = a_re * a_re + a_im * a_im
    nr_, ni_ = abar_re - 1.0, abar_im
    f_re = (nr_ * a_re + ni_ * a_im) / den
    f_im = (ni_ * a_re - nr_ * a_im) / den
    bbar_re = f_re[..., None] * b_re - f_im[..., None] * b_im
    bbar_im = f_re[..., None] * b_im + f_im[..., None] * b_re
    eye = jnp.eye(NS, dtype=F32)

    def in_blockdiag(bb):
        t = bb.reshape(K, NS, P, Cg).transpose(0, 1, 3, 2)
        return jnp.einsum('kgcp,gh->kgchp', t, eye).reshape(K, NS * Cg, NS * P)

    def out_blockdiag(cc):
        t = cc.reshape(K, NS, Cg, P).transpose(0, 1, 3, 2)
        return jnp.einsum('kgpc,gh->kgphc', t, eye).reshape(K, NS * P, NS * Cg)

    bbd = jnp.concatenate([in_blockdiag(bbar_re), in_blockdiag(bbar_im)], axis=-1).astype(BF16)
    cbd = jnp.concatenate([out_blockdiag(c_re), -out_blockdiag(c_im)], axis=1).astype(BF16)
    are = abar_re.reshape(K, 1, w)
    aim = abar_im.reshape(K, 1, w)
    rows = tc * B
    return pl.pallas_call(
        _s5_scan_kernel,
        grid=(K, S // tc),
        in_specs=[pl.BlockSpec((rows, LANES), lambda k, c: (c, k)),
                  pl.BlockSpec((1, LANES, 2 * w), lambda k, c: (k, 0, 0)),
                  pl.BlockSpec((1, 1, w), lambda k, c: (k, 0, 0)),
                  pl.BlockSpec((1, 1, w), lambda k, c: (k, 0, 0)),
                  pl.BlockSpec((1, 2 * w, LANES), lambda k, c: (k, 0, 0)),
                  pl.BlockSpec((1, LANES), lambda k, c: (0, k))],
        out_specs=pl.BlockSpec((rows, LANES), lambda k, c: (c, k)),
        out_shape=jax.ShapeDtypeStruct((S * B, D), BF16),
        scratch_shapes=[pltpu.VMEM((rows, 2 * w), F32), pltpu.VMEM((rows, 2 * w), F32),
                        pltpu.VMEM((2 * B, w), F32)],
        compiler_params=_params("parallel", "arbitrary"),
        name="s5_scan",
    )(u_tm, bbd, are, aim, cbd, d_skip.reshape(1, D))


def _glu_res_kernel(y_ref, wa_ref, wb_ref, r_ref, o_ref):
    y = y_ref[...]
    a = _dot(y, wa_ref[...])
    b = _dot(y, wb_ref[...])
    o_ref[...] = r_ref[...] + a * jax.nn.sigmoid(b)


def glu_res(y_tm, w, res, B, S, *, tm):
    D = w.shape[0]
    N = w.shape[1] // 2
    ns = S // tm
    return pl.pallas_call(
        _glu_res_kernel,
        grid=(B * ns,),
        in_specs=[pl.BlockSpec((tm, D), lambda i: (i % ns, i // ns)),
                  pl.BlockSpec((D, N), lambda i: (0, 0)),
                  pl.BlockSpec((D, N), lambda i: (0, 1)),
                  pl.BlockSpec((tm, N), lambda i: (i, 0))],
        out_specs=pl.BlockSpec((tm, N), lambda i: (i, 0)),
        out_shape=jax.ShapeDtypeStruct((B * S, N), F32),
        compiler_params=_params("parallel"),
        name="glu_res",
    )(y_tm, w, w, res)


def _ple(h, g_ref, p_ref, wgate_ref, wproj_ref):
    gate = jax.nn.sigmoid(_dot(_rms(h, g_ref[...]).astype(BF16), wgate_ref[...]))
    return h + gate * _dot(p_ref[...].astype(BF16), wproj_ref[...])


def _ffn_kernel(x_ref, g_ref, wg_ref, wu_ref, wd_ref, gp_ref, p_ref, wgate_ref, wproj_ref, o_ref,
                xn_scr, acc_scr):
    f = pl.program_id(1)

    @pl.when(f == 0)
    def _():
        xn_scr[...] = _rms(x_ref[...], g_ref[...]).astype(BF16)
        acc_scr[...] = jnp.zeros_like(acc_scr)

    xn = xn_scr[...]
    gg = _dot(xn, wg_ref[...])
    uu = _dot(xn, wu_ref[...])
    acc_scr[...] += _dot((gg * jax.nn.sigmoid(gg) * uu).astype(BF16), wd_ref[...])

    @pl.when(f == pl.num_programs(1) - 1)
    def _():
        o_ref[...] = _ple(x_ref[...] + acc_scr[...], gp_ref, p_ref, wgate_ref, wproj_ref)


def swiglu_ffn_ple(x, g, w_gate_up, w_down, g_ple, p_all, layer, w_pgate, w_pproj, *, tm, tf):
    T, D = x.shape
    F = w_down.shape[0]
    PD = p_all.shape[2]
    nf = F // tf
    return pl.pallas_call(
        _ffn_kernel,
        grid=(T // tm, nf),
        in_specs=[pl.BlockSpec((tm, D), lambda i, f: (i, 0)),
                  pl.BlockSpec((1, D), lambda i, f: (0, 0)),
                  pl.BlockSpec((D, tf), lambda i, f: (0, f)),
                  pl.BlockSpec((D, tf), lambda i, f: (0, nf + f)),
                  pl.BlockSpec((tf, D), lambda i, f: (f, 0)),
                  pl.BlockSpec((1, D), lambda i, f: (0, 0)),
                  pl.BlockSpec((None, tm, PD), lambda i, f: (layer, i, 0)),
                  pl.BlockSpec((D, D), lambda i, f: (0, 0)),
                  pl.BlockSpec((PD, D), lambda i, f: (0, 0))],
        out_specs=pl.BlockSpec((tm, D), lambda i, f: (i, 0)),
        out_shape=jax.ShapeDtypeStruct((T, D), F32),
        scratch_shapes=[pltpu.VMEM((tm, D), BF16), pltpu.VMEM((tm, D), F32)],
        compiler_params=_params("parallel", "arbitrary"),
        name="swiglu_ffn_ple",
    )(x, g.reshape(1, D), w_gate_up, w_gate_up, w_down, g_ple.reshape(1, D), p_all, w_pgate, w_pproj)


META_E0, META_E1, META_W0, META_W1, META_R0, META_R1 = range(6)


def _router_kernel(x_ref, g_ref, w_ref, meta_ref, cnt_ref, carry_scr, tri_scr):
    @pl.when(pl.program_id(0) == 0)
    def _():
        carry_scr[...] = jnp.zeros_like(carry_scr)
        n = tri_scr.shape[0]
        tri = lax.broadcasted_iota(jnp.int32, (n, n), 0) > lax.broadcasted_iota(jnp.int32, (n, n), 1)
        tri_scr[...] = tri.astype(BF16)

    xn = _rms(x_ref[...], g_ref[...])
    x_hi = xn.astype(BF16)
    x_lo = (xn - x_hi.astype(F32)).astype(BF16)
    w = w_ref[...]
    w_hi = w.astype(BF16)
    w_lo = (w - w_hi.astype(F32)).astype(BF16)
    logits = _dot(x_hi, w_hi) + (_dot(x_hi, w_lo) + _dot(x_lo, w_hi))
    lane = lax.broadcasted_iota(jnp.int32, logits.shape, 1).astype(F32)
    logits = jnp.where(lane < N_EXPERTS, logits, -jnp.inf)

    def top1(v):
        m = jnp.max(v, axis=-1, keepdims=True)
        idx = jnp.min(jnp.where(v == m, lane, float(LANES)), axis=-1, keepdims=True)
        return m, idx, lane == idx

    m1, e0, hot1 = top1(logits)
    m2, e1, hot2 = top1(jnp.where(hot1, -jnp.inf, logits))
    ex = jnp.exp(m2 - m1)
    w0 = 1.0 / (1.0 + ex)
    w1 = ex * w0

    both = jnp.where(hot1 | hot2, 1.0, 0.0)
    before = _dot(tri_scr[...], both.astype(BF16)) + carry_scr[0:1, :]
    r0 = jnp.sum(jnp.where(hot1, before, 0.0), axis=-1, keepdims=True)
    r1 = jnp.sum(jnp.where(hot2, before, 0.0), axis=-1, keepdims=True)
    carry_scr[...] = carry_scr[...] + jnp.sum(both, axis=0, keepdims=True)
    cnt_ref[...] = carry_scr[...]

    meta = jnp.zeros_like(logits)
    for k, v in ((META_E0, e0), (META_E1, e1), (META_W0, w0), (META_W1, w1), (META_R0, r0), (META_R1, r1)):
        meta = jnp.where(lane == float(k), v, meta)
    meta_ref[...] = meta


def router_topk(x, g, w_router, *, tm):
    T, D = x.shape
    w = jnp.zeros((D, LANES), F32).at[:, :N_EXPERTS].set(w_router)
    return pl.pallas_call(
        _router_kernel,
        grid=(T // tm,),
        in_specs=[pl.BlockSpec((tm, D), lambda i: (i, 0)),
                  pl.BlockSpec((1, D), lambda i: (0, 0)),
                  pl.BlockSpec((D, LANES), lambda i: (0, 0))],
        out_specs=[pl.BlockSpec((tm, LANES), lambda i: (i, 0)),
                   pl.BlockSpec((8, LANES), lambda i: (0, 0))],
        out_shape=[jax.ShapeDtypeStruct((T, LANES), F32), jax.ShapeDtypeStruct((8, LANES), F32)],
        scratch_shapes=[pltpu.VMEM((8, LANES), F32), pltpu.VMEM((tm, tm), BF16)],
        compiler_params=_params("arbitrary"),
        name="router",
    )(x, g.reshape(1, D), w)


def _start_row_gathers(idx_ref, src_hbm, dst_ref, sem, lo, hi, idx_off=0):
    for r in range(lo, hi):
        pltpu.make_async_copy(src_hbm.at[pl.ds(idx_ref[0, 0, idx_off + r], 1), :], dst_ref.at[pl.ds(r, 1), :],
                              sem).start()


def _wait_row_gathers(src_hbm, dst_ref, sem):
    pltpu.make_async_copy(src_hbm.at[pl.ds(0, dst_ref.shape[0]), :], dst_ref, sem).wait()


def _moe_expert_kernel(te_ref, nu_ref, tok_ref, tok_next_ref, x_hbm, g_ref, wg_ref, wu_ref, wd_ref, y_ref,
                       xg_scr, xn_scr, acc_scr, sems):
    i = pl.program_id(0)
    f = pl.program_id(1)
    n_tiles = pl.num_programs(0)
    nf = pl.num_programs(1)
    n_used = nu_ref[0]
    active = i < n_used
    tm = xn_scr.shape[0]
    cur = xg_scr.at[i % 2]
    nxt = xg_scr.at[(i + 1) % 2]
    sem_cur = sems.at[i % 2]
    sem_nxt = sems.at[(i + 1) % 2]

    @pl.when((i == 0) & (f == 0))
    def _():
        def issue(r, carry):
            pltpu.make_async_copy(x_hbm.at[pl.ds(tok_ref[0, 0, r], 1), :], cur.at[pl.ds(r, 1), :], sem_cur).start()
            return carry
        lax.fori_loop(0, tm, issue, 0, unroll=8)

    @pl.when((f == 0) & (i <= n_used))
    def _():
        _wait_row_gathers(x_hbm, cur, sem_cur)

    @pl.when(active & (f == 0))
    def _():
        xn_scr[...] = _rms(cur[...], g_ref[...]).astype(BF16)
        acc_scr[...] = jnp.zeros_like(acc_scr)

    def step(lo, hi):
        xn = xn_scr[...]
        gg = _dot(xn, wg_ref[0].astype(BF16))
        uu = _dot(xn, wu_ref[0].astype(BF16))
        acc_scr[...] += _dot((gg * jax.nn.sigmoid(gg) * uu).astype(BF16), wd_ref[0].astype(BF16))
        _start_row_gathers(tok_next_ref, x_hbm, nxt, sem_nxt, lo, hi)

    per = -(-tm // MOE_NF)
    for fs in range(MOE_NF):
        @pl.when(active & (f == fs))
        def _(fs=fs):
            step(min(fs * per, tm), min((fs + 1) * per, tm))

    @pl.when(f == nf - 1)
    def _():
        y_ref[...] = jnp.where(active, acc_scr[...], 0.0)

    @pl.when(active & (i == n_tiles - 1) & (f == nf - 1))
    def _():
        _wait_row_gathers(x_hbm, nxt, sem_nxt)


def moe_expert_ffn(x, g, tok_of_slot, tile_expert, n_used, w_gate_up, w_down, layer, *, tm):
    T, D = x.shape
    _, E, F, _ = w_down.shape
    nf = MOE_NF
    tf = F // nf
    n_tiles = tok_of_slot.shape[0]

    def fblk(i, f, nu):
        return jnp.where(i < nu[0], f, nf - 1)

    tok_spec = lambda off: pl.BlockSpec((1, 1, tm), lambda i, f, te, nu: (jnp.minimum(i + off, n_tiles - 1), 0, 0),
                                        memory_space=pltpu.SMEM)
    grid_spec = pltpu.PrefetchScalarGridSpec(
        num_scalar_prefetch=2,
        grid=(n_tiles, nf),
        in_specs=[tok_spec(0), tok_spec(1),
                  pl.BlockSpec(memory_space=pl.ANY),
                  pl.BlockSpec((1, D), lambda i, f, te, nu: (0, 0)),
                  pl.BlockSpec((None, 1, D, tf), lambda i, f, te, nu: (layer, te[i], 0, fblk(i, f, nu))),
                  pl.BlockSpec((None, 1, D, tf), lambda i, f, te, nu: (layer, te[i], 0, nf + fblk(i, f, nu))),
                  pl.BlockSpec((None, 1, tf, D), lambda i, f, te, nu: (layer, te[i], fblk(i, f, nu), 0))],
        out_specs=pl.BlockSpec((tm, D), lambda i, f, te, nu: (i, 0)),
        scratch_shapes=[pltpu.VMEM((2, tm, D), F32), pltpu.VMEM((tm, D), BF16), pltpu.VMEM((tm, D), F32),
                        pltpu.SemaphoreType.DMA((2,))],
    )
    return pl.pallas_call(
        _moe_expert_kernel,
        grid_spec=grid_spec,
        out_shape=jax.ShapeDtypeStruct((n_tiles * tm, D), F32),
        compiler_params=_params("arbitrary", "arbitrary"),
        name="moe_experts",
    )(tile_expert, n_used, tok_of_slot, tok_of_slot, x, g.reshape(1, D), w_gate_up, w_gate_up, w_down)


def _moe_combine_kernel(s0_ref, s1_ref, s0n_ref, s1n_ref, x_ref, meta_ref, y_hbm, gp_ref, p_ref, wgate_ref,
                        wproj_ref, o_ref, ya0, ya1, yb0, yb1, sems):
    i = pl.program_id(0)
    n = ya0.shape[0]

    def gather(idx0, idx1, buf0, buf1, k, off):
        _start_row_gathers(idx0, y_hbm, buf0, sems.at[k, 0], 0, n, off)
        _start_row_gathers(idx1, y_hbm, buf1, sems.at[k, 1], 0, n, off)

    def wait(buf0, buf1, k):
        _wait_row_gathers(y_hbm, buf0, sems.at[k, 0])
        _wait_row_gathers(y_hbm, buf1, sems.at[k, 1])

    def combine(buf0, buf1, rows):
        meta = meta_ref[rows, :]
        w0 = meta[:, META_W0:META_W0 + 1]
        w1 = meta[:, META_W1:META_W1 + 1]
        h = x_ref[rows, :] + w0 * buf0[...] + w1 * buf1[...]
        gate = jax.nn.sigmoid(_dot(_rms(h, gp_ref[...]).astype(BF16), wgate_ref[...]))
        o_ref[rows, :] = h + gate * _dot(p_ref[rows, :].astype(BF16), wproj_ref[...])

    @pl.when(i == 0)
    def _():
        gather(s0_ref, s1_ref, ya0, ya1, 0, 0)

    wait(ya0, ya1, 0)
    gather(s0_ref, s1_ref, yb0, yb1, 1, n)
    combine(ya0, ya1, slice(0, n))
    wait(yb0, yb1, 1)
    gather(s0n_ref, s1n_ref, ya0, ya1, 0, 0)
    combine(yb0, yb1, slice(n, 2 * n))

    @pl.when(i == pl.num_programs(0) - 1)
    def _():
        wait(ya0, ya1, 0)


def moe_combine_ple(x, meta, slot0, slot1, y, g_ple, p_all, layer, w_pgate, w_pproj, *, tm):
    T, D = x.shape
    PD = p_all.shape[2]
    nt = T // tm
    idx = lambda off: pl.BlockSpec((1, 1, tm), lambda i: (jnp.minimum(i + off, nt - 1), 0, 0),
                                   memory_space=pltpu.SMEM)
    return pl.pallas_call(
        _moe_combine_kernel,
        grid=(nt,),
        in_specs=[idx(0), idx(0), idx(1), idx(1),
                  pl.BlockSpec((tm, D), lambda i: (i, 0)),
                  pl.BlockSpec((tm, LANES), lambda i: (i, 0)),
                  pl.BlockSpec(memory_space=pl.ANY),
                  pl.BlockSpec((1, D), lambda i: (0, 0)),
                  pl.BlockSpec((None, tm, PD), lambda i: (layer, i, 0)),
                  pl.BlockSpec((D, D), lambda i: (0, 0)),
                  pl.BlockSpec((PD, D), lambda i: (0, 0))],
        out_specs=pl.BlockSpec((tm, D), lambda i: (i, 0)),
        out_shape=jax.ShapeDtypeStruct((T, D), F32),
        scratch_shapes=[pltpu.VMEM((tm // 2, D), F32)] * 4 + [pltpu.SemaphoreType.DMA((2, 2))],
        compiler_params=_params("arbitrary"),
        name="moe_combine_ple",
    )(slot0, slot1, slot0, slot1, x, meta, y, g_ple.reshape(1, D), p_all, w_pgate, w_pproj)


def moe_layer(x, g, w_router, w_gate_up, w_down, layer, ple, *, tm_route, tm_expert, tm_combine):
    T, D = x.shape
    E = w_down.shape[1]
    n_tiles = (2 * T) // tm_expert + E
    meta, cnt = router_topk(x, g, w_router, tm=tm_route)
    e0 = meta[:, META_E0].astype(jnp.int32)
    e1 = meta[:, META_E1].astype(jnp.int32)
    r0 = meta[:, META_R0].astype(jnp.int32)
    r1 = meta[:, META_R1].astype(jnp.int32)
    counts = cnt[0, :E].astype(jnp.int32)
    tiles_per = (counts + tm_expert - 1) // tm_expert
    tile_end = jnp.cumsum(tiles_per)
    group_start = (tile_end - tiles_per) * tm_expert
    slot0 = group_start[e0] + r0
    slot1 = group_start[e1] + r1
    n_used = tile_end[-1:]
    tile_ids = jnp.arange(n_tiles, dtype=jnp.int32)
    tile_expert = jnp.searchsorted(tile_end, jnp.minimum(tile_ids, n_used[0] - 1), side="right").astype(jnp.int32)
    tile_expert = jnp.minimum(tile_expert, E - 1)
    tok = jnp.arange(T, dtype=jnp.int32)
    tok_of_slot = jnp.zeros((n_tiles * tm_expert,), jnp.int32).at[slot0].set(tok).at[slot1].set(tok)
    slot0 = slot0.reshape(T // tm_combine, 1, tm_combine)
    slot1 = slot1.reshape(T // tm_combine, 1, tm_combine)
    y = moe_expert_ffn(x, g, tok_of_slot.reshape(n_tiles, 1, tm_expert), tile_expert, n_used.astype(jnp.int32),
                       w_gate_up, w_down, layer, tm=tm_expert)
    return moe_combine_ple(x, meta, slot0, slot1, y, *ple, tm=tm_combine)


def _row_tile(S, want):
    return min(want, S)


def retention_layer(h, tabs, B, S, g_norm, w_in, gn, w_out):
    tm = _row_tile(S, 1024)
    dk = w_in.shape[1] // (6 * RET_HEADS)
    qk_cols = 2 * RET_HEADS * dk
    proj = norm_matmul_rope(h, g_norm, w_in.astype(BF16), tabs["ret_cos"], tabs["ret_sin"], tm=tm, tn=qk_cols,
                            rope_cols=qk_cols, dk=dk, out_dtype=BF16)
    o = retention_core(proj, gn, B, S)
    return matmul_res(o, w_out.astype(BF16), h, tm=_row_tile(S, 512))


def nsa_layer(h, tabs, B, S, g_norm, w_in, q_norm, k_norm, cmp_pos, cmp_w1, cmp_w2, w_out):
    D, n_in = w_in.shape
    n_pad = -(-n_in // (7 * LANES)) * (7 * LANES)
    w_in_p = jnp.zeros((D, n_pad), BF16).at[:, :n_in].set(w_in.astype(BF16))
    H, G = NSA_HEADS, NSA_GROUPS
    n_blocks = n_pad // LANES
    block_scale = [None] * n_blocks
    gains = jnp.zeros((-(-n_blocks // 8) * 8, NSA_DH), F32)
    for c in range(H):
        block_scale[c] = NSA_DH ** -0.5 * LOG2_E
    gains = gains.at[0:H].set(jnp.broadcast_to(q_norm, (H, NSA_DH)))
    for branch, first in ((1, H + 2 * G), (2, H + 4 * G)):
        for c in range(first, first + G):
            block_scale[c] = 1.0
        gains = gains.at[first:first + G].set(jnp.broadcast_to(k_norm[branch], (G, NSA_DH)))
    proj = norm_matmul_headnorm(h, g_norm, w_in_p, tabs["nsa_cos"], tabs["nsa_sin"], gains, block_scale,
                                tm=_row_tile(S, 1024), tn=7 * LANES, out_dtype=BF16)
    kn = jnp.zeros((8, NSA_DH), F32).at[:3].set(k_norm)
    w1 = cmp_w1.astype(BF16).reshape(2, NSA_CMP_LEN, NSA_DH, cmp_w1.shape[-1])
    kcc, vcc = nsa_compress(proj, cmp_pos, w1, cmp_w2.astype(BF16), kn, tabs["nsa_cos"], tabs["nsa_sin"], B, S)
    o = nsa_attention(proj, kcc, vcc, B, S)
    return matmul_res(o, w_out.astype(BF16), h, tm=_row_tile(S, 512))


def s5_layer(h, B, S, g_norm, w_in, a_re, a_im, b_re, b_im, c_re, c_im, d_skip, log_dt, w_glu):
    D = h.shape[1]
    tm = _row_tile(S, 1024)
    u = norm_matmul(h, g_norm, w_in.astype(BF16), tm=tm, tn=D, out_dtype=BF16, time_major_batch=(B, S))
    y = s5_scan(u.reshape(S * B, D), a_re, a_im, b_re, b_im, c_re, c_im, d_skip, log_dt, B, S)
    return glu_res(y.reshape(S, B * D), w_glu.astype(BF16), h, B, S, tm=_row_tile(S, 512))


def kernel(x, p, positions, norm_mix, norm_ffn, norm_ple, ret_w_in, ret_gn, ret_w_out, nsa_w_in, nsa_q_norm, nsa_k_norm, nsa_cmp_pos, nsa_cmp_w1, nsa_cmp_w2, nsa_w_out, s5_w_in, s5_a_re, s5_a_im, s5_b_re, s5_b_im, s5_c_re, s5_c_im, s5_d, s5_log_dt, s5_w_glu, ffn_w_gate_up, ffn_w_down, moe_router, moe_w_gate_up, moe_w_down, ple_w_proj, ple_w_gate):
    B, S, D = x.shape
    depth = p.shape[0]
    T = B * S
    h = x.reshape(T, D)
    pos_col = positions.reshape(T, 1)
    p_all = p.reshape(depth, T, p.shape[-1])
    moe_gu, moe_dn = moe_w_gate_up, moe_w_down

    ones = jnp.ones((1, LANES), F32)
    dk = D // RET_HEADS
    inv_ret = (ROPE_THETA ** (-jnp.arange(0, dk, 2, dtype=F32) / dk)).reshape(1, LANES)
    inv_half = ROPE_THETA ** (-jnp.arange(0, NSA_DH, 2, dtype=F32) / NSA_DH)
    inv_nsa = jnp.concatenate([inv_half, inv_half]).reshape(1, LANES)
    sign_nsa = jnp.concatenate([-jnp.ones((NSA_DH // 2,), F32), jnp.ones((NSA_DH // 2,), F32)]).reshape(1, LANES)
    tabs = {}
    tabs["ret_cos"], tabs["ret_sin"] = rope_tables(pos_col, inv_ret, ones, _row_tile(T, 1024))
    if depth > 1:
        tabs["nsa_cos"], tabs["nsa_sin"] = rope_tables(pos_col, inv_nsa, sign_nsa, _row_tile(T, 1024))

    for i in range(depth):
        m, j = i % 3, i // 3
        if m == 0:
            h = retention_layer(h, tabs, B, S, norm_mix[i], ret_w_in[j], ret_gn[j], ret_w_out[j])
        elif m == 1:
            h = nsa_layer(h, tabs, B, S, norm_mix[i], nsa_w_in[j], nsa_q_norm[j], nsa_k_norm[j],
                          nsa_cmp_pos[j], nsa_cmp_w1[j], nsa_cmp_w2[j], nsa_w_out[j])
        else:
            h = s5_layer(h, B, S, norm_mix[i], s5_w_in[j], s5_a_re[j], s5_a_im[j], s5_b_re[j], s5_b_im[j],
                         s5_c_re[j], s5_c_im[j], s5_d[j], s5_log_dt[j], s5_w_glu[j])
        tm_ffn = _row_tile(T, 1024)
        ple = (norm_ple[i], p_all, i, ple_w_gate[i].astype(BF16), ple_w_proj[i].astype(BF16))
        if i % 2 == 0:
            h = swiglu_ffn_ple(h, norm_ffn[i], ffn_w_gate_up[i // 2].astype(BF16),
                               ffn_w_down[i // 2].astype(BF16), *ple, tm=tm_ffn, tf=256)
        else:
            h = moe_layer(h, norm_ffn[i], moe_router[i // 2], moe_gu, moe_dn, i // 2, ple,
                          tm_route=tm_ffn, tm_expert=tm_ffn, tm_combine=tm_ffn)
    return h.reshape(B, S, D)
```

```python
import functools
import math

import numpy as np
import jax
import jax.numpy as jnp
from jax import lax
from jax.experimental import pallas as pl
from jax.experimental.pallas import tpu as pltpu

F32 = jnp.float32
BF16 = jnp.bfloat16

NORM_EPS = 1e-6
ROPE_THETA = 10000.0

RET_HEADS = 4
RET_CHUNK = 128
RET_BLOCK = 512

NSA_HEADS = 8
NSA_GROUPS = 2
NSA_DH = 128
NSA_CMP_LEN = 32
NSA_CMP_STRIDE = 16
NSA_SEL_LEN = 64
NSA_TOP_N = 16
NSA_WINDOW = 512
NSA_Q_TILE = 256
NSA_KV_TILE = 512

S5_GROUP = 16
S5_STATE = 64
S5_SET = 8
S5_TIME_CHUNK = 64

N_EXPERTS = 8
MOE_NF = 7
LANES = 128
MASK_NEG = -1e30
LOG2_E = 1.4426950408889634

VMEM_LIMIT = 56 * 1024 * 1024


def _params(*sem):
    return pltpu.CompilerParams(dimension_semantics=sem, vmem_limit_bytes=VMEM_LIMIT)


def _dot(a, b):
    return jnp.dot(a, b, preferred_element_type=F32)


def _dot_nt(a, b):
    return lax.dot_general(a, b, (((1,), (1,)), ((), ())), preferred_element_type=F32)


def _rms(x, g):
    return x * lax.rsqrt(jnp.mean(x * x, axis=-1, keepdims=True) + NORM_EPS) * g


def _gelu_tanh(x):
    return 0.5 * x * (1.0 + jnp.tanh(math.sqrt(2.0 / math.pi) * (x + 0.044715 * (x * x * x))))


def _rope_table_kernel(pos_ref, inv_ref, sign_ref, cos_ref, sin_ref):
    ang = pos_ref[...].astype(F32) * inv_ref[...]
    cos_ref[...] = jnp.cos(ang)
    sin_ref[...] = jnp.sin(ang) * sign_ref[...]


def rope_tables(pos_col, inv, sign, tm):
    T = pos_col.shape[0]
    return pl.pallas_call(
        _rope_table_kernel,
        grid=(T // tm,),
        in_specs=[pl.BlockSpec((tm, 1), lambda i: (i, 0)),
                  pl.BlockSpec((1, LANES), lambda i: (0, 0)),
                  pl.BlockSpec((1, LANES), lambda i: (0, 0))],
        out_specs=[pl.BlockSpec((tm, LANES), lambda i: (i, 0))] * 2,
        out_shape=[jax.ShapeDtypeStruct((T, LANES), F32)] * 2,
        compiler_params=_params("parallel"),
        name="rope_tables",
    )(pos_col, inv, sign)


def _norm_matmul_kernel(x_ref, g_ref, w_ref, o_ref, xn_ref):
    @pl.when(pl.program_id(1) == 0)
    def _():
        xn_ref[...] = _rms(x_ref[...], g_ref[...]).astype(BF16)

    o_ref[...] = _dot(xn_ref[...], w_ref[...]).astype(o_ref.dtype)


def norm_matmul(x, g, w, *, tm, tn, out_dtype, time_major_batch=None):
    T, D = x.shape
    N = w.shape[1]
    nj = N // tn
    if time_major_batch is None:
        out_shape = jax.ShapeDtypeStruct((T, N), out_dtype)
        out_spec = pl.BlockSpec((tm, tn), lambda i, j: (i, j))
    else:
        B, S = time_major_batch
        ns = S // tm
        out_shape = jax.ShapeDtypeStruct((S, B * N), out_dtype)
        out_spec = pl.BlockSpec((tm, tn), lambda i, j: (i % ns, (i // ns) * nj + j))
    return pl.pallas_call(
        _norm_matmul_kernel,
        grid=(T // tm, nj),
        in_specs=[pl.BlockSpec((tm, D), lambda i, j: (i, 0)),
                  pl.BlockSpec((1, D), lambda i, j: (0, 0)),
                  pl.BlockSpec((D, tn), lambda i, j: (0, j))],
        out_specs=out_spec,
        out_shape=out_shape,
        scratch_shapes=[pltpu.VMEM((tm, D), BF16)],
        compiler_params=_params("parallel", "arbitrary"),
        name="norm_matmul",
    )(x, g.reshape(1, D), w)


def _norm_matmul_rope_kernel(x_ref, g_ref, w_ref, cos_ref, sin_ref, o_ref, xn_ref, *, n_rope_tiles, dk):
    j = pl.program_id(1)

    @pl.when(j == 0)
    def _():
        xn_ref[...] = _rms(x_ref[...], g_ref[...]).astype(BF16)

    @pl.when(j < n_rope_tiles)
    def _():
        cos = cos_ref[...]
        sin = sin_ref[...]
        half = dk // 2
        xn = xn_ref[...]
        for h in range(o_ref.shape[1] // dk):
            lo, mid, hi = h * dk, h * dk + half, (h + 1) * dk
            y = _dot(xn, w_ref[:, lo:hi])
            x1, x2 = y[:, 0:half], y[:, half:dk]
            o_ref[:, lo:mid] = (x1 * cos - x2 * sin).astype(o_ref.dtype)
            o_ref[:, mid:hi] = (x2 * cos + x1 * sin).astype(o_ref.dtype)

    @pl.when(j >= n_rope_tiles)
    def _():
        o_ref[...] = _dot(xn_ref[...], w_ref[...]).astype(o_ref.dtype)


def norm_matmul_rope(x, g, w, cos, sin, *, tm, tn, rope_cols, dk, out_dtype):
    T, D = x.shape
    N = w.shape[1]
    assert rope_cols % tn == 0 and tn % dk == 0
    return pl.pallas_call(
        functools.partial(_norm_matmul_rope_kernel, n_rope_tiles=rope_cols // tn, dk=dk),
        grid=(T // tm, N // tn),
        in_specs=[pl.BlockSpec((tm, D), lambda i, j: (i, 0)),
                  pl.BlockSpec((1, D), lambda i, j: (0, 0)),
                  pl.BlockSpec((D, tn), lambda i, j: (0, j)),
                  pl.BlockSpec((tm, dk // 2), lambda i, j: (i, 0)),
                  pl.BlockSpec((tm, dk // 2), lambda i, j: (i, 0))],
        out_specs=pl.BlockSpec((tm, tn), lambda i, j: (i, j)),
        out_shape=jax.ShapeDtypeStruct((T, N), out_dtype),
        scratch_shapes=[pltpu.VMEM((tm, D), BF16)],
        compiler_params=_params("parallel", "arbitrary"),
        name="norm_matmul_rope",
    )(x, g.reshape(1, D), w, cos, sin)


def _norm_matmul_headnorm_kernel(x_ref, g_ref, w_ref, cos_ref, sin_ref, gain_ref, o_ref, xn_ref, *, block_scale):
    j = pl.program_id(1)
    per_tile = o_ref.shape[1] // LANES

    @pl.when(j == 0)
    def _():
        xn_ref[...] = _rms(x_ref[...], g_ref[...]).astype(BF16)

    y = _dot(xn_ref[...], w_ref[...])
    for jt in range(len(block_scale) // per_tile):
        @pl.when(j == jt)
        def _(jt=jt):
            for b in range(per_tile):
                c = jt * per_tile + b
                yb = y[:, b * LANES:(b + 1) * LANES]
                if block_scale[c] is not None:
                    yb = _rms(yb, gain_ref[c:c + 1, :])
                    yb = (yb * cos_ref[...] + pltpu.roll(yb, LANES // 2, axis=1) * sin_ref[...]) * block_scale[c]
                o_ref[:, b * LANES:(b + 1) * LANES] = yb.astype(o_ref.dtype)


def norm_matmul_headnorm(x, g, w, cos, sin, gains, block_scale, *, tm, tn, out_dtype):
    T, D = x.shape
    N = w.shape[1]
    assert N // LANES == len(block_scale) and tn % LANES == 0
    return pl.pallas_call(
        functools.partial(_norm_matmul_headnorm_kernel, block_scale=tuple(block_scale)),
        grid=(T // tm, N // tn),
        in_specs=[pl.BlockSpec((tm, D), lambda i, j: (i, 0)),
                  pl.BlockSpec((1, D), lambda i, j: (0, 0)),
                  pl.BlockSpec((D, tn), lambda i, j: (0, j)),
                  pl.BlockSpec((tm, LANES), lambda i, j: (i, 0)),
                  pl.BlockSpec((tm, LANES), lambda i, j: (i, 0)),
                  pl.BlockSpec(gains.shape, lambda i, j: (0, 0))],
        out_specs=pl.BlockSpec((tm, tn), lambda i, j: (i, j)),
        out_shape=jax.ShapeDtypeStruct((T, N), out_dtype),
        scratch_shapes=[pltpu.VMEM((tm, D), BF16)],
        compiler_params=_params("parallel", "arbitrary"),
        name="norm_matmul_headnorm",
    )(x, g.reshape(1, D), w, cos, sin, gains)


def _matmul_res_kernel(x_ref, w_ref, r_ref, o_ref):
    o_ref[...] = r_ref[...] + _dot(x_ref[...], w_ref[...])


def matmul_res(x, w, res, *, tm):
    T, K = x.shape
    N = w.shape[1]
    return pl.pallas_call(
        _matmul_res_kernel,
        grid=(T // tm,),
        in_specs=[pl.BlockSpec((tm, K), lambda i: (i, 0)),
                  pl.BlockSpec((K, N), lambda i: (0, 0)),
                  pl.BlockSpec((tm, N), lambda i: (i, 0))],
        out_specs=pl.BlockSpec((tm, N), lambda i: (i, 0)),
        out_shape=jax.ShapeDtypeStruct((T, N), F32),
        compiler_params=_params("parallel"),
        name="matmul_res",
    )(x, w, res)


def _retention_kernel(q_ref, k_ref, v_ref, g_ref, dmask_ref, qdec_ref, kdec_ref, cdec_ref, gn_ref, o_ref, r_scr):
    H, dk, dv = r_scr.shape
    C = dmask_ref.shape[1]

    @pl.when(pl.program_id(1) == 0)
    def _():
        r_scr[...] = jnp.zeros_like(r_scr)

    for c in range(q_ref.shape[0] // C):
        rows = slice(c * C, (c + 1) * C)
        for h in range(H):
            qb = q_ref[rows, h * dk:(h + 1) * dk]
            kb = k_ref[rows, h * dk:(h + 1) * dk]
            v = v_ref[rows, h * dv:(h + 1) * dv]
            scores = _dot_nt(qb, kb) * dmask_ref[h]
            intra = _dot(scores.astype(BF16), v)
            r_old = r_scr[h]
            cross = _dot(qb, r_old.astype(BF16)) * qdec_ref[h]
            kd_t = (kb.astype(F32) * kdec_ref[h]).T.astype(BF16)
            r_scr[h] = r_old * cdec_ref[h] + _dot(kd_t, v)

            o = intra + cross
            mu = jnp.mean(o, axis=-1, keepdims=True)
            oc = o - mu
            var = jnp.mean(oc * oc, axis=-1, keepdims=True)
            on = oc * lax.rsqrt(var + NORM_EPS) * gn_ref[:, h * dv:(h + 1) * dv]
            g = g_ref[rows, h * dv:(h + 1) * dv].astype(F32)
            o_ref[rows, h * dv:(h + 1) * dv] = (on * (g * jax.nn.sigmoid(g))).astype(o_ref.dtype)


def retention_core(proj, gn, B, S):
    H, C = RET_HEADS, RET_CHUNK
    T = proj.shape[0]
    dk = proj.shape[1] // (6 * H)
    dv = 2 * dk
    tb = min(RET_BLOCK, S)
    nb = S // tb
    scale = dk ** -0.5
    log_gamma = jnp.log1p(-jnp.exp2(-5.0 - jnp.arange(H, dtype=F32)))
    idx = jnp.arange(C, dtype=F32)
    diff = idx[:, None] - idx[None, :]
    dmask = jnp.where(diff >= 0, jnp.exp(jnp.maximum(diff, 0.0) * log_gamma[:, None, None]), 0.0) * scale
    qdec = jnp.exp((idx + 1.0) * log_gamma[:, None])[..., None]
    kdec = jnp.exp((C - 1.0 - idx) * log_gamma[:, None])[..., None] * scale
    cdec = jnp.exp(C * log_gamma).reshape(H, 1, 1)
    whole = lambda shape: pl.BlockSpec(shape, lambda b, n: (0,) * len(shape))
    return pl.pallas_call(
        _retention_kernel,
        grid=(B, nb),
        in_specs=[pl.BlockSpec((tb, H * dk), lambda b, n: (b * nb + n, 0)),
                  pl.BlockSpec((tb, H * dk), lambda b, n: (b * nb + n, 1)),
                  pl.BlockSpec((tb, H * dv), lambda b, n: (b * nb + n, 1)),
                  pl.BlockSpec((tb, H * dv), lambda b, n: (b * nb + n, 2)),
                  whole((H, C, C)), whole((H, C, 1)), whole((H, C, 1)), whole((H, 1, 1)),
                  whole((1, H * dv))],
        out_specs=pl.BlockSpec((tb, H * dv), lambda b, n: (b * nb + n, 0)),
        out_shape=jax.ShapeDtypeStruct((T, H * dv), BF16),
        scratch_shapes=[pltpu.VMEM((H, dk, dv), F32)],
        compiler_params=_params("parallel", "arbitrary"),
        name="retention",
    )(proj, proj, proj, proj, dmask, qdec, kdec, cdec, gn.reshape(1, H * dv))


def _nsa_compress_kernel(kc_ref, vc_ref, pe_ref, w1_ref, w2_ref, kn_ref, cos_ref, sin_ref,
                         kco_ref, vco_ref, xs_scr):
    nh = NSA_CMP_LEN // NSA_CMP_STRIDE
    n_rows = xs_scr.shape[0] // NSA_CMP_STRIDE
    for br, (src, dst) in enumerate(((kc_ref, kco_ref), (vc_ref, vco_ref))):
        xs_scr[...] = src[...].astype(F32)
        acc = [jnp.zeros((n_rows, w1_ref.shape[-1]), F32) for _ in range(nh)]
        for l in range(NSA_CMP_STRIDE):
            piece = xs_scr[pl.ds(l, n_rows, stride=NSA_CMP_STRIDE), :]
            for a in range(nh):
                ll = a * NSA_CMP_STRIDE + l
                acc[a] = acc[a] + _dot((piece + pe_ref[br, ll:ll + 1, :]).astype(BF16), w1_ref[br, ll])
        hid = acc[0] + pltpu.roll(acc[1], n_rows - 1, axis=0)
        z = _dot(_gelu_tanh(hid).astype(BF16), w2_ref[br])
        if br == 0:
            z = _rms(z, kn_ref[0:1, :])
            cos = cos_ref[pl.ds(0, n_rows, stride=NSA_CMP_STRIDE), :]
            sin = sin_ref[pl.ds(0, n_rows, stride=NSA_CMP_STRIDE), :]
            z = z * cos + pltpu.roll(z, NSA_DH // 2, axis=1) * sin
        dst[0, 0] = z.astype(dst.dtype)


def nsa_compress(proj, pe, w1, w2, kn, cosn, sinn, B, S):
    G, dh = NSA_GROUPS, NSA_DH
    nr = S // NSA_CMP_STRIDE
    kc0 = NSA_HEADS
    vc0 = NSA_HEADS + G
    return pl.pallas_call(
        _nsa_compress_kernel,
        grid=(B, G),
        in_specs=[pl.BlockSpec((S, dh), lambda b, g: (b, kc0 + g)),
                  pl.BlockSpec((S, dh), lambda b, g: (b, vc0 + g)),
                  pl.BlockSpec(pe.shape, lambda b, g: (0, 0, 0)),
                  pl.BlockSpec(w1.shape, lambda b, g: (0, 0, 0, 0)),
                  pl.BlockSpec(w2.shape, lambda b, g: (0, 0, 0)),
                  pl.BlockSpec(kn.shape, lambda b, g: (0, 0)),
                  pl.BlockSpec((S, dh), lambda b, g: (b, 0)),
                  pl.BlockSpec((S, dh), lambda b, g: (b, 0))],
        out_specs=[pl.BlockSpec((1, 1, nr, dh), lambda b, g: (b, g, 0, 0))] * 2,
        out_shape=[jax.ShapeDtypeStruct((B, G, nr, dh), BF16)] * 2,
        scratch_shapes=[pltpu.VMEM((S, dh), F32)],
        compiler_params=_params("parallel", "parallel"),
        name="nsa_compress",
    )(proj, proj, pe, w1, w2, kn, cosn, sinn)


def _softmax2_parts(s, bias, hg):
    tq, n = bias.shape
    s3 = s.reshape(hg, tq, n) + bias[None]
    p = jnp.exp2(s3 - jnp.max(s3, axis=-1, keepdims=True))
    return p.reshape(hg * tq, n), jnp.sum(p, axis=-1, keepdims=True).reshape(hg * tq, 1)


def _nsa_attn_kernel(q_ref, gate_ref, kc_ref, vc_ref, ks_ref, vs_ref, kw_ref, vw_ref, mmat_ref, o_ref):
    grp = pl.program_id(1)
    qi = pl.program_id(2)
    tq = q_ref.shape[0]
    dh = NSA_DH
    hg = q_ref.shape[1] // dh
    S = ks_ref.shape[0]
    tk = min(NSA_KV_TILE, S)
    q0 = qi * tq

    qall = jnp.concatenate([q_ref[:, h * dh:(h + 1) * dh] for h in range(hg)], axis=0)

    def tok(n):
        return q0 + lax.broadcasted_iota(jnp.int32, (tq, n), 0)

    def key(n):
        return lax.broadcasted_iota(jnp.int32, (tq, n), 1)

    ncp = kc_ref.shape[2]
    valid_c = key(ncp) * NSA_CMP_STRIDE + (NSA_CMP_LEN - 1) <= tok(ncp)
    p_c, l_c = _softmax2_parts(_dot_nt(qall, kc_ref[0, 0]), jnp.where(valid_c, 0.0, MASK_NEG), hg)
    t_row = q0 + jnp.bitwise_and(lax.broadcasted_iota(jnp.int32, (hg * tq, 1), 0), tq - 1)
    p_c = jnp.where(t_row >= NSA_CMP_LEN - 1, p_c / l_c, 0.0)
    o_c = _dot(p_c.astype(BF16), vc_ref[0, 0])
    p_sum = p_c[0:tq]
    for h in range(1, hg):
        p_sum = p_sum + p_c[h * tq:(h + 1) * tq]
    imp = jnp.dot(p_sum, mmat_ref[...], preferred_element_type=F32, precision=lax.Precision.HIGHEST)

    n_sel = S // NSA_SEL_LEN
    sel_rows = ((n_sel + 7) // 8) * 8
    v_imp = imp.T[0:sel_rows, :]
    jb = lax.broadcasted_iota(jnp.int32, (sel_rows, tq), 0)
    cur = (q0 + lax.broadcasted_iota(jnp.int32, (sel_rows, tq), 1)) // NSA_SEL_LEN
    forced = (jb == 0) | (jb == cur) | (jb == cur - 1)
    v_imp = jnp.where(forced, jnp.inf, jnp.where(jb > cur, -jnp.inf, v_imp))
    if sel_rows > n_sel:
        v_imp = jnp.where(jb >= n_sel, -jnp.inf, v_imp)
    rank = jnp.zeros((sel_rows, tq), jnp.int32)
    for i in range(n_sel):
        r = v_imp[i:i + 1, :]
        beats = (r > v_imp) | ((r == v_imp) & (jb > i))
        rank = rank + beats.astype(jnp.int32)
    sel_t = ((rank < min(NSA_TOP_N, n_sel)) & (jb < n_sel)).astype(F32)
    if sel_rows < LANES:
        sel_t = jnp.concatenate([sel_t, jnp.zeros((LANES - sel_rows, tq), F32)], axis=0)
    sel = sel_t.T.astype(BF16)

    row_t = tok(tk)
    lane_k = key(tk)
    e_row = lax.broadcasted_iota(jnp.int32, (LANES, tk), 0)
    e_blk = lax.broadcasted_iota(jnp.int32, (LANES, tk), 1) // NSA_SEL_LEN

    def sel_step(j, carry):
        m, l, acc = carry
        base = pl.multiple_of(j * tk, tk)
        s = _dot_nt(qall, ks_ref[pl.ds(base, tk), :])
        expand = (e_blk + j * (tk // NSA_SEL_LEN) == e_row).astype(BF16)
        chosen = _dot(sel, expand)
        bias = jnp.where((chosen > 0.5) & (lane_k + base <= row_t), 0.0, MASK_NEG)
        s3 = s.reshape(hg, tq, tk) + bias[None]
        m_new = jnp.maximum(m, jnp.max(s3, axis=-1, keepdims=True))
        alpha = jnp.exp2(m - m_new)
        p = jnp.exp2(s3 - m_new)
        l = alpha * l + jnp.sum(p, axis=-1, keepdims=True)
        pv = _dot(p.reshape(hg * tq, tk).astype(BF16), vs_ref[pl.ds(base, tk), :])
        acc = alpha.reshape(hg * tq, 1) * acc + pv
        return m_new, l, acc

    n_kv = (q0 + tq + tk - 1) // tk
    init = (jnp.full((hg, tq, 1), MASK_NEG, F32), jnp.zeros((hg, tq, 1), F32), jnp.zeros((hg * tq, dh), F32))
    _, l_s, acc_s = lax.fori_loop(0, n_kv, sel_step, init)
    o_s = acc_s / l_s.reshape(hg * tq, 1)

    span = min(tq + NSA_WINDOW, S)
    start = pl.multiple_of(jnp.maximum(q0 + tq - span, 0), tq)
    dist = tok(span) - (start + key(span))
    bias_w = jnp.where((dist >= 0) & (dist < NSA_WINDOW), 0.0, MASK_NEG)
    p_w, l_w = _softmax2_parts(_dot_nt(qall, kw_ref[pl.ds(start, span), :]), bias_w, hg)
    o_w = _dot(p_w.astype(BF16), vw_ref[pl.ds(start, span), :]) / l_w

    gsig = jax.nn.sigmoid(gate_ref[...].astype(F32))
    n_br = 3
    for h in range(hg):
        def gate_col(br):
            lo = h * n_br + br
            hi = lo + hg * n_br
            return jnp.where(grp == 0, gsig[:, lo:lo + 1], gsig[:, hi:hi + 1])
        rows = slice(h * tq, (h + 1) * tq)
        o = gate_col(0) * o_c[rows] + gate_col(1) * o_s[rows] + gate_col(2) * o_w[rows]
        o_ref[:, h * dh:(h + 1) * dh] = o.astype(o_ref.dtype)


def nsa_attention(proj, kcc, vcc, B, S):
    H, G, dh = NSA_HEADS, NSA_GROUPS, NSA_DH
    hg = H // G
    T = proj.shape[0]
    tq = min(NSA_Q_TILE, S)
    nq = S // tq
    nr = kcc.shape[2]
    n_c = (S - NSA_CMP_LEN) // NSA_CMP_STRIDE + 1
    n_sel = S // NSA_SEL_LEN
    cs = (np.arange(n_c) * NSA_CMP_STRIDE)[:, None]
    js = (np.arange(n_sel) * NSA_SEL_LEN)[None, :]
    overlap = np.clip(np.minimum(cs + NSA_CMP_LEN, js + NSA_SEL_LEN) - np.maximum(cs, js), 0, None) / NSA_CMP_LEN
    mmat = np.zeros((nr, LANES), np.float32)
    mmat[:n_c, :n_sel] = overlap
    col = lambda base: (lambda b, g, i: (b, base + g))
    return pl.pallas_call(
        _nsa_attn_kernel,
        grid=(B, G, nq),
        in_specs=[pl.BlockSpec((tq, hg * dh), lambda b, g, i: (b * nq + i, g)),
                  pl.BlockSpec((tq, LANES), lambda b, g, i: (b * nq + i, H + 6 * G)),
                  pl.BlockSpec((1, 1, nr, dh), lambda b, g, i: (b, g, 0, 0)),
                  pl.BlockSpec((1, 1, nr, dh), lambda b, g, i: (b, g, 0, 0)),
                  pl.BlockSpec((S, dh), col(H + 2 * G)),
                  pl.BlockSpec((S, dh), col(H + 3 * G)),
                  pl.BlockSpec((S, dh), col(H + 4 * G)),
                  pl.BlockSpec((S, dh), col(H + 5 * G)),
                  pl.BlockSpec(mmat.shape, lambda b, g, i: (0, 0))],
        out_specs=pl.BlockSpec((tq, hg * dh), lambda b, g, i: (b * nq + i, g)),
        out_shape=jax.ShapeDtypeStruct((T, H * dh), BF16),
        compiler_params=_params("parallel", "parallel", "arbitrary"),
        name="nsa_attention",
    )(proj, proj, kcc, vcc, proj, proj, proj, proj, jnp.asarray(mmat))


def _s5_scan_kernel(u_ref, bbd_ref, are_ref, aim_ref, cbd_ref, d_ref, y_ref, bu_scr, xs_scr, st_scr):
    nb = st_scr.shape[0] // 2
    w = are_ref.shape[-1]
    tc = u_ref.shape[0] // nb

    @pl.when(pl.program_id(1) == 0)
    def _():
        st_scr[...] = jnp.zeros_like(st_scr)

    u = u_ref[...]
    bu_scr[...] = _dot(u, bbd_ref[0])
    a_re = jnp.broadcast_to(are_ref[0], (nb, w))
    a_im = jnp.broadcast_to(aim_ref[0], (nb, w))

    def step(t, carry):
        xr, xi = carry
        r0 = pl.multiple_of(t * nb, nb)
        nxr = a_re * xr - a_im * xi + bu_scr[pl.ds(r0, nb), 0:w]
        nxi = a_re * xi + a_im * xr + bu_scr[pl.ds(r0, nb), w:2 * w]
        xs_scr[pl.ds(r0, nb), 0:w] = nxr.astype(BF16)
        xs_scr[pl.ds(r0, nb), w:2 * w] = nxi.astype(BF16)
        return nxr, nxi

    xr, xi = lax.fori_loop(0, tc, step, (st_scr[0:nb, :], st_scr[nb:2 * nb, :]), unroll=4)
    st_scr[0:nb, :] = xr
    st_scr[nb:2 * nb, :] = xi
    y = _dot(xs_scr[...], cbd_ref[0]) + d_ref[...] * u.astype(F32)
    y_ref[...] = _gelu_tanh(y).astype(y_ref.dtype)


def s5_scan(u_tm, a_re, a_im, b_re, b_im, c_re, c_im, d_skip, log_dt, B, S):
    D = u_tm.shape[1]
    Cg, P, NS = S5_GROUP, S5_STATE, S5_SET
    G = D // Cg
    K = G // NS
    w = NS * P
    tc = min(S5_TIME_CHUNK, S)
    dt = jnp.exp(log_dt.astype(F32))[:, None]
    mag = jnp.exp(dt * a_re)
    abar_re, abar_im = mag * jnp.cos(dt * a_im), mag * jnp.sin(dt * a_im)
    den = a_re * a_re + a_im * a_im
    nr_, ni_ = abar_re - 1.0, abar_im
    f_re = (nr_ * a_re + ni_ * a_im) / den
    f_im = (ni_ * a_re - nr_ * a_im) / den
    bbar_re = f_re[..., None] * b_re - f_im[..., None] * b_im
    bbar_im = f_re[..., None] * b_im + f_im[..., None] * b_re
    eye = jnp.eye(NS, dtype=F32)

    def in_blockdiag(bb):
        t = bb.reshape(K, NS, P, Cg).transpose(0, 1, 3, 2)
        return jnp.einsum('kgcp,gh->kgchp', t, eye).reshape(K, NS * Cg, NS * P)

    def out_blockdiag(cc):
        t = cc.reshape(K, NS, Cg, P).transpose(0, 1, 3, 2)
        return jnp.einsum('kgpc,gh->kgphc', t, eye).reshape(K, NS * P, NS * Cg)

    bbd = jnp.concatenate([in_blockdiag(bbar_re), in_blockdiag(bbar_im)], axis=-1).astype(BF16)
    cbd = jnp.concatenate([out_blockdiag(c_re), -out_blockdiag(c_im)], axis=1).astype(BF16)
    are = abar_re.reshape(K, 1, w)
    aim = abar_im.reshape(K, 1, w)
    rows = tc * B
    return pl.pallas_call(
        _s5_scan_kernel,
        grid=(K, S // tc),
        in_specs=[pl.BlockSpec((rows, LANES), lambda k, c: (c, k)),
                  pl.BlockSpec((1, LANES, 2 * w), lambda k, c: (k, 0, 0)),
                  pl.BlockSpec((1, 1, w), lambda k, c: (k, 0, 0)),
                  pl.BlockSpec((1, 1, w), lambda k, c: (k, 0, 0)),
                  pl.BlockSpec((1, 2 * w, LANES), lambda k, c: (k, 0, 0)),
                  pl.BlockSpec((1, LANES), lambda k, c: (0, k))],
        out_specs=pl.BlockSpec((rows, LANES), lambda k, c: (c, k)),
        out_shape=jax.ShapeDtypeStruct((S * B, D), BF16),
        scratch_shapes=[pltpu.VMEM((rows, 2 * w), F32), pltpu.VMEM((rows, 2 * w), BF16),
                        pltpu.VMEM((2 * B, w), F32)],
        compiler_params=_params("parallel", "arbitrary"),
        name="s5_scan",
    )(u_tm, bbd, are, aim, cbd, d_skip.reshape(1, D))


def _glu_res_kernel(y_ref, wa_ref, wb_ref, r_ref, o_ref):
    y = y_ref[...]
    a = _dot(y, wa_ref[...])
    b = _dot(y, wb_ref[...])
    o_ref[...] = r_ref[...] + a * jax.nn.sigmoid(b)


def glu_res(y_tm, w, res, B, S, *, tm):
    D = w.shape[0]
    N = w.shape[1] // 2
    ns = S // tm
    return pl.pallas_call(
        _glu_res_kernel,
        grid=(B * ns,),
        in_specs=[pl.BlockSpec((tm, D), lambda i: (i % ns, i // ns)),
                  pl.BlockSpec((D, N), lambda i: (0, 0)),
                  pl.BlockSpec((D, N), lambda i: (0, 1)),
                  pl.BlockSpec((tm, N), lambda i: (i, 0))],
        out_specs=pl.BlockSpec((tm, N), lambda i: (i, 0)),
        out_shape=jax.ShapeDtypeStruct((B * S, N), F32),
        compiler_params=_params("parallel"),
        name="glu_res",
    )(y_tm, w, w, res)


def _ple(h, g_ref, p_ref, wgate_ref, wproj_ref):
    gate = jax.nn.sigmoid(_dot(_rms(h, g_ref[...]).astype(BF16), wgate_ref[...]))
    return h + gate * _dot(p_ref[...].astype(BF16), wproj_ref[...])


def _ffn_kernel(x_ref, g_ref, wg_ref, wu_ref, wd_ref, gp_ref, p_ref, wgate_ref, wproj_ref, o_ref,
                xn_scr, acc_scr):
    f = pl.program_id(1)

    @pl.when(f == 0)
    def _():
        xn_scr[...] = _rms(x_ref[...], g_ref[...]).astype(BF16)
        acc_scr[...] = jnp.zeros_like(acc_scr)

    xn = xn_scr[...]
    gg = _dot(xn, wg_ref[...])
    uu = _dot(xn, wu_ref[...])
    acc_scr[...] += _dot((gg * jax.nn.sigmoid(gg) * uu).astype(BF16), wd_ref[...])

    @pl.when(f == pl.num_programs(1) - 1)
    def _():
        o_ref[...] = _ple(x_ref[...] + acc_scr[...], gp_ref, p_ref, wgate_ref, wproj_ref)


def swiglu_ffn_ple(x, g, w_gate_up, w_down, g_ple, p_all, layer, w_pgate, w_pproj, *, tm, tf):
    T, D = x.shape
    F = w_down.shape[0]
    PD = p_all.shape[2]
    nf = F // tf
    return pl.pallas_call(
        _ffn_kernel,
        grid=(T // tm, nf),
        in_specs=[pl.BlockSpec((tm, D), lambda i, f: (i, 0)),
                  pl.BlockSpec((1, D), lambda i, f: (0, 0)),
                  pl.BlockSpec((D, tf), lambda i, f: (0, f)),
                  pl.BlockSpec((D, tf), lambda i, f: (0, nf + f)),
                  pl.BlockSpec((tf, D), lambda i, f: (f, 0)),
                  pl.BlockSpec((1, D), lambda i, f: (0, 0)),
                  pl.BlockSpec((None, tm, PD), lambda i, f: (layer, i, 0)),
                  pl.BlockSpec((D, D), lambda i, f: (0, 0)),
                  pl.BlockSpec((PD, D), lambda i, f: (0, 0))],
        out_specs=pl.BlockSpec((tm, D), lambda i, f: (i, 0)),
        out_shape=jax.ShapeDtypeStruct((T, D), F32),
        scratch_shapes=[pltpu.VMEM((tm, D), BF16), pltpu.VMEM((tm, D), F32)],
        compiler_params=_params("parallel", "arbitrary"),
        name="swiglu_ffn_ple",
    )(x, g.reshape(1, D), w_gate_up, w_gate_up, w_down, g_ple.reshape(1, D), p_all, w_pgate, w_pproj)


META_E0, META_E1, META_W0, META_W1, META_R0, META_R1 = range(6)


def _router_kernel(x_ref, g_ref, w_ref, meta_ref, cnt_ref, carry_scr, tri_scr):
    @pl.when(pl.program_id(0) == 0)
    def _():
        carry_scr[...] = jnp.zeros_like(carry_scr)
        n = tri_scr.shape[0]
        tri = lax.broadcasted_iota(jnp.int32, (n, n), 0) > lax.broadcasted_iota(jnp.int32, (n, n), 1)
        tri_scr[...] = tri.astype(BF16)

    xn = _rms(x_ref[...], g_ref[...])
    x_hi = xn.astype(BF16)
    x_lo = (xn - x_hi.astype(F32)).astype(BF16)
    w = w_ref[...]
    w_hi = w.astype(BF16)
    w_lo = (w - w_hi.astype(F32)).astype(BF16)
    logits = _dot(x_hi, w_hi) + (_dot(x_hi, w_lo) + _dot(x_lo, w_hi))
    lane = lax.broadcasted_iota(jnp.int32, logits.shape, 1).astype(F32)
    logits = jnp.where(lane < N_EXPERTS, logits, -jnp.inf)

    def top1(v):
        m = jnp.max(v, axis=-1, keepdims=True)
        idx = jnp.min(jnp.where(v == m, lane, float(LANES)), axis=-1, keepdims=True)
        return m, idx, lane == idx

    m1, e0, hot1 = top1(logits)
    m2, e1, hot2 = top1(jnp.where(hot1, -jnp.inf, logits))
    ex = jnp.exp(m2 - m1)
    w0 = 1.0 / (1.0 + ex)
    w1 = ex * w0

    both = jnp.where(hot1 | hot2, 1.0, 0.0)
    before = _dot(tri_scr[...], both.astype(BF16)) + carry_scr[0:1, :]
    r0 = jnp.sum(jnp.where(hot1, before, 0.0), axis=-1, keepdims=True)
    r1 = jnp.sum(jnp.where(hot2, before, 0.0), axis=-1, keepdims=True)
    carry_scr[...] = carry_scr[...] + jnp.sum(both, axis=0, keepdims=True)
    cnt_ref[...] = carry_scr[...]

    meta = jnp.zeros_like(logits)
    for k, v in ((META_E0, e0), (META_E1, e1), (META_W0, w0), (META_W1, w1), (META_R0, r0), (META_R1, r1)):
        meta = jnp.where(lane == float(k), v, meta)
    meta_ref[...] = meta


def router_topk(x, g, w_router, *, tm):
    T, D = x.shape
    w = jnp.zeros((D, LANES), F32).at[:, :N_EXPERTS].set(w_router)
    return pl.pallas_call(
        _router_kernel,
        grid=(T // tm,),
        in_specs=[pl.BlockSpec((tm, D), lambda i: (i, 0)),
                  pl.BlockSpec((1, D), lambda i: (0, 0)),
                  pl.BlockSpec((D, LANES), lambda i: (0, 0))],
        out_specs=[pl.BlockSpec((tm, LANES), lambda i: (i, 0)),
                   pl.BlockSpec((8, LANES), lambda i: (0, 0))],
        out_shape=[jax.ShapeDtypeStruct((T, LANES), F32), jax.ShapeDtypeStruct((8, LANES), F32)],
        scratch_shapes=[pltpu.VMEM((8, LANES), F32), pltpu.VMEM((tm, tm), BF16)],
        compiler_params=_params("arbitrary"),
        name="router",
    )(x, g.reshape(1, D), w)


def _start_row_gathers(idx_ref, src_hbm, dst_ref, sem, lo, hi, idx_off=0):
    for r in range(lo, hi):
        pltpu.make_async_copy(src_hbm.at[pl.ds(idx_ref[0, 0, idx_off + r], 1), :], dst_ref.at[pl.ds(r, 1), :],
                              sem).start()


def _wait_row_gathers(src_hbm, dst_ref, sem):
    pltpu.make_async_copy(src_hbm.at[pl.ds(0, dst_ref.shape[0]), :], dst_ref, sem).wait()


def _moe_expert_kernel(te_ref, nu_ref, tok_ref, tok_next_ref, x_hbm, g_ref, wg_ref, wu_ref, wd_ref, y_ref,
                       xg_scr, xn_scr, acc_scr, sems):
    i = pl.program_id(0)
    f = pl.program_id(1)
    n_tiles = pl.num_programs(0)
    nf = pl.num_programs(1)
    n_used = nu_ref[0]
    active = i < n_used
    tm = xn_scr.shape[0]
    cur = xg_scr.at[i % 2]
    nxt = xg_scr.at[(i + 1) % 2]
    sem_cur = sems.at[i % 2]
    sem_nxt = sems.at[(i + 1) % 2]

    @pl.when((i == 0) & (f == 0))
    def _():
        def issue(r, carry):
            pltpu.make_async_copy(x_hbm.at[pl.ds(tok_ref[0, 0, r], 1), :], cur.at[pl.ds(r, 1), :], sem_cur).start()
            return carry
        lax.fori_loop(0, tm, issue, 0, unroll=8)

    @pl.when((f == 0) & (i <= n_used))
    def _():
        _wait_row_gathers(x_hbm, cur, sem_cur)

    @pl.when(active & (f == 0))
    def _():
        xn_scr[...] = _rms(cur[...], g_ref[...]).astype(BF16)
        acc_scr[...] = jnp.zeros_like(acc_scr)

    def step(lo, hi):
        xn = xn_scr[...]
        gg = _dot(xn, wg_ref[0].astype(BF16))
        uu = _dot(xn, wu_ref[0].astype(BF16))
        acc_scr[...] += _dot((gg * jax.nn.sigmoid(gg) * uu).astype(BF16), wd_ref[0].astype(BF16))
        _start_row_gathers(tok_next_ref, x_hbm, nxt, sem_nxt, lo, hi)

    per = -(-tm // MOE_NF)
    for fs in range(MOE_NF):
        @pl.when(active & (f == fs))
        def _(fs=fs):
            step(min(fs * per, tm), min((fs + 1) * per, tm))

    @pl.when(f == nf - 1)
    def _():
        y_ref[...] = jnp.where(active, acc_scr[...], 0.0)

    @pl.when(active & (i == n_tiles - 1) & (f == nf - 1))
    def _():
        _wait_row_gathers(x_hbm, nxt, sem_nxt)


def moe_expert_ffn(x, g, tok_of_slot, tile_expert, n_used, w_gate_up, w_down, layer, *, tm):
    T, D = x.shape
    _, E, F, _ = w_down.shape
    nf = MOE_NF
    tf = F // nf
    n_tiles = tok_of_slot.shape[0]

    def fblk(i, f, nu):
        return jnp.where(i < nu[0], f, nf - 1)

    tok_spec = lambda off: pl.BlockSpec((1, 1, tm), lambda i, f, te, nu: (jnp.minimum(i + off, n_tiles - 1), 0, 0),
                                        memory_space=pltpu.SMEM)
    grid_spec = pltpu.PrefetchScalarGridSpec(
        num_scalar_prefetch=2,
        grid=(n_tiles, nf),
        in_specs=[tok_spec(0), tok_spec(1),
                  pl.BlockSpec(memory_space=pl.ANY),
                  pl.BlockSpec((1, D), lambda i, f, te, nu: (0, 0)),
                  pl.BlockSpec((None, 1, D, tf), lambda i, f, te, nu: (layer, te[i], 0, fblk(i, f, nu))),
                  pl.BlockSpec((None, 1, D, tf), lambda i, f, te, nu: (layer, te[i], 0, nf + fblk(i, f, nu))),
                  pl.BlockSpec((None, 1, tf, D), lambda i, f, te, nu: (layer, te[i], fblk(i, f, nu), 0))],
        out_specs=pl.BlockSpec((tm, D), lambda i, f, te, nu: (i, 0)),
        scratch_shapes=[pltpu.VMEM((2, tm, D), F32), pltpu.VMEM((tm, D), BF16), pltpu.VMEM((tm, D), F32),
                        pltpu.SemaphoreType.DMA((2,))],
    )
    return pl.pallas_call(
        _moe_expert_kernel,
        grid_spec=grid_spec,
        out_shape=jax.ShapeDtypeStruct((n_tiles * tm, D), F32),
        compiler_params=_params("arbitrary", "arbitrary"),
        name="moe_experts",
    )(tile_expert, n_used, tok_of_slot, tok_of_slot, x, g.reshape(1, D), w_gate_up, w_gate_up, w_down)


def _moe_combine_kernel(s0_ref, s1_ref, s0n_ref, s1n_ref, x_ref, meta_ref, y_hbm, gp_ref, p_ref, wgate_ref,
                        wproj_ref, o_ref, ya0, ya1, yb0, yb1, sems):
    i = pl.program_id(0)
    n = ya0.shape[0]

    def gather(idx0, idx1, buf0, buf1, k, off):
        _start_row_gathers(idx0, y_hbm, buf0, sems.at[k, 0], 0, n, off)
        _start_row_gathers(idx1, y_hbm, buf1, sems.at[k, 1], 0, n, off)

    def wait(buf0, buf1, k):
        _wait_row_gathers(y_hbm, buf0, sems.at[k, 0])
        _wait_row_gathers(y_hbm, buf1, sems.at[k, 1])

    def combine(buf0, buf1, rows):
        meta = meta_ref[rows, :]
        w0 = meta[:, META_W0:META_W0 + 1]
        w1 = meta[:, META_W1:META_W1 + 1]
        h = x_ref[rows, :] + w0 * buf0[...] + w1 * buf1[...]
        gate = jax.nn.sigmoid(_dot(_rms(h, gp_ref[...]).astype(BF16), wgate_ref[...]))
        o_ref[rows, :] = h + gate * _dot(p_ref[rows, :].astype(BF16), wproj_ref[...])

    @pl.when(i == 0)
    def _():
        gather(s0_ref, s1_ref, ya0, ya1, 0, 0)

    wait(ya0, ya1, 0)
    gather(s0_ref, s1_ref, yb0, yb1, 1, n)
    combine(ya0, ya1, slice(0, n))
    wait(yb0, yb1, 1)
    gather(s0n_ref, s1n_ref, ya0, ya1, 0, 0)
    combine(yb0, yb1, slice(n, 2 * n))

    @pl.when(i == pl.num_programs(0) - 1)
    def _():
        wait(ya0, ya1, 0)


def moe_combine_ple(x, meta, slot0, slot1, y, g_ple, p_all, layer, w_pgate, w_pproj, *, tm):
    T, D = x.shape
    PD = p_all.shape[2]
    nt = T // tm
    idx = lambda off: pl.BlockSpec((1, 1, tm), lambda i: (jnp.minimum(i + off, nt - 1), 0, 0),
                                   memory_space=pltpu.SMEM)
    return pl.pallas_call(
        _moe_combine_kernel,
        grid=(nt,),
        in_specs=[idx(0), idx(0), idx(1), idx(1),
                  pl.BlockSpec((tm, D), lambda i: (i, 0)),
                  pl.BlockSpec((tm, LANES), lambda i: (i, 0)),
                  pl.BlockSpec(memory_space=pl.ANY),
                  pl.BlockSpec((1, D), lambda i: (0, 0)),
                  pl.BlockSpec((None, tm, PD), lambda i: (layer, i, 0)),
                  pl.BlockSpec((D, D), lambda i: (0, 0)),
                  pl.BlockSpec((PD, D), lambda i: (0, 0))],
        out_specs=pl.BlockSpec((tm, D), lambda i: (i, 0)),
        out_shape=jax.ShapeDtypeStruct((T, D), F32),
        scratch_shapes=[pltpu.VMEM((tm // 2, D), F32)] * 4 + [pltpu.SemaphoreType.DMA((2, 2))],
        compiler_params=_params("arbitrary"),
        name="moe_combine_ple",
    )(slot0, slot1, slot0, slot1, x, meta, y, g_ple.reshape(1, D), p_all, w_pgate, w_pproj)


def moe_layer(x, g, w_router, w_gate_up, w_down, layer, ple, *, tm_route, tm_expert, tm_combine):
    T, D = x.shape
    E = w_down.shape[1]
    n_tiles = (2 * T) // tm_expert + E
    meta, cnt = router_topk(x, g, w_router, tm=tm_route)
    e0 = meta[:, META_E0].astype(jnp.int32)
    e1 = meta[:, META_E1].astype(jnp.int32)
    r0 = meta[:, META_R0].astype(jnp.int32)
    r1 = meta[:, META_R1].astype(jnp.int32)
    counts = cnt[0, :E].astype(jnp.int32)
    tiles_per = (counts + tm_expert - 1) // tm_expert
    tile_end = jnp.cumsum(tiles_per)
    group_start = (tile_end - tiles_per) * tm_expert
    slot0 = group_start[e0] + r0
    slot1 = group_start[e1] + r1
    n_used = tile_end[-1:]
    tile_ids = jnp.arange(n_tiles, dtype=jnp.int32)
    tile_expert = jnp.searchsorted(tile_end, jnp.minimum(tile_ids, n_used[0] - 1), side="right").astype(jnp.int32)
    tile_expert = jnp.minimum(tile_expert, E - 1)
    tok = jnp.arange(T, dtype=jnp.int32)
    tok_of_slot = jnp.zeros((n_tiles * tm_expert,), jnp.int32).at[slot0].set(tok).at[slot1].set(tok)
    slot0 = slot0.reshape(T // tm_combine, 1, tm_combine)
    slot1 = slot1.reshape(T // tm_combine, 1, tm_combine)
    y = moe_expert_ffn(x, g, tok_of_slot.reshape(n_tiles, 1, tm_expert), tile_expert, n_used.astype(jnp.int32),
                       w_gate_up, w_down, layer, tm=tm_expert)
    return moe_combine_ple(x, meta, slot0, slot1, y, *ple, tm=tm_combine)


def _row_tile(S, want):
    return min(want, S)


def retention_layer(h, tabs, B, S, g_norm, w_in, gn, w_out):
    tm = _row_tile(S, 1024)
    dk = w_in.shape[1] // (6 * RET_HEADS)
    qk_cols = 2 * RET_HEADS * dk
    proj = norm_matmul_rope(h, g_norm, w_in.astype(BF16), tabs["ret_cos"], tabs["ret_sin"], tm=tm, tn=qk_cols,
                            rope_cols=qk_cols, dk=dk, out_dtype=BF16)
    o = retention_core(proj, gn, B, S)
    return matmul_res(o, w_out.astype(BF16), h, tm=_row_tile(S, 512))


def nsa_layer(h, tabs, B, S, g_norm, w_in, q_norm, k_norm, cmp_pos, cmp_w1, cmp_w2, w_out):
    D, n_in = w_in.shape
    n_pad = -(-n_in // (7 * LANES)) * (7 * LANES)
    w_in_p = jnp.zeros((D, n_pad), BF16).at[:, :n_in].set(w_in.astype(BF16))
    H, G = NSA_HEADS, NSA_GROUPS
    n_blocks = n_pad // LANES
    block_scale = [None] * n_blocks
    gains = jnp.zeros((-(-n_blocks // 8) * 8, NSA_DH), F32)
    for c in range(H):
        block_scale[c] = NSA_DH ** -0.5 * LOG2_E
    gains = gains.at[0:H].set(jnp.broadcast_to(q_norm, (H, NSA_DH)))
    for branch, first in ((1, H + 2 * G), (2, H + 4 * G)):
        for c in range(first, first + G):
            block_scale[c] = 1.0
        gains = gains.at[first:first + G].set(jnp.broadcast_to(k_norm[branch], (G, NSA_DH)))
    proj = norm_matmul_headnorm(h, g_norm, w_in_p, tabs["nsa_cos"], tabs["nsa_sin"], gains, block_scale,
                                tm=_row_tile(S, 1024), tn=7 * LANES, out_dtype=BF16)
    kn = jnp.zeros((8, NSA_DH), F32).at[:3].set(k_norm)
    w1 = cmp_w1.astype(BF16).reshape(2, NSA_CMP_LEN, NSA_DH, cmp_w1.shape[-1])
    kcc, vcc = nsa_compress(proj, cmp_pos, w1, cmp_w2.astype(BF16), kn, tabs["nsa_cos"], tabs["nsa_sin"], B, S)
    o = nsa_attention(proj, kcc, vcc, B, S)
    return matmul_res(o, w_out.astype(BF16), h, tm=_row_tile(S, 512))


def s5_layer(h, B, S, g_norm, w_in, a_re, a_im, b_re, b_im, c_re, c_im, d_skip, log_dt, w_glu):
    D = h.shape[1]
    tm = _row_tile(S, 1024)
    u = norm_matmul(h, g_norm, w_in.astype(BF16), tm=tm, tn=D, out_dtype=BF16, time_major_batch=(B, S))
    y = s5_scan(u.reshape(S * B, D), a_re, a_im, b_re, b_im, c_re, c_im, d_skip, log_dt, B, S)
    return glu_res(y.reshape(S, B * D), w_glu.astype(BF16), h, B, S, tm=_row_tile(S, 512))


def kernel(x, p, positions, norm_mix, norm_ffn, norm_ple, ret_w_in, ret_gn, ret_w_out, nsa_w_in, nsa_q_norm, nsa_k_norm, nsa_cmp_pos, nsa_cmp_w1, nsa_cmp_w2, nsa_w_out, s5_w_in, s5_a_re, s5_a_im, s5_b_re, s5_b_im, s5_c_re, s5_c_im, s5_d, s5_log_dt, s5_w_glu, ffn_w_gate_up, ffn_w_down, moe_router, moe_w_gate_up, moe_w_down, ple_w_proj, ple_w_gate):
    B, S, D = x.shape
    depth = p.shape[0]
    T = B * S
    h = x.reshape(T, D)
    pos_col = positions.reshape(T, 1)
    p_all = p.reshape(depth, T, p.shape[-1])
    moe_gu, moe_dn = moe_w_gate_up, moe_w_down

    ones = jnp.ones((1, LANES), F32)
    dk = D // RET_HEADS
    inv_ret = (ROPE_THETA ** (-jnp.arange(0, dk, 2, dtype=F32) / dk)).reshape(1, LANES)
    inv_half = ROPE_THETA ** (-jnp.arange(0, NSA_DH, 2, dtype=F32) / NSA_DH)
    inv_nsa = jnp.concatenate([inv_half, inv_half]).reshape(1, LANES)
    sign_nsa = jnp.concatenate([-jnp.ones((NSA_DH // 2,), F32), jnp.ones((NSA_DH // 2,), F32)]).reshape(1, LANES)
    tabs = {}
    tabs["ret_cos"], tabs["ret_sin"] = rope_tables(pos_col, inv_ret, ones, _row_tile(T, 1024))
    if depth > 1:
        tabs["nsa_cos"], tabs["nsa_sin"] = rope_tables(pos_col, inv_nsa, sign_nsa, _row_tile(T, 1024))

    for i in range(depth):
        m, j = i % 3, i // 3
        if m == 0:
            h = retention_layer(h, tabs, B, S, norm_mix[i], ret_w_in[j], ret_gn[j], ret_w_out[j])
        elif m == 1:
            h = nsa_layer(h, tabs, B, S, norm_mix[i], nsa_w_in[j], nsa_q_norm[j], nsa_k_norm[j],
                          nsa_cmp_pos[j], nsa_cmp_w1[j], nsa_cmp_w2[j], nsa_w_out[j])
        else:
            h = s5_layer(h, B, S, norm_mix[i], s5_w_in[j], s5_a_re[j], s5_a_im[j], s5_b_re[j], s5_b_im[j],
                         s5_c_re[j], s5_c_im[j], s5_d[j], s5_log_dt[j], s5_w_glu[j])
        tm_ffn = _row_tile(T, 1024)
        ple = (norm_ple[i], p_all, i, ple_w_gate[i].astype(BF16), ple_w_proj[i].astype(BF16))
        if i % 2 == 0:
            h = swiglu_ffn_ple(h, norm_ffn[i], ffn_w_gate_up[i // 2].astype(BF16),
                               ffn_w_down[i // 2].astype(BF16), *ple, tm=tm_ffn, tf=256)
        else:
            h = moe_layer(h, norm_ffn[i], moe_router[i // 2], moe_gu, moe_dn, i // 2, ple,
                          tm_route=tm_ffn, tm_expert=tm_ffn, tm_combine=tm_ffn)
    return h.reshape(B, S, D)
```

```python
import functools
import math

import numpy as np
import jax
import jax.numpy as jnp
from jax import lax
from jax.experimental import pallas as pl
from jax.experimental.pallas import tpu as pltpu

F32 = jnp.float32
BF16 = jnp.bfloat16

NORM_EPS = 1e-6
ROPE_THETA = 10000.0

RET_HEADS = 4
RET_CHUNK = 128
RET_BLOCK = 512

NSA_HEADS = 8
NSA_GROUPS = 2
NSA_DH = 128
NSA_CMP_LEN = 32
NSA_CMP_STRIDE = 16
NSA_SEL_LEN = 64
NSA_TOP_N = 16
NSA_WINDOW = 512
NSA_Q_TILE = 256
NSA_KV_TILE = 512

S5_GROUP = 16
S5_STATE = 64
S5_SET = 8
S5_TIME_CHUNK = 64

N_EXPERTS = 8
MOE_NF = 7
LANES = 128
MASK_NEG = -1e30
LOG2_E = 1.4426950408889634

VMEM_LIMIT = 56 * 1024 * 1024


def _params(*sem):
    return pltpu.CompilerParams(dimension_semantics=sem, vmem_limit_bytes=VMEM_LIMIT)


def _dot(a, b):
    return jnp.dot(a, b, preferred_element_type=F32)


def _dot_nt(a, b):
    return lax.dot_general(a, b, (((1,), (1,)), ((), ())), preferred_element_type=F32)


def _rms(x, g):
    return x * lax.rsqrt(jnp.mean(x * x, axis=-1, keepdims=True) + NORM_EPS) * g


def _gelu_tanh(x):
    return 0.5 * x * (1.0 + jnp.tanh(math.sqrt(2.0 / math.pi) * (x + 0.044715 * (x * x * x))))


def _rope_table_kernel(pos_ref, inv_ref, sign_ref, cos_ref, sin_ref):
    ang = pos_ref[...].astype(F32) * inv_ref[...]
    cos_ref[...] = jnp.cos(ang)
    sin_ref[...] = jnp.sin(ang) * sign_ref[...]


def rope_tables(pos_col, inv, sign, tm):
    T = pos_col.shape[0]
    return pl.pallas_call(
        _rope_table_kernel,
        grid=(T // tm,),
        in_specs=[pl.BlockSpec((tm, 1), lambda i: (i, 0)),
                  pl.BlockSpec((1, LANES), lambda i: (0, 0)),
                  pl.BlockSpec((1, LANES), lambda i: (0, 0))],
        out_specs=[pl.BlockSpec((tm, LANES), lambda i: (i, 0))] * 2,
        out_shape=[jax.ShapeDtypeStruct((T, LANES), F32)] * 2,
        compiler_params=_params("parallel"),
        name="rope_tables",
    )(pos_col, inv, sign)


def _norm_matmul_kernel(x_ref, g_ref, w_ref, o_ref, xn_ref):
    @pl.when(pl.program_id(1) == 0)
    def _():
        xn_ref[...] = _rms(x_ref[...], g_ref[...]).astype(BF16)

    o_ref[...] = _dot(xn_ref[...], w_ref[...]).astype(o_ref.dtype)


def norm_matmul(x, g, w, *, tm, tn, out_dtype, time_major_batch=None):
    T, D = x.shape
    N = w.shape[1]
    nj = N // tn
    if time_major_batch is None:
        out_shape = jax.ShapeDtypeStruct((T, N), out_dtype)
        out_spec = pl.BlockSpec((tm, tn), lambda i, j: (i, j))
    else:
        B, S = time_major_batch
        ns = S // tm
        out_shape = jax.ShapeDtypeStruct((S, B * N), out_dtype)
        out_spec = pl.BlockSpec((tm, tn), lambda i, j: (i % ns, (i // ns) * nj + j))
    return pl.pallas_call(
        _norm_matmul_kernel,
        grid=(T // tm, nj),
        in_specs=[pl.BlockSpec((tm, D), lambda i, j: (i, 0)),
                  pl.BlockSpec((1, D), lambda i, j: (0, 0)),
                  pl.BlockSpec((D, tn), lambda i, j: (0, j))],
        out_specs=out_spec,
        out_shape=out_shape,
        scratch_shapes=[pltpu.VMEM((tm, D), BF16)],
        compiler_params=_params("parallel", "arbitrary"),
        name="norm_matmul",
    )(x, g.reshape(1, D), w)


def _norm_matmul_rope_kernel(x_ref, g_ref, w_ref, cos_ref, sin_ref, o_ref, xn_ref, *, n_rope_tiles, dk):
    j = pl.program_id(1)

    @pl.when(j == 0)
    def _():
        xn_ref[...] = _rms(x_ref[...], g_ref[...]).astype(BF16)

    @pl.when(j < n_rope_tiles)
    def _():
        cos = cos_ref[...]
        sin = sin_ref[...]
        half = dk // 2
        xn = xn_ref[...]
        for h in range(o_ref.shape[1] // dk):
            lo, mid, hi = h * dk, h * dk + half, (h + 1) * dk
            y = _dot(xn, w_ref[:, lo:hi])
            x1, x2 = y[:, 0:half], y[:, half:dk]
            o_ref[:, lo:mid] = (x1 * cos - x2 * sin).astype(o_ref.dtype)
            o_ref[:, mid:hi] = (x2 * cos + x1 * sin).astype(o_ref.dtype)

    @pl.when(j >= n_rope_tiles)
    def _():
        o_ref[...] = _dot(xn_ref[...], w_ref[...]).astype(o_ref.dtype)


def norm_matmul_rope(x, g, w, cos, sin, *, tm, tn, rope_cols, dk, out_dtype):
    T, D = x.shape
    N = w.shape[1]
    assert rope_cols % tn == 0 and tn % dk == 0
    return pl.pallas_call(
        functools.partial(_norm_matmul_rope_kernel, n_rope_tiles=rope_cols // tn, dk=dk),
        grid=(T // tm, N // tn),
        in_specs=[pl.BlockSpec((tm, D), lambda i, j: (i, 0)),
                  pl.BlockSpec((1, D), lambda i, j: (0, 0)),
                  pl.BlockSpec((D, tn), lambda i, j: (0, j)),
                  pl.BlockSpec((tm, dk // 2), lambda i, j: (i, 0)),
                  pl.BlockSpec((tm, dk // 2), lambda i, j: (i, 0))],
        out_specs=pl.BlockSpec((tm, tn), lambda i, j: (i, j)),
        out_shape=jax.ShapeDtypeStruct((T, N), out_dtype),
        scratch_shapes=[pltpu.VMEM((tm, D), BF16)],
        compiler_params=_params("parallel", "arbitrary"),
        name="norm_matmul_rope",
    )(x, g.reshape(1, D), w, cos, sin)


def _norm_matmul_headnorm_kernel(x_ref, g_ref, w_ref, cos_ref, sin_ref, gain_ref, o_ref, xn_ref, *, block_scale):
    j = pl.program_id(1)
    per_tile = o_ref.shape[1] // LANES

    @pl.when(j == 0)
    def _():
        xn_ref[...] = _rms(x_ref[...], g_ref[...]).astype(BF16)

    y = _dot(xn_ref[...], w_ref[...])
    for jt in range(len(block_scale) // per_tile):
        @pl.when(j == jt)
        def _(jt=jt):
            for b in range(per_tile):
                c = jt * per_tile + b
                yb = y[:, b * LANES:(b + 1) * LANES]
                if block_scale[c] is not None:
                    yb = _rms(yb, gain_ref[c:c + 1, :])
                    yb = (yb * cos_ref[...] + pltpu.roll(yb, LANES // 2, axis=1) * sin_ref[...]) * block_scale[c]
                o_ref[:, b * LANES:(b + 1) * LANES] = yb.astype(o_ref.dtype)


def norm_matmul_headnorm(x, g, w, cos, sin, gains, block_scale, *, tm, tn, out_dtype):
    T, D = x.shape
    N = w.shape[1]
    assert N // LANES == len(block_scale) and tn % LANES == 0
    return pl.pallas_call(
        functools.partial(_norm_matmul_headnorm_kernel, block_scale=tuple(block_scale)),
        grid=(T // tm, N // tn),
        in_specs=[pl.BlockSpec((tm, D), lambda i, j: (i, 0)),
                  pl.BlockSpec((1, D), lambda i, j: (0, 0)),
                  pl.BlockSpec((D, tn), lambda i, j: (0, j)),
                  pl.BlockSpec((tm, LANES), lambda i, j: (i, 0)),
                  pl.BlockSpec((tm, LANES), lambda i, j: (i, 0)),
                  pl.BlockSpec(gains.shape, lambda i, j: (0, 0))],
        out_specs=pl.BlockSpec((tm, tn), lambda i, j: (i, j)),
        out_shape=jax.ShapeDtypeStruct((T, N), out_dtype),
        scratch_shapes=[pltpu.VMEM((tm, D), BF16)],
        compiler_params=_params("parallel", "arbitrary"),
        name="norm_matmul_headnorm",
    )(x, g.reshape(1, D), w, cos, sin, gains)


def _matmul_res_kernel(x_ref, w_ref, r_ref, o_ref):
    o_ref[...] = r_ref[...] + _dot(x_ref[...], w_ref[...])


def matmul_res(x, w, res, *, tm):
    T, K = x.shape
    N = w.shape[1]
    return pl.pallas_call(
        _matmul_res_kernel,
        grid=(T // tm,),
        in_specs=[pl.BlockSpec((tm, K), lambda i: (i, 0)),
                  pl.BlockSpec((K, N), lambda i: (0, 0)),
                  pl.BlockSpec((tm, N), lambda i: (i, 0))],
        out_specs=pl.BlockSpec((tm, N), lambda i: (i, 0)),
        out_shape=jax.ShapeDtypeStruct((T, N), F32),
        compiler_params=_params("parallel"),
        name="matmul_res",
    )(x, w, res)


def _retention_kernel(q_ref, k_ref, v_ref, g_ref, dmask_ref, qdec_ref, kdec_ref, cdec_ref, gn_ref, wout_ref,
                      res_ref, out_ref, r_scr, o_ref):
    H, dk, dv = r_scr.shape
    C = dmask_ref.shape[1]

    @pl.when(pl.program_id(1) == 0)
    def _():
        r_scr[...] = jnp.zeros_like(r_scr)

    for c in range(q_ref.shape[0] // C):
        rows = slice(c * C, (c + 1) * C)
        for h in range(H):
            qb = q_ref[rows, h * dk:(h + 1) * dk]
            kb = k_ref[rows, h * dk:(h + 1) * dk]
            v = v_ref[rows, h * dv:(h + 1) * dv]
            scores = _dot_nt(qb, kb) * dmask_ref[h]
            intra = _dot(scores.astype(BF16), v)
            r_old = r_scr[h]
            cross = _dot(qb, r_old.astype(BF16)) * qdec_ref[h]
            kd_t = (kb.astype(F32) * kdec_ref[h]).T.astype(BF16)
            r_scr[h] = r_old * cdec_ref[h] + _dot(kd_t, v)

            o = intra + cross
            mu = jnp.mean(o, axis=-1, keepdims=True)
            oc = o - mu
            var = jnp.mean(oc * oc, axis=-1, keepdims=True)
            on = oc * lax.rsqrt(var + NORM_EPS) * gn_ref[:, h * dv:(h + 1) * dv]
            g = g_ref[rows, h * dv:(h + 1) * dv].astype(F32)
            o_ref[rows, h * dv:(h + 1) * dv] = (on * (g * jax.nn.sigmoid(g))).astype(o_ref.dtype)
        out_ref[rows, :] = res_ref[rows, :] + _dot(o_ref[rows, :], wout_ref[...])


def retention_core(proj, gn, w_out, res, B, S):
    H, C = RET_HEADS, RET_CHUNK
    T = proj.shape[0]
    dk = proj.shape[1] // (6 * H)
    dv = 2 * dk
    D = w_out.shape[1]
    tb = min(RET_BLOCK, S)
    nb = S // tb
    scale = dk ** -0.5
    log_gamma = jnp.log1p(-jnp.exp2(-5.0 - jnp.arange(H, dtype=F32)))
    idx = jnp.arange(C, dtype=F32)
    diff = idx[:, None] - idx[None, :]
    dmask = jnp.where(diff >= 0, jnp.exp(jnp.maximum(diff, 0.0) * log_gamma[:, None, None]), 0.0) * scale
    qdec = jnp.exp((idx + 1.0) * log_gamma[:, None])[..., None]
    kdec = jnp.exp((C - 1.0 - idx) * log_gamma[:, None])[..., None] * scale
    cdec = jnp.exp(C * log_gamma).reshape(H, 1, 1)
    whole = lambda shape: pl.BlockSpec(shape, lambda b, n: (0,) * len(shape))
    return pl.pallas_call(
        _retention_kernel,
        grid=(B, nb),
        in_specs=[pl.BlockSpec((tb, H * dk), lambda b, n: (b * nb + n, 0)),
                  pl.BlockSpec((tb, H * dk), lambda b, n: (b * nb + n, 1)),
                  pl.BlockSpec((tb, H * dv), lambda b, n: (b * nb + n, 1)),
                  pl.BlockSpec((tb, H * dv), lambda b, n: (b * nb + n, 2)),
                  whole((H, C, C)), whole((H, C, 1)), whole((H, C, 1)), whole((H, 1, 1)),
                  whole((1, H * dv)), whole(w_out.shape),
                  pl.BlockSpec((tb, D), lambda b, n: (b * nb + n, 0))],
        out_specs=pl.BlockSpec((tb, D), lambda b, n: (b * nb + n, 0)),
        out_shape=jax.ShapeDtypeStruct((T, D), F32),
        scratch_shapes=[pltpu.VMEM((H, dk, dv), F32), pltpu.VMEM((tb, H * dv), BF16)],
        compiler_params=_params("parallel", "arbitrary"),
        name="retention",
    )(proj, proj, proj, proj, dmask, qdec, kdec, cdec, gn.reshape(1, H * dv), w_out, res)


def _nsa_compress_kernel(kc_ref, vc_ref, pe_ref, w1_ref, w2_ref, kn_ref, cos_ref, sin_ref,
                         kco_ref, vco_ref, xs_scr):
    nh = NSA_CMP_LEN // NSA_CMP_STRIDE
    n_rows = xs_scr.shape[0] // NSA_CMP_STRIDE
    for br, (src, dst) in enumerate(((kc_ref, kco_ref), (vc_ref, vco_ref))):
        xs_scr[...] = src[...].astype(F32)
        acc = [jnp.zeros((n_rows, w1_ref.shape[-1]), F32) for _ in range(nh)]
        for l in range(NSA_CMP_STRIDE):
            piece = xs_scr[pl.ds(l, n_rows, stride=NSA_CMP_STRIDE), :]
            for a in range(nh):
                ll = a * NSA_CMP_STRIDE + l
                acc[a] = acc[a] + _dot((piece + pe_ref[br, ll:ll + 1, :]).astype(BF16), w1_ref[br, ll])
        hid = acc[0] + pltpu.roll(acc[1], n_rows - 1, axis=0)
        z = _dot(_gelu_tanh(hid).astype(BF16), w2_ref[br])
        if br == 0:
            z = _rms(z, kn_ref[0:1, :])
            cos = cos_ref[pl.ds(0, n_rows, stride=NSA_CMP_STRIDE), :]
            sin = sin_ref[pl.ds(0, n_rows, stride=NSA_CMP_STRIDE), :]
            z = z * cos + pltpu.roll(z, NSA_DH // 2, axis=1) * sin
        dst[0, 0] = z.astype(dst.dtype)


def nsa_compress(proj, pe, w1, w2, kn, cosn, sinn, B, S):
    G, dh = NSA_GROUPS, NSA_DH
    nr = S // NSA_CMP_STRIDE
    kc0 = NSA_HEADS
    vc0 = NSA_HEADS + G
    return pl.pallas_call(
        _nsa_compress_kernel,
        grid=(B, G),
        in_specs=[pl.BlockSpec((S, dh), lambda b, g: (b, kc0 + g)),
                  pl.BlockSpec((S, dh), lambda b, g: (b, vc0 + g)),
                  pl.BlockSpec(pe.shape, lambda b, g: (0, 0, 0)),
                  pl.BlockSpec(w1.shape, lambda b, g: (0, 0, 0, 0)),
                  pl.BlockSpec(w2.shape, lambda b, g: (0, 0, 0)),
                  pl.BlockSpec(kn.shape, lambda b, g: (0, 0)),
                  pl.BlockSpec((S, dh), lambda b, g: (b, 0)),
                  pl.BlockSpec((S, dh), lambda b, g: (b, 0))],
        out_specs=[pl.BlockSpec((1, 1, nr, dh), lambda b, g: (b, g, 0, 0))] * 2,
        out_shape=[jax.ShapeDtypeStruct((B, G, nr, dh), BF16)] * 2,
        scratch_shapes=[pltpu.VMEM((S, dh), F32)],
        compiler_params=_params("parallel", "parallel"),
        name="nsa_compress",
    )(proj, proj, pe, w1, w2, kn, cosn, sinn)


def _softmax2_parts(s, bias, hg):
    tq, n = bias.shape
    s3 = s.reshape(hg, tq, n) + bias[None]
    p = jnp.exp2(s3 - jnp.max(s3, axis=-1, keepdims=True))
    return p.reshape(hg * tq, n), jnp.sum(p, axis=-1, keepdims=True).reshape(hg * tq, 1)


def _nsa_attn_kernel(q_ref, gate_ref, kc_ref, vc_ref, ks_ref, vs_ref, kw_ref, vw_ref, mmat_ref, o_ref):
    grp = pl.program_id(1)
    qi = pl.program_id(2)
    tq = q_ref.shape[0]
    dh = NSA_DH
    hg = q_ref.shape[1] // dh
    S = ks_ref.shape[0]
    tk = min(NSA_KV_TILE, S)
    q0 = qi * tq

    qall = jnp.concatenate([q_ref[:, h * dh:(h + 1) * dh] for h in range(hg)], axis=0)

    def tok(n):
        return q0 + lax.broadcasted_iota(jnp.int32, (tq, n), 0)

    def key(n):
        return lax.broadcasted_iota(jnp.int32, (tq, n), 1)

    ncp = kc_ref.shape[2]
    valid_c = key(ncp) * NSA_CMP_STRIDE + (NSA_CMP_LEN - 1) <= tok(ncp)
    p_c, l_c = _softmax2_parts(_dot_nt(qall, kc_ref[0, 0]), jnp.where(valid_c, 0.0, MASK_NEG), hg)
    t_row = q0 + jnp.bitwise_and(lax.broadcasted_iota(jnp.int32, (hg * tq, 1), 0), tq - 1)
    p_c = jnp.where(t_row >= NSA_CMP_LEN - 1, p_c / l_c, 0.0)
    o_c = _dot(p_c.astype(BF16), vc_ref[0, 0])
    p_sum = p_c[0:tq]
    for h in range(1, hg):
        p_sum = p_sum + p_c[h * tq:(h + 1) * tq]
    imp = jnp.dot(p_sum, mmat_ref[...], preferred_element_type=F32, precision=lax.Precision.HIGHEST)

    n_sel = S // NSA_SEL_LEN
    sel_rows = ((n_sel + 7) // 8) * 8
    v_imp = imp.T[0:sel_rows, :]
    jb = lax.broadcasted_iota(jnp.int32, (sel_rows, tq), 0)
    cur = (q0 + lax.broadcasted_iota(jnp.int32, (sel_rows, tq), 1)) // NSA_SEL_LEN
    forced = (jb == 0) | (jb == cur) | (jb == cur - 1)
    v_imp = jnp.where(forced, jnp.inf, jnp.where(jb > cur, -jnp.inf, v_imp))
    if sel_rows > n_sel:
        v_imp = jnp.where(jb >= n_sel, -jnp.inf, v_imp)
    rank = jnp.zeros((sel_rows, tq), jnp.int32)
    for i in range(n_sel):
        r = v_imp[i:i + 1, :]
        beats = (r > v_imp) | ((r == v_imp) & (jb > i))
        rank = rank + beats.astype(jnp.int32)
    sel_t = ((rank < min(NSA_TOP_N, n_sel)) & (jb < n_sel)).astype(F32)
    if sel_rows < LANES:
        sel_t = jnp.concatenate([sel_t, jnp.zeros((LANES - sel_rows, tq), F32)], axis=0)
    sel = sel_t.T.astype(BF16)

    row_t = tok(tk)
    lane_k = key(tk)
    e_row = lax.broadcasted_iota(jnp.int32, (LANES, tk), 0)
    e_blk = lax.broadcasted_iota(jnp.int32, (LANES, tk), 1) // NSA_SEL_LEN

    def sel_step(j, carry):
        m, l, acc = carry
        base = pl.multiple_of(j * tk, tk)
        s = _dot_nt(qall, ks_ref[pl.ds(base, tk), :])
        expand = (e_blk + j * (tk // NSA_SEL_LEN) == e_row).astype(BF16)
        chosen = _dot(sel, expand)
        bias = jnp.where((chosen > 0.5) & (lane_k + base <= row_t), 0.0, MASK_NEG)
        s3 = s.reshape(hg, tq, tk) + bias[None]
        m_new = jnp.maximum(m, jnp.max(s3, axis=-1, keepdims=True))
        alpha = jnp.exp2(m - m_new)
        p = jnp.exp2(s3 - m_new)
        l = alpha * l + jnp.sum(p, axis=-1, keepdims=True)
        pv = _dot(p.reshape(hg * tq, tk).astype(BF16), vs_ref[pl.ds(base, tk), :])
        acc = alpha.reshape(hg * tq, 1) * acc + pv
        return m_new, l, acc

    n_kv = (q0 + tq + tk - 1) // tk
    init = (jnp.full((hg, tq, 1), MASK_NEG, F32), jnp.zeros((hg, tq, 1), F32), jnp.zeros((hg * tq, dh), F32))
    _, l_s, acc_s = lax.fori_loop(0, n_kv, sel_step, init)
    o_s = acc_s / l_s.reshape(hg * tq, 1)

    span = min(tq + NSA_WINDOW, S)
    start = pl.multiple_of(jnp.maximum(q0 + tq - span, 0), tq)
    dist = tok(span) - (start + key(span))
    bias_w = jnp.where((dist >= 0) & (dist < NSA_WINDOW), 0.0, MASK_NEG)
    p_w, l_w = _softmax2_parts(_dot_nt(qall, kw_ref[pl.ds(start, span), :]), bias_w, hg)
    o_w = _dot(p_w.astype(BF16), vw_ref[pl.ds(start, span), :]) / l_w

    gsig = jax.nn.sigmoid(gate_ref[...].astype(F32))
    n_br = 3
    for h in range(hg):
        def gate_col(br):
            lo = h * n_br + br
            hi = lo + hg * n_br
            return jnp.where(grp == 0, gsig[:, lo:lo + 1], gsig[:, hi:hi + 1])
        rows = slice(h * tq, (h + 1) * tq)
        o = gate_col(0) * o_c[rows] + gate_col(1) * o_s[rows] + gate_col(2) * o_w[rows]
        o_ref[:, h * dh:(h + 1) * dh] = o.astype(o_ref.dtype)


def nsa_attention(proj, kcc, vcc, B, S):
    H, G, dh = NSA_HEADS, NSA_GROUPS, NSA_DH
    hg = H // G
    T = proj.shape[0]
    tq = min(NSA_Q_TILE, S)
    nq = S // tq
    nr = kcc.shape[2]
    n_c = (S - NSA_CMP_LEN) // NSA_CMP_STRIDE + 1
    n_sel = S // NSA_SEL_LEN
    cs = (np.arange(n_c) * NSA_CMP_STRIDE)[:, None]
    js = (np.arange(n_sel) * NSA_SEL_LEN)[None, :]
    overlap = np.clip(np.minimum(cs + NSA_CMP_LEN, js + NSA_SEL_LEN) - np.maximum(cs, js), 0, None) / NSA_CMP_LEN
    mmat = np.zeros((nr, LANES), np.float32)
    mmat[:n_c, :n_sel] = overlap
    col = lambda base: (lambda b, g, i: (b, base + g))
    return pl.pallas_call(
        _nsa_attn_kernel,
        grid=(B, G, nq),
        in_specs=[pl.BlockSpec((tq, hg * dh), lambda b, g, i: (b * nq + i, g)),
                  pl.BlockSpec((tq, LANES), lambda b, g, i: (b * nq + i, H + 6 * G)),
                  pl.BlockSpec((1, 1, nr, dh), lambda b, g, i: (b, g, 0, 0)),
                  pl.BlockSpec((1, 1, nr, dh), lambda b, g, i: (b, g, 0, 0)),
                  pl.BlockSpec((S, dh), col(H + 2 * G)),
                  pl.BlockSpec((S, dh), col(H + 3 * G)),
                  pl.BlockSpec((S, dh), col(H + 4 * G)),
                  pl.BlockSpec((S, dh), col(H + 5 * G)),
                  pl.BlockSpec(mmat.shape, lambda b, g, i: (0, 0))],
        out_specs=pl.BlockSpec((tq, hg * dh), lambda b, g, i: (b * nq + i, g)),
        out_shape=jax.ShapeDtypeStruct((T, H * dh), BF16),
        compiler_params=_params("parallel", "parallel", "arbitrary"),
        name="nsa_attention",
    )(proj, proj, kcc, vcc, proj, proj, proj, proj, jnp.asarray(mmat))


def _s5_scan_kernel(u_ref, bbd_ref, are_ref, aim_ref, cbd_ref, d_ref, y_ref, bu_scr, xs_scr, st_scr):
    nb = st_scr.shape[0] // 2
    w = are_ref.shape[-1]
    tc = u_ref.shape[0] // nb

    @pl.when(pl.program_id(1) == 0)
    def _():
        st_scr[...] = jnp.zeros_like(st_scr)

    u = u_ref[...]
    bu_scr[...] = _dot(u, bbd_ref[0])
    a_re = jnp.broadcast_to(are_ref[0], (nb, w))
    a_im = jnp.broadcast_to(aim_ref[0], (nb, w))

    def step(t, carry):
        xr, xi = carry
        r0 = pl.multiple_of(t * nb, nb)
        nxr = a_re * xr - a_im * xi + bu_scr[pl.ds(r0, nb), 0:w]
        nxi = a_re * xi + a_im * xr + bu_scr[pl.ds(r0, nb), w:2 * w]
        xs_scr[pl.ds(r0, nb), 0:w] = nxr.astype(BF16)
        xs_scr[pl.ds(r0, nb), w:2 * w] = nxi.astype(BF16)
        return nxr, nxi

    xr, xi = lax.fori_loop(0, tc, step, (st_scr[0:nb, :], st_scr[nb:2 * nb, :]), unroll=4)
    st_scr[0:nb, :] = xr
    st_scr[nb:2 * nb, :] = xi
    y = _dot(xs_scr[...], cbd_ref[0]) + d_ref[...] * u.astype(F32)
    y_ref[...] = _gelu_tanh(y).astype(y_ref.dtype)


def s5_scan(u_tm, a_re, a_im, b_re, b_im, c_re, c_im, d_skip, log_dt, B, S):
    D = u_tm.shape[1]
    Cg, P, NS = S5_GROUP, S5_STATE, S5_SET
    G = D // Cg
    K = G // NS
    w = NS * P
    tc = min(S5_TIME_CHUNK, S)
    dt = jnp.exp(log_dt.astype(F32))[:, None]
    mag = jnp.exp(dt * a_re)
    abar_re, abar_im = mag * jnp.cos(dt * a_im), mag * jnp.sin(dt * a_im)
    den = a_re * a_re + a_im * a_im
    nr_, ni_ = abar_re - 1.0, abar_im
    f_re = (nr_ * a_re + ni_ * a_im) / den
    f_im = (ni_ * a_re - nr_ * a_im) / den
    bbar_re = f_re[..., None] * b_re - f_im[..., None] * b_im
    bbar_im = f_re[..., None] * b_im + f_im[..., None] * b_re
    eye = jnp.eye(NS, dtype=F32)

    def in_blockdiag(bb):
        t = bb.reshape(K, NS, P, Cg).transpose(0, 1, 3, 2)
        return jnp.einsum('kgcp,gh->kgchp', t, eye).reshape(K, NS * Cg, NS * P)

    def out_blockdiag(cc):
        t = cc.reshape(K, NS, Cg, P).transpose(0, 1, 3, 2)
        return jnp.einsum('kgpc,gh->kgphc', t, eye).reshape(K, NS * P, NS * Cg)

    bbd = jnp.concatenate([in_blockdiag(bbar_re), in_blockdiag(bbar_im)], axis=-1).astype(BF16)
    cbd = jnp.concatenate([out_blockdiag(c_re), -out_blockdiag(c_im)], axis=1).astype(BF16)
    are = abar_re.reshape(K, 1, w)
    aim = abar_im.reshape(K, 1, w)
    rows = tc * B
    return pl.pallas_call(
        _s5_scan_kernel,
        grid=(K, S // tc),
        in_specs=[pl.BlockSpec((rows, LANES), lambda k, c: (c, k)),
                  pl.BlockSpec((1, LANES, 2 * w), lambda k, c: (k, 0, 0)),
                  pl.BlockSpec((1, 1, w), lambda k, c: (k, 0, 0)),
                  pl.BlockSpec((1, 1, w), lambda k, c: (k, 0, 0)),
                  pl.BlockSpec((1, 2 * w, LANES), lambda k, c: (k, 0, 0)),
                  pl.BlockSpec((1, LANES), lambda k, c: (0, k))],
        out_specs=pl.BlockSpec((rows, LANES), lambda k, c: (c, k)),
        out_shape=jax.ShapeDtypeStruct((S * B, D), BF16),
        scratch_shapes=[pltpu.VMEM((rows, 2 * w), F32), pltpu.VMEM((rows, 2 * w), BF16),
                        pltpu.VMEM((2 * B, w), F32)],
        compiler_params=_params("parallel", "arbitrary"),
        name="s5_scan",
    )(u_tm, bbd, are, aim, cbd, d_skip.reshape(1, D))


def _glu_res_kernel(y_ref, wa_ref, wb_ref, r_ref, o_ref):
    y = y_ref[...]
    a = _dot(y, wa_ref[...])
    b = _dot(y, wb_ref[...])
    o_ref[...] = r_ref[...] + a * jax.nn.sigmoid(b)


def glu_res(y_tm, w, res, B, S, *, tm):
    D = w.shape[0]
    N = w.shape[1] // 2
    ns = S // tm
    return pl.pallas_call(
        _glu_res_kernel,
        grid=(B * ns,),
        in_specs=[pl.BlockSpec((tm, D), lambda i: (i % ns, i // ns)),
                  pl.BlockSpec((D, N), lambda i: (0, 0)),
                  pl.BlockSpec((D, N), lambda i: (0, 1)),
                  pl.BlockSpec((tm, N), lambda i: (i, 0))],
        out_specs=pl.BlockSpec((tm, N), lambda i: (i, 0)),
        out_shape=jax.ShapeDtypeStruct((B * S, N), F32),
        compiler_params=_params("parallel"),
        name="glu_res",
    )(y_tm, w, w, res)


def _ple(h, g_ref, p_ref, wgate_ref, wproj_ref):
    gate = jax.nn.sigmoid(_dot(_rms(h, g_ref[...]).astype(BF16), wgate_ref[...]))
    return h + gate * _dot(p_ref[...].astype(BF16), wproj_ref[...])


def _ffn_kernel(x_ref, g_ref, wg_ref, wu_ref, wd_ref, gp_ref, p_ref, wgate_ref, wproj_ref, o_ref,
                xn_scr, acc_scr):
    f = pl.program_id(1)

    @pl.when(f == 0)
    def _():
        xn_scr[...] = _rms(x_ref[...], g_ref[...]).astype(BF16)
        acc_scr[...] = jnp.zeros_like(acc_scr)

    xn = xn_scr[...]
    gg = _dot(xn, wg_ref[...])
    uu = _dot(xn, wu_ref[...])
    acc_scr[...] += _dot((gg * jax.nn.sigmoid(gg) * uu).astype(BF16), wd_ref[...])

    @pl.when(f == pl.num_programs(1) - 1)
    def _():
        o_ref[...] = _ple(x_ref[...] + acc_scr[...], gp_ref, p_ref, wgate_ref, wproj_ref)


def swiglu_ffn_ple(x, g, w_gate_up, w_down, g_ple, p_all, layer, w_pgate, w_pproj, *, tm, tf):
    T, D = x.shape
    F = w_down.shape[0]
    PD = p_all.shape[2]
    nf = F // tf
    return pl.pallas_call(
        _ffn_kernel,
        grid=(T // tm, nf),
        in_specs=[pl.BlockSpec((tm, D), lambda i, f: (i, 0)),
                  pl.BlockSpec((1, D), lambda i, f: (0, 0)),
                  pl.BlockSpec((D, tf), lambda i, f: (0, f)),
                  pl.BlockSpec((D, tf), lambda i, f: (0, nf + f)),
                  pl.BlockSpec((tf, D), lambda i, f: (f, 0)),
                  pl.BlockSpec((1, D), lambda i, f: (0, 0)),
                  pl.BlockSpec((None, tm, PD), lambda i, f: (layer, i, 0)),
                  pl.BlockSpec((D, D), lambda i, f: (0, 0)),
                  pl.BlockSpec((PD, D), lambda i, f: (0, 0))],
        out_specs=pl.BlockSpec((tm, D), lambda i, f: (i, 0)),
        out_shape=jax.ShapeDtypeStruct((T, D), F32),
        scratch_shapes=[pltpu.VMEM((tm, D), BF16), pltpu.VMEM((tm, D), F32)],
        compiler_params=_params("parallel", "arbitrary"),
        name="swiglu_ffn_ple",
    )(x, g.reshape(1, D), w_gate_up, w_gate_up, w_down, g_ple.reshape(1, D), p_all, w_pgate, w_pproj)


META_E0, META_E1, META_W0, META_W1, META_R0, META_R1 = range(6)


def _router_kernel(x_ref, g_ref, w_ref, meta_ref, cnt_ref, carry_scr, tri_scr):
    @pl.when(pl.program_id(0) == 0)
    def _():
        carry_scr[...] = jnp.zeros_like(carry_scr)
        n = tri_scr.shape[0]
        tri = lax.broadcasted_iota(jnp.int32, (n, n), 0) > lax.broadcasted_iota(jnp.int32, (n, n), 1)
        tri_scr[...] = tri.astype(BF16)

    xn = _rms(x_ref[...], g_ref[...])
    x_hi = xn.astype(BF16)
    x_lo = (xn - x_hi.astype(F32)).astype(BF16)
    w = w_ref[...]
    w_hi = w.astype(BF16)
    w_lo = (w - w_hi.astype(F32)).astype(BF16)
    logits = _dot(x_hi, w_hi) + (_dot(x_hi, w_lo) + _dot(x_lo, w_hi))
    lane = lax.broadcasted_iota(jnp.int32, logits.shape, 1).astype(F32)
    logits = jnp.where(lane < N_EXPERTS, logits, -jnp.inf)

    def top1(v):
        m = jnp.max(v, axis=-1, keepdims=True)
        idx = jnp.min(jnp.where(v == m, lane, float(LANES)), axis=-1, keepdims=True)
        return m, idx, lane == idx

    m1, e0, hot1 = top1(logits)
    m2, e1, hot2 = top1(jnp.where(hot1, -jnp.inf, logits))
    ex = jnp.exp(m2 - m1)
    w0 = 1.0 / (1.0 + ex)
    w1 = ex * w0

    both = jnp.where(hot1 | hot2, 1.0, 0.0)
    before = _dot(tri_scr[...], both.astype(BF16)) + carry_scr[0:1, :]
    r0 = jnp.sum(jnp.where(hot1, before, 0.0), axis=-1, keepdims=True)
    r1 = jnp.sum(jnp.where(hot2, before, 0.0), axis=-1, keepdims=True)
    carry_scr[...] = carry_scr[...] + jnp.sum(both, axis=0, keepdims=True)
    cnt_ref[...] = carry_scr[...]

    meta = jnp.zeros_like(logits)
    for k, v in ((META_E0, e0), (META_E1, e1), (META_W0, w0), (META_W1, w1), (META_R0, r0), (META_R1, r1)):
        meta = jnp.where(lane == float(k), v, meta)
    meta_ref[...] = meta


def router_topk(x, g, w_router, *, tm):
    T, D = x.shape
    w = jnp.zeros((D, LANES), F32).at[:, :N_EXPERTS].set(w_router)
    return pl.pallas_call(
        _router_kernel,
        grid=(T // tm,),
        in_specs=[pl.BlockSpec((tm, D), lambda i: (i, 0)),
                  pl.BlockSpec((1, D), lambda i: (0, 0)),
                  pl.BlockSpec((D, LANES), lambda i: (0, 0))],
        out_specs=[pl.BlockSpec((tm, LANES), lambda i: (i, 0)),
                   pl.BlockSpec((8, LANES), lambda i: (0, 0))],
        out_shape=[jax.ShapeDtypeStruct((T, LANES), F32), jax.ShapeDtypeStruct((8, LANES), F32)],
        scratch_shapes=[pltpu.VMEM((8, LANES), F32), pltpu.VMEM((tm, tm), BF16)],
        compiler_params=_params("arbitrary"),
        name="router",
    )(x, g.reshape(1, D), w)


def _start_row_gathers(idx_ref, src_hbm, dst_ref, sem, lo, hi, idx_off=0):
    for r in range(lo, hi):
        pltpu.make_async_copy(src_hbm.at[pl.ds(idx_ref[0, 0, idx_off + r], 1), :], dst_ref.at[pl.ds(r, 1), :],
                              sem).start()


def _wait_row_gathers(src_hbm, dst_ref, sem):
    pltpu.make_async_copy(src_hbm.at[pl.ds(0, dst_ref.shape[0]), :], dst_ref, sem).wait()


def _moe_expert_kernel(te_ref, nu_ref, tok_ref, tok_next_ref, x_hbm, g_ref, wg_ref, wu_ref, wd_ref, y_ref,
                       xg_scr, xn_scr, acc_scr, sems):
    i = pl.program_id(0)
    f = pl.program_id(1)
    n_tiles = pl.num_programs(0)
    nf = pl.num_programs(1)
    n_used = nu_ref[0]
    active = i < n_used
    tm = xn_scr.shape[0]
    cur = xg_scr.at[i % 2]
    nxt = xg_scr.at[(i + 1) % 2]
    sem_cur = sems.at[i % 2]
    sem_nxt = sems.at[(i + 1) % 2]

    @pl.when((i == 0) & (f == 0))
    def _():
        def issue(r, carry):
            pltpu.make_async_copy(x_hbm.at[pl.ds(tok_ref[0, 0, r], 1), :], cur.at[pl.ds(r, 1), :], sem_cur).start()
            return carry
        lax.fori_loop(0, tm, issue, 0, unroll=8)

    @pl.when((f == 0) & (i <= n_used))
    def _():
        _wait_row_gathers(x_hbm, cur, sem_cur)

    @pl.when(active & (f == 0))
    def _():
        xn_scr[...] = _rms(cur[...], g_ref[...]).astype(BF16)
        acc_scr[...] = jnp.zeros_like(acc_scr)

    def step(lo, hi):
        xn = xn_scr[...]
        gg = _dot(xn, wg_ref[0].astype(BF16))
        uu = _dot(xn, wu_ref[0].astype(BF16))
        acc_scr[...] += _dot((gg * jax.nn.sigmoid(gg) * uu).astype(BF16), wd_ref[0].astype(BF16))
        _start_row_gathers(tok_next_ref, x_hbm, nxt, sem_nxt, lo, hi)

    per = -(-tm // MOE_NF)
    for fs in range(MOE_NF):
        @pl.when(active & (f == fs))
        def _(fs=fs):
            step(min(fs * per, tm), min((fs + 1) * per, tm))

    @pl.when(f == nf - 1)
    def _():
        y_ref[...] = jnp.where(active, acc_scr[...], 0.0)

    @pl.when(active & (i == n_tiles - 1) & (f == nf - 1))
    def _():
        _wait_row_gathers(x_hbm, nxt, sem_nxt)


def moe_expert_ffn(x, g, tok_of_slot, tile_expert, n_used, w_gate_up, w_down, layer, *, tm):
    T, D = x.shape
    _, E, F, _ = w_down.shape
    nf = MOE_NF
    tf = F // nf
    n_tiles = tok_of_slot.shape[0]

    def fblk(i, f, nu):
        return jnp.where(i < nu[0], f, nf - 1)

    tok_spec = lambda off: pl.BlockSpec((1, 1, tm), lambda i, f, te, nu: (jnp.minimum(i + off, n_tiles - 1), 0, 0),
                                        memory_space=pltpu.SMEM)
    grid_spec = pltpu.PrefetchScalarGridSpec(
        num_scalar_prefetch=2,
        grid=(n_tiles, nf),
        in_specs=[tok_spec(0), tok_spec(1),
                  pl.BlockSpec(memory_space=pl.ANY),
                  pl.BlockSpec((1, D), lambda i, f, te, nu: (0, 0)),
                  pl.BlockSpec((None, 1, D, tf), lambda i, f, te, nu: (layer, te[i], 0, fblk(i, f, nu))),
                  pl.BlockSpec((None, 1, D, tf), lambda i, f, te, nu: (layer, te[i], 0, nf + fblk(i, f, nu))),
                  pl.BlockSpec((None, 1, tf, D), lambda i, f, te, nu: (layer, te[i], fblk(i, f, nu), 0))],
        out_specs=pl.BlockSpec((tm, D), lambda i, f, te, nu: (i, 0)),
        scratch_shapes=[pltpu.VMEM((2, tm, D), F32), pltpu.VMEM((tm, D), BF16), pltpu.VMEM((tm, D), F32),
                        pltpu.SemaphoreType.DMA((2,))],
    )
    return pl.pallas_call(
        _moe_expert_kernel,
        grid_spec=grid_spec,
        out_shape=jax.ShapeDtypeStruct((n_tiles * tm, D), F32),
        compiler_params=_params("arbitrary", "arbitrary"),
        name="moe_experts",
    )(tile_expert, n_used, tok_of_slot, tok_of_slot, x, g.reshape(1, D), w_gate_up, w_gate_up, w_down)


def _moe_combine_kernel(s0_ref, s1_ref, s0n_ref, s1n_ref, x_ref, meta_ref, y_hbm, gp_ref, p_ref, wgate_ref,
                        wproj_ref, o_ref, ya0, ya1, yb0, yb1, sems):
    i = pl.program_id(0)
    n = ya0.shape[0]

    def gather(idx0, idx1, buf0, buf1, k, off):
        _start_row_gathers(idx0, y_hbm, buf0, sems.at[k, 0], 0, n, off)
        _start_row_gathers(idx1, y_hbm, buf1, sems.at[k, 1], 0, n, off)

    def wait(buf0, buf1, k):
        _wait_row_gathers(y_hbm, buf0, sems.at[k, 0])
        _wait_row_gathers(y_hbm, buf1, sems.at[k, 1])

    def combine(buf0, buf1, rows):
        meta = meta_ref[rows, :]
        w0 = meta[:, META_W0:META_W0 + 1]
        w1 = meta[:, META_W1:META_W1 + 1]
        h = x_ref[rows, :] + w0 * buf0[...] + w1 * buf1[...]
        gate = jax.nn.sigmoid(_dot(_rms(h, gp_ref[...]).astype(BF16), wgate_ref[...]))
        o_ref[rows, :] = h + gate * _dot(p_ref[rows, :].astype(BF16), wproj_ref[...])

    @pl.when(i == 0)
    def _():
        gather(s0_ref, s1_ref, ya0, ya1, 0, 0)

    wait(ya0, ya1, 0)
    gather(s0_ref, s1_ref, yb0, yb1, 1, n)
    combine(ya0, ya1, slice(0, n))
    wait(yb0, yb1, 1)
    gather(s0n_ref, s1n_ref, ya0, ya1, 0, 0)
    combine(yb0, yb1, slice(n, 2 * n))

    @pl.when(i == pl.num_programs(0) - 1)
    def _():
        wait(ya0, ya1, 0)


def moe_combine_ple(x, meta, slot0, slot1, y, g_ple, p_all, layer, w_pgate, w_pproj, *, tm):
    T, D = x.shape
    PD = p_all.shape[2]
    nt = T // tm
    idx = lambda off: pl.BlockSpec((1, 1, tm), lambda i: (jnp.minimum(i + off, nt - 1), 0, 0),
                                   memory_space=pltpu.SMEM)
    return pl.pallas_call(
        _moe_combine_kernel,
        grid=(nt,),
        in_specs=[idx(0), idx(0), idx(1), idx(1),
                  pl.BlockSpec((tm, D), lambda i: (i, 0)),
                  pl.BlockSpec((tm, LANES), lambda i: (i, 0)),
                  pl.BlockSpec(memory_space=pl.ANY),
                  pl.BlockSpec((1, D), lambda i: (0, 0)),
                  pl.BlockSpec((None, tm, PD), lambda i: (layer, i, 0)),
                  pl.BlockSpec((D, D), lambda i: (0, 0)),
                  pl.BlockSpec((PD, D), lambda i: (0, 0))],
        out_specs=pl.BlockSpec((tm, D), lambda i: (i, 0)),
        out_shape=jax.ShapeDtypeStruct((T, D), F32),
        scratch_shapes=[pltpu.VMEM((tm // 2, D), F32)] * 4 + [pltpu.SemaphoreType.DMA((2, 2))],
        compiler_params=_params("arbitrary"),
        name="moe_combine_ple",
    )(slot0, slot1, slot0, slot1, x, meta, y, g_ple.reshape(1, D), p_all, w_pgate, w_pproj)


def moe_layer(x, g, w_router, w_gate_up, w_down, layer, ple, *, tm_route, tm_expert, tm_combine):
    T, D = x.shape
    E = w_down.shape[1]
    n_tiles = (2 * T) // tm_expert + E
    meta, cnt = router_topk(x, g, w_router, tm=tm_route)
    e0 = meta[:, META_E0].astype(jnp.int32)
    e1 = meta[:, META_E1].astype(jnp.int32)
    r0 = meta[:, META_R0].astype(jnp.int32)
    r1 = meta[:, META_R1].astype(jnp.int32)
    counts = cnt[0, :E].astype(jnp.int32)
    tiles_per = (counts + tm_expert - 1) // tm_expert
    tile_end = jnp.cumsum(tiles_per)
    group_start = (tile_end - tiles_per) * tm_expert
    slot0 = group_start[e0] + r0
    slot1 = group_start[e1] + r1
    n_used = tile_end[-1:]
    tile_ids = jnp.arange(n_tiles, dtype=jnp.int32)
    tile_expert = jnp.searchsorted(tile_end, jnp.minimum(tile_ids, n_used[0] - 1), side="right").astype(jnp.int32)
    tile_expert = jnp.minimum(tile_expert, E - 1)
    tok = jnp.arange(T, dtype=jnp.int32)
    tok_of_slot = jnp.zeros((n_tiles * tm_expert,), jnp.int32).at[slot0].set(tok).at[slot1].set(tok)
    slot0 = slot0.reshape(T // tm_combine, 1, tm_combine)
    slot1 = slot1.reshape(T // tm_combine, 1, tm_combine)
    y = moe_expert_ffn(x, g, tok_of_slot.reshape(n_tiles, 1, tm_expert), tile_expert, n_used.astype(jnp.int32),
                       w_gate_up, w_down, layer, tm=tm_expert)
    return moe_combine_ple(x, meta, slot0, slot1, y, *ple, tm=tm_combine)


def _row_tile(S, want):
    return min(want, S)


def retention_layer(h, tabs, B, S, g_norm, w_in, gn, w_out):
    tm = _row_tile(S, 1024)
    dk = w_in.shape[1] // (6 * RET_HEADS)
    qk_cols = 2 * RET_HEADS * dk
    proj = norm_matmul_rope(h, g_norm, w_in.astype(BF16), tabs["ret_cos"], tabs["ret_sin"], tm=tm, tn=qk_cols,
                            rope_cols=qk_cols, dk=dk, out_dtype=BF16)
    return retention_core(proj, gn, w_out.astype(BF16), h, B, S)


def nsa_layer(h, tabs, B, S, g_norm, w_in, q_norm, k_norm, cmp_pos, cmp_w1, cmp_w2, w_out):
    D, n_in = w_in.shape
    n_pad = -(-n_in // (7 * LANES)) * (7 * LANES)
    w_in_p = jnp.zeros((D, n_pad), BF16).at[:, :n_in].set(w_in.astype(BF16))
    H, G = NSA_HEADS, NSA_GROUPS
    n_blocks = n_pad // LANES
    block_scale = [None] * n_blocks
    gains = jnp.zeros((-(-n_blocks // 8) * 8, NSA_DH), F32)
    for c in range(H):
        block_scale[c] = NSA_DH ** -0.5 * LOG2_E
    gains = gains.at[0:H].set(jnp.broadcast_to(q_norm, (H, NSA_DH)))
    for branch, first in ((1, H + 2 * G), (2, H + 4 * G)):
        for c in range(first, first + G):
            block_scale[c] = 1.0
        gains = gains.at[first:first + G].set(jnp.broadcast_to(k_norm[branch], (G, NSA_DH)))
    proj = norm_matmul_headnorm(h, g_norm, w_in_p, tabs["nsa_cos"], tabs["nsa_sin"], gains, block_scale,
                                tm=_row_tile(S, 1024), tn=7 * LANES, out_dtype=BF16)
    kn = jnp.zeros((8, NSA_DH), F32).at[:3].set(k_norm)
    w1 = cmp_w1.astype(BF16).reshape(2, NSA_CMP_LEN, NSA_DH, cmp_w1.shape[-1])
    kcc, vcc = nsa_compress(proj, cmp_pos, w1, cmp_w2.astype(BF16), kn, tabs["nsa_cos"], tabs["nsa_sin"], B, S)
    o = nsa_attention(proj, kcc, vcc, B, S)
    return matmul_res(o, w_out.astype(BF16), h, tm=_row_tile(S, 512))


def s5_layer(h, B, S, g_norm, w_in, a_re, a_im, b_re, b_im, c_re, c_im, d_skip, log_dt, w_glu):
    D = h.shape[1]
    tm = _row_tile(S, 1024)
    u = norm_matmul(h, g_norm, w_in.astype(BF16), tm=tm, tn=D, out_dtype=BF16, time_major_batch=(B, S))
    y = s5_scan(u.reshape(S * B, D), a_re, a_im, b_re, b_im, c_re, c_im, d_skip, log_dt, B, S)
    return glu_res(y.reshape(S, B * D), w_glu.astype(BF16), h, B, S, tm=_row_tile(S, 512))


def kernel(x, p, positions, norm_mix, norm_ffn, norm_ple, ret_w_in, ret_gn, ret_w_out, nsa_w_in, nsa_q_norm, nsa_k_norm, nsa_cmp_pos, nsa_cmp_w1, nsa_cmp_w2, nsa_w_out, s5_w_in, s5_a_re, s5_a_im, s5_b_re, s5_b_im, s5_c_re, s5_c_im, s5_d, s5_log_dt, s5_w_glu, ffn_w_gate_up, ffn_w_down, moe_router, moe_w_gate_up, moe_w_down, ple_w_proj, ple_w_gate):
    B, S, D = x.shape
    depth = p.shape[0]
    T = B * S
    h = x.reshape(T, D)
    pos_col = positions.reshape(T, 1)
    p_all = p.reshape(depth, T, p.shape[-1])
    moe_gu, moe_dn = moe_w_gate_up, moe_w_down

    ones = jnp.ones((1, LANES), F32)
    dk = D // RET_HEADS
    inv_ret = (ROPE_THETA ** (-jnp.arange(0, dk, 2, dtype=F32) / dk)).reshape(1, LANES)
    inv_half = ROPE_THETA ** (-jnp.arange(0, NSA_DH, 2, dtype=F32) / NSA_DH)
    inv_nsa = jnp.concatenate([inv_half, inv_half]).reshape(1, LANES)
    sign_nsa = jnp.concatenate([-jnp.ones((NSA_DH // 2,), F32), jnp.ones((NSA_DH // 2,), F32)]).reshape(1, LANES)
    tabs = {}
    tabs["ret_cos"], tabs["ret_sin"] = rope_tables(pos_col, inv_ret, ones, _row_tile(T, 1024))
    if depth > 1:
        tabs["nsa_cos"], tabs["nsa_sin"] = rope_tables(pos_col, inv_nsa, sign_nsa, _row_tile(T, 1024))

    for i in range(depth):
        m, j = i % 3, i // 3
        if m == 0:
            h = retention_layer(h, tabs, B, S, norm_mix[i], ret_w_in[j], ret_gn[j], ret_w_out[j])
        elif m == 1:
            h = nsa_layer(h, tabs, B, S, norm_mix[i], nsa_w_in[j], nsa_q_norm[j], nsa_k_norm[j],
                          nsa_cmp_pos[j], nsa_cmp_w1[j], nsa_cmp_w2[j], nsa_w_out[j])
        else:
            h = s5_layer(h, B, S, norm_mix[i], s5_w_in[j], s5_a_re[j], s5_a_im[j], s5_b_re[j], s5_b_im[j],
                         s5_c_re[j], s5_c_im[j], s5_d[j], s5_log_dt[j], s5_w_glu[j])
        tm_ffn = _row_tile(T, 1024)
        ple = (norm_ple[i], p_all, i, ple_w_gate[i].astype(BF16), ple_w_proj[i].astype(BF16))
        if i % 2 == 0:
            h = swiglu_ffn_ple(h, norm_ffn[i], ffn_w_gate_up[i // 2].astype(BF16),
                               ffn_w_down[i // 2].astype(BF16), *ple, tm=tm_ffn, tf=256)
        else:
            h = moe_layer(h, norm_ffn[i], moe_router[i // 2], moe_gu, moe_dn, i // 2, ple,
                          tm_route=tm_ffn, tm_expert=tm_ffn, tm_combine=tm_ffn)
    return h.reshape(B, S, D)
```
